```python
import math
import jax, jax.numpy as jnp
from jax import lax
import numpy as np

D_MODEL = 1024
BATCH = 8
SEQ = 4096
DEPTH = 1

CHUNK = 64
Q_BLOCK = 128
D_FF = 2816
FFN_RES_WEIGHT = 0.5
ADA_SUBLAYERS = 3
ADA_WIDTH = ADA_SUBLAYERS * 3 * D_MODEL
MLA_HEADS = 8
MLA_Q_RANK = 256
MLA_KV_RANK = 128
MLA_NOPE = 64
MLA_ROPE = 32
MLA_V = 64
ROPE_THETA = 10000.0
CA_HEADS = 8
CA_HEAD_DIM = 64
CA_LEFT_CHUNKS = 8
CA_BAND = CA_LEFT_CHUNKS + 1
MAX_REL_DIST = 256
CA_WIDTH = CA_HEADS * CA_HEAD_DIM
MLA_OUT_WIDTH = MLA_HEADS * MLA_V
W_IN_COLS = MLA_Q_RANK + MLA_KV_RANK + MLA_ROPE + 3 * CA_WIDTH + 2 * D_MODEL
EPS = 1e-6
NEG_INF = -1e30

kernel_name = "hybrid_mla_chunkattn_macaron_adaln"


def rmsnorm(x, g):
    xf = x.astype(jnp.float32)
    y = xf * lax.rsqrt(jnp.mean(xf * xf, axis=-1, keepdims=True) + EPS)
    return (y * g.astype(jnp.float32)).astype(x.dtype)


def modulate(h, shift, scale):
    return h * (1 + scale[:, None, :]) + shift[:, None, :]


def swiglu(h, w_in, w_out):
    gu = h @ w_in
    g, u = jnp.split(gu, 2, axis=-1)
    return (jax.nn.silu(g) * u) @ w_out


def rope(x, cos, sin):
    half = x.shape[-1] // 2
    x1, x2 = x[..., :half], x[..., half:]
    return jnp.concatenate([x1 * cos - x2 * sin, x1 * sin + x2 * cos], axis=-1).astype(x.dtype)


def mla_attention(q_lat, kv_lat, k_pe_raw, positions, q_norm, w_uq, kv_norm, w_ukv):
    B, S = q_lat.shape[:2]
    cq = rmsnorm(q_lat, q_norm)
    q = (cq @ w_uq).reshape(B, S, MLA_HEADS, MLA_NOPE + MLA_ROPE)
    q_nope, q_pe = q[..., :MLA_NOPE], q[..., MLA_NOPE:]
    ckv = rmsnorm(kv_lat, kv_norm)
    kv = (ckv @ w_ukv).reshape(B, S, MLA_HEADS, MLA_NOPE + MLA_V)
    k_nope, v = kv[..., :MLA_NOPE], kv[..., MLA_NOPE:]

    inv_freq = ROPE_THETA ** (-jnp.arange(0, MLA_ROPE, 2, dtype=jnp.float32) / MLA_ROPE)
    ang = positions.astype(jnp.float32)[..., None] * inv_freq
    cos, sin = jnp.cos(ang), jnp.sin(ang)
    q_pe = rope(q_pe, cos[:, :, None, :], sin[:, :, None, :])
    k_pe = rope(k_pe_raw, cos, sin)

    scale = (MLA_NOPE + MLA_ROPE) ** -0.5
    key_chunk = jnp.arange(S) // CHUNK

    def block(qi):
        start = qi * Q_BLOCK
        qn = lax.dynamic_slice_in_dim(q_nope, start, Q_BLOCK, axis=1)
        qp = lax.dynamic_slice_in_dim(q_pe, start, Q_BLOCK, axis=1)
        s = (jnp.einsum('bqhd,bkhd->bhqk', qn, k_nope)
             + jnp.einsum('bqhd,bkd->bhqk', qp, k_pe)).astype(jnp.float32) * scale
        q_chunk = (start + jnp.arange(Q_BLOCK)) // CHUNK
        mask = key_chunk[None, :] <= q_chunk[:, None]
        s = jnp.where(mask[None, None], s, NEG_INF)
        p = jax.nn.softmax(s, axis=-1).astype(v.dtype)
        return jnp.einsum('bhqk,bkhd->bqhd', p, v)

    o = lax.map(block, jnp.arange(S // Q_BLOCK))
    return jnp.moveaxis(o, 0, 1).reshape(B, S, MLA_OUT_WIDTH)


def chunk_attention(q, k, v, rel_bias):
    B, S = q.shape[:2]
    NC = S // CHUNK
    KB = CA_BAND * CHUNK
    q = q.reshape(B, NC, CHUNK, CA_HEADS, CA_HEAD_DIM)
    pad = ((0, 0), (CA_LEFT_CHUNKS, 0), (0, 0), (0, 0), (0, 0))
    kpad = jnp.pad(k.reshape(B, NC, CHUNK, CA_HEADS, CA_HEAD_DIM), pad)
    vpad = jnp.pad(v.reshape(B, NC, CHUNK, CA_HEADS, CA_HEAD_DIM), pad)
    idx = jnp.arange(NC)[:, None] + jnp.arange(CA_BAND)[None, :]
    kb = kpad[:, idx].reshape(B, NC, KB, CA_HEADS, CA_HEAD_DIM)
    vb = vpad[:, idx].reshape(B, NC, KB, CA_HEADS, CA_HEAD_DIM)

    s = jnp.einsum('bnqhd,bnkhd->bnhqk', q, kb).astype(jnp.float32) * CA_HEAD_DIM ** -0.5
    rel = CA_LEFT_CHUNKS * CHUNK + jnp.arange(CHUNK)[:, None] - jnp.arange(KB)[None, :]
    rel = jnp.clip(rel, -MAX_REL_DIST, MAX_REL_DIST) + MAX_REL_DIST
    bias = jnp.transpose(rel_bias[rel], (2, 0, 1)).astype(jnp.float32)
    s = s + bias[None, None]
    key_chunk = jnp.arange(NC)[:, None] - CA_LEFT_CHUNKS + jnp.arange(KB)[None, :] // CHUNK
    valid = key_chunk >= 0
    s = jnp.where(valid[None, :, None, None, :], s, NEG_INF)
    p = jax.nn.softmax(s, axis=-1).astype(vb.dtype)
    o = jnp.einsum('bnhqk,bnkhd->bnqhd', p, vb)
    return o.reshape(B, S, CA_WIDTH)


def setup_inputs(seed: int = 0) -> dict:
    key = jax.random.key(seed)
    ks = jax.random.split(key, 24)
    L, D = DEPTH, D_MODEL

    def w(k, shape, fan_in, mult=1.0):
        return jax.random.normal(k, shape, jnp.float32) * (mult * fan_in ** -0.5)

    def gain(k, shape):
        return 1.0 + 0.1 * jax.random.normal(k, shape, jnp.float32)

    offsets = jax.random.randint(ks[2], (BATCH, 1), 0, 64, dtype=jnp.int32) * CHUNK
    positions = offsets + jnp.arange(SEQ, dtype=jnp.int32)[None, :]
    return {
        "x": jax.random.normal(ks[0], (BATCH, SEQ, D), jnp.float32),
        "c": jax.random.normal(ks[1], (BATCH, D), jnp.float32),
        "positions": positions,
        "w_ada": w(ks[3], (L, D, ADA_WIDTH), D),
        "b_ada": 0.1 * jax.random.normal(ks[4], (L, ADA_WIDTH), jnp.float32),
        "ffn1_norm": gain(ks[5], (L, D)),
        "ffn1_w_in": w(ks[6], (L, D, 2 * D_FF), D),
        "ffn1_w_out": w(ks[7], (L, D_FF, D), D_FF),
        "mix_norm": gain(ks[8], (L, D)),
        "w_in": w(ks[9], (L, D, W_IN_COLS), D),
        "mla_q_norm": gain(ks[10], (L, MLA_Q_RANK)),
        "mla_w_uq": w(ks[11], (L, MLA_Q_RANK, MLA_HEADS * (MLA_NOPE + MLA_ROPE)), MLA_Q_RANK),
        "mla_kv_norm": gain(ks[12], (L, MLA_KV_RANK)),
        "mla_w_ukv": w(ks[13], (L, MLA_KV_RANK, MLA_HEADS * (MLA_NOPE + MLA_V)), MLA_KV_RANK),
        "rel_bias": 0.5 * jax.random.normal(ks[14], (L, 2 * MAX_REL_DIST + 1, CA_HEADS), jnp.float32),
        "w_branch_a": w(ks[15], (L, MLA_OUT_WIDTH, D), MLA_OUT_WIDTH),
        "w_branch_b": w(ks[16], (L, CA_WIDTH, D), CA_WIDTH),
        "w_out": w(ks[17], (L, D, D), D),
        "ffn2_norm": gain(ks[18], (L, D)),
        "ffn2_w_in": w(ks[19], (L, D, 2 * D_FF), D),
        "ffn2_w_out": w(ks[20], (L, D_FF, D), D_FF),
        "final_norm": gain(ks[21], (D,)),
    }


def reference(x, c, positions, w_ada, b_ada, ffn1_norm, ffn1_w_in, ffn1_w_out,
              mix_norm, w_in, mla_q_norm, mla_w_uq, mla_kv_norm, mla_w_ukv, rel_bias,
              w_branch_a, w_branch_b, w_out, ffn2_norm, ffn2_w_in, ffn2_w_out, final_norm):
    c_act = jax.nn.silu(c)
    splits = []
    acc = 0
    for width in (MLA_Q_RANK, MLA_KV_RANK, MLA_ROPE, CA_WIDTH, CA_WIDTH, CA_WIDTH, D_MODEL):
        acc += width
        splits.append(acc)

    for l in range(DEPTH):
        ada = c_act @ w_ada[l] + b_ada[l]
        (sh1, sc1, g1, sh2, sc2, g2, sh3, sc3, g3) = jnp.split(ada, 3 * ADA_SUBLAYERS, axis=-1)

        h = modulate(rmsnorm(x, ffn1_norm[l]), sh1, sc1)
        x = x + FFN_RES_WEIGHT * g1[:, None, :] * swiglu(h, ffn1_w_in[l], ffn1_w_out[l])

        h = modulate(rmsnorm(x, mix_norm[l]), sh2, sc2)
        z = h @ w_in[l]
        q_lat, kv_lat, k_pe_raw, ca_q, ca_k, ca_v, gate_a, gate_b = jnp.split(z, splits, axis=-1)
        y_a = mla_attention(q_lat, kv_lat, k_pe_raw, positions, mla_q_norm[l], mla_w_uq[l],
                            mla_kv_norm[l], mla_w_ukv[l]) @ w_branch_a[l]
        B, S = ca_q.shape[:2]
        shp = (B, S, CA_HEADS, CA_HEAD_DIM)
        y_b = chunk_attention(ca_q.reshape(shp), ca_k.reshape(shp), ca_v.reshape(shp),
                              rel_bias[l]) @ w_branch_b[l]
        merged = jax.nn.sigmoid(gate_a) * y_a + jax.nn.sigmoid(gate_b) * y_b
        x = x + g2[:, None, :] * (merged @ w_out[l])

        h = modulate(rmsnorm(x, ffn2_norm[l]), sh3, sc3)
        x = x + FFN_RES_WEIGHT * g3[:, None, :] * swiglu(h, ffn2_w_in[l], ffn2_w_out[l])

    return rmsnorm(x, final_norm)
```

```python
import functools

import jax
import jax.numpy as jnp
from jax import lax
from jax.experimental import pallas as pl
from jax.experimental.pallas import tpu as pltpu

D_MODEL = 1024
CHUNK = 64
D_FF = 2816
FFN_RES_WEIGHT = 0.5
MLA_HEADS = 8
MLA_Q_RANK = 256
MLA_KV_RANK = 128
MLA_NOPE = 64
MLA_ROPE = 32
MLA_V = 64
ROPE_THETA = 10000.0
CA_HEADS = 8
CA_HEAD_DIM = 64
CA_LEFT_CHUNKS = 8
MAX_REL_DIST = 256
CA_WIDTH = CA_HEADS * CA_HEAD_DIM
MLA_OUT_WIDTH = MLA_HEADS * MLA_V
EPS = 1e-6
NEG_INF = -1e30

LANES = 128
HEAD_SLAB = LANES
VMEM_LIMIT_BYTES = 56 * 1024 * 1024

ZC_QLAT = 0
ZC_KVLAT = ZC_QLAT + MLA_Q_RANK
ZC_KPE_A = ZC_KVLAT + MLA_KV_RANK
ZC_KPE_B = ZC_KPE_A + LANES
ZC_CAQ = ZC_KPE_B + LANES
ZC_CAK = ZC_CAQ + CA_WIDTH
ZC_CAV = ZC_CAK + CA_WIDTH
ZC_GA = ZC_CAV + CA_WIDTH
ZC_GB = ZC_GA + D_MODEL
Z_COLS = ZC_GB + D_MODEL

FFN_CHUNKS = ((0, 1024), (1024, 1024), (2048, 768))

BF16 = jnp.bfloat16
F32 = jnp.float32


def _const_spec(shape):
    nd = len(shape)
    return pl.BlockSpec(shape, lambda *_: (0,) * nd, pipeline_mode=pl.Buffered(1))


def _rmsnorm(x, g):
    return x * lax.rsqrt(jnp.mean(x * x, axis=-1, keepdims=True) + EPS) * g


def _dot(a, b):
    return jnp.dot(a, b, preferred_element_type=F32)


def _dot_nt(a, b):
    return lax.dot_general(a, b, (((1,), (1,)), ((), ())), preferred_element_type=F32)


def _ada_kernel(c_ref, w_ref, b_ref, o_ref):
    c = c_ref[...]
    c_act = c * jax.nn.sigmoid(c)
    o_ref[...] = jnp.dot(c_act, w_ref[...], preferred_element_type=F32,
                         precision=lax.Precision.HIGHEST) + b_ref[...]


def _ada(c, w_ada, b_ada):
    bsz, d = c.shape
    n = w_ada.shape[1]
    bn = 1152
    return pl.pallas_call(
        _ada_kernel,
        grid=(n // bn,),
        in_specs=[pl.BlockSpec((bsz, d), lambda j: (0, 0)),
                  pl.BlockSpec((d, bn), lambda j: (0, j)),
                  pl.BlockSpec((1, bn), lambda j: (0, j))],
        out_specs=pl.BlockSpec((bsz, bn), lambda j: (0, j)),
        out_shape=jax.ShapeDtypeStruct((bsz, n), F32),
        name="ada",
    )(c, w_ada, b_ada.reshape(1, n))


def _ffn_body(x, ada_ref, ada_base, norm_ref, w_in_ref, w_out_ref):
    shift = ada_ref[0, ada_base:ada_base + 1, :]
    scale = ada_ref[0, ada_base + 1:ada_base + 2, :]
    gate = ada_ref[0, ada_base + 2:ada_base + 3, :]
    h = (_rmsnorm(x, norm_ref[...]) * (1.0 + scale) + shift).astype(BF16)
    acc = None
    for c0, cw in FFN_CHUNKS:
        g = _dot(h, w_in_ref[:, c0:c0 + cw])
        u = _dot(h, w_in_ref[:, D_FF + c0:D_FF + c0 + cw])
        a = (g * jax.nn.sigmoid(g) * u).astype(BF16)
        part = _dot(a, w_out_ref[c0:c0 + cw, :])
        acc = part if acc is None else acc + part
    return x + (FFN_RES_WEIGHT * gate) * acc


def _ffn1_kernel(x_ref, ada_ref, norm_ref, w_in_ref, w_out_ref, o_ref):
    o_ref[0] = _ffn_body(x_ref[0], ada_ref, 0, norm_ref, w_in_ref, w_out_ref)


def _ffn1(x, ada, norm, w_in, w_out, tm):
    bsz, s, d = x.shape
    row = pl.BlockSpec((1, tm, d), lambda b, i: (b, i, 0))
    return pl.pallas_call(
        _ffn1_kernel,
        grid=(bsz, s // tm),
        in_specs=[row,
                  pl.BlockSpec((1, 9, d), lambda b, i: (b, 0, 0)),
                  _const_spec((1, d)),
                  _const_spec(w_in.shape),
                  _const_spec(w_out.shape)],
        out_specs=row,
        out_shape=jax.ShapeDtypeStruct(x.shape, F32),
        compiler_params=pltpu.CompilerParams(
            dimension_semantics=("parallel", "parallel"),
            vmem_limit_bytes=VMEM_LIMIT_BYTES),
        name="ffn1",
    )(x, ada, norm, w_in, w_out)


def _proj_kernel(x_ref, ada_ref, norm_ref, pos_ref, invf_ref, w_in_ref, qn_ref, kvn_ref,
                 wqa_ref, wqb_ref, wuk_ref, wuv_ref,
                 q_ref, k_ref, v_ref, caq_ref, cak_ref, cav_ref, ga_ref, gb_ref):
    x = x_ref[0]
    shift = ada_ref[0, 3:4, :]
    scale = ada_ref[0, 4:5, :]
    h = (_rmsnorm(x, norm_ref[...]) * (1.0 + scale) + shift).astype(BF16)
    z = _dot(h, w_in_ref[...])

    lane = lax.broadcasted_iota(jnp.int32, (1, LANES), 1)
    ang = pos_ref[0].astype(F32) * invf_ref[...]
    is_rope = (lane >= MLA_NOPE) & (lane < MLA_NOPE + MLA_ROPE)
    cos_t = jnp.where(lane < MLA_NOPE, 1.0, jnp.where(is_rope, jnp.cos(ang), 0.0))
    sin_t = jnp.where(is_rope, jnp.sin(ang), 0.0)

    cq = _rmsnorm(z[:, ZC_QLAT:ZC_QLAT + MLA_Q_RANK], qn_ref[...]).astype(BF16)
    qa = _dot(cq, wqa_ref[...])
    qb = _dot(cq, wqb_ref[...])
    qk_scale = (MLA_NOPE + MLA_ROPE) ** -0.5
    ckv = _rmsnorm(z[:, ZC_KVLAT:ZC_KVLAT + MLA_KV_RANK], kvn_ref[...]).astype(BF16)
    kn = _dot(ckv, wuk_ref[...])
    kpe = z[:, ZC_KPE_A:ZC_KPE_A + LANES] * cos_t + z[:, ZC_KPE_B:ZC_KPE_B + LANES] * sin_t
    for hd in range(MLA_HEADS):
        sl = slice(hd * HEAD_SLAB, (hd + 1) * HEAD_SLAB)
        q_ref[0, :, sl] = ((qa[:, sl] * cos_t + qb[:, sl] * sin_t) * qk_scale).astype(BF16)
        k_ref[0, :, sl] = (kn[:, sl] + kpe).astype(BF16)
    v_ref[0] = _dot(ckv, wuv_ref[...]).astype(BF16)

    caq_ref[0] = (z[:, ZC_CAQ:ZC_CAQ + CA_WIDTH] * CA_HEAD_DIM ** -0.5).astype(BF16)
    cak_ref[0] = z[:, ZC_CAK:ZC_CAK + CA_WIDTH].astype(BF16)
    cav_ref[0] = z[:, ZC_CAV:ZC_CAV + CA_WIDTH].astype(BF16)
    ga_ref[0] = jax.nn.sigmoid(z[:, ZC_GA:ZC_GA + D_MODEL]).astype(BF16)
    gb_ref[0] = jax.nn.sigmoid(z[:, ZC_GB:ZC_GB + D_MODEL]).astype(BF16)


def _proj(x, ada, norm, pos, invf, w_in_ext, qn, kvn, wqa, wqb, wuk, wuv, tm):
    bsz, s, d = x.shape

    def row(width):
        return pl.BlockSpec((1, tm, width), lambda b, i: (b, i, 0))

    def out(width):
        return jax.ShapeDtypeStruct((bsz, s, width), BF16)

    widths = (MLA_HEADS * HEAD_SLAB, MLA_HEADS * HEAD_SLAB, MLA_OUT_WIDTH,
              CA_WIDTH, CA_WIDTH, CA_WIDTH, D_MODEL, D_MODEL)
    return pl.pallas_call(
        _proj_kernel,
        grid=(bsz, s // tm),
        in_specs=[row(d),
                  pl.BlockSpec((1, 9, d), lambda b, i: (b, 0, 0)),
                  _const_spec((1, d)),
                  row(1),
                  _const_spec((1, LANES)),
                  _const_spec(w_in_ext.shape),
                  _const_spec(qn.shape), _const_spec(kvn.shape),
                  _const_spec(wqa.shape), _const_spec(wqb.shape),
                  _const_spec(wuk.shape), _const_spec(wuv.shape)],
        out_specs=[row(w) for w in widths],
        out_shape=[out(w) for w in widths],
        compiler_params=pltpu.CompilerParams(
            dimension_semantics=("parallel", "parallel"),
            vmem_limit_bytes=VMEM_LIMIT_BYTES),
        name="proj",
    )(x, ada, norm, pos, invf, w_in_ext, qn, kvn, wqa, wqb, wuk, wuv)


def _mla_kernel(q_ref, k_ref, v_ref, o_ref, m_ref, l_ref, acc_ref, *, blk):
    qi = pl.program_id(2)
    lo = lax.broadcasted_iota(jnp.int32, (1, LANES), 1) < MLA_V
    m_ref[...] = jnp.full(m_ref.shape, NEG_INF, F32)
    l_ref[...] = jnp.zeros(l_ref.shape, F32)
    acc_ref[...] = jnp.zeros(acc_ref.shape, F32)

    def step(j, mask):
        k0 = pl.multiple_of(j * blk, blk)
        vp = v_ref[0, pl.ds(k0, blk), :]
        pvs, alphas = [], []
        for hh in range(2):
            sl = slice(hh * HEAD_SLAB, (hh + 1) * HEAD_SLAB)
            s = _dot_nt(q_ref[0, :, sl], k_ref[0, pl.ds(k0, blk), sl])
            if mask is not None:
                s = jnp.where(mask, s, NEG_INF)
            m_prev = m_ref[hh]
            m_new = jnp.maximum(m_prev, jnp.max(s, axis=1, keepdims=True))
            alpha = jnp.exp(m_prev - m_new)
            p = jnp.exp(s - m_new[:, 0:1])
            l_ref[hh] = alpha * l_ref[hh] + jnp.sum(p, axis=1, keepdims=True)
            m_ref[hh] = m_new
            pvs.append(_dot(p.astype(BF16), vp))
            alphas.append(alpha)
        acc_ref[...] = (acc_ref[...] * jnp.where(lo, alphas[0], alphas[1])
                        + jnp.where(lo, pvs[0], pvs[1]))

    def body(j, carry):
        step(j, None)
        return carry

    lax.fori_loop(0, qi, body, 0)
    rc = lax.broadcasted_iota(jnp.int32, (blk, blk), 0) // CHUNK
    cc = lax.broadcasted_iota(jnp.int32, (blk, blk), 1) // CHUNK
    step(qi, cc <= rc)
    inv = jnp.where(lo, 1.0 / l_ref[0], 1.0 / l_ref[1])
    o_ref[0] = (acc_ref[...] * inv).astype(BF16)


def _mla(q, k, v, blk):
    bsz, s, _ = q.shape
    pairs = MLA_HEADS // 2
    return pl.pallas_call(
        functools.partial(_mla_kernel, blk=blk),
        grid=(bsz, pairs, s // blk),
        in_specs=[pl.BlockSpec((1, blk, 2 * HEAD_SLAB), lambda b, p, i: (b, i, p)),
                  pl.BlockSpec((1, s, 2 * HEAD_SLAB), lambda b, p, i: (b, 0, p)),
                  pl.BlockSpec((1, s, LANES), lambda b, p, i: (b, 0, p))],
        out_specs=pl.BlockSpec((1, blk, LANES), lambda b, p, i: (b, i, p)),
        out_shape=jax.ShapeDtypeStruct((bsz, s, MLA_OUT_WIDTH), BF16),
        scratch_shapes=[pltpu.VMEM((2, blk, LANES), F32),
                        pltpu.VMEM((2, blk, LANES), F32),
                        pltpu.VMEM((blk, LANES), F32)],
        compiler_params=pltpu.CompilerParams(
            dimension_semantics=("parallel", "parallel", "arbitrary"),
            vmem_limit_bytes=VMEM_LIMIT_BYTES),
        name="mla",
    )(q, k, v)


CA_TQ = 256
CA_TK = CA_TQ + CA_LEFT_CHUNKS * CHUNK
CA_KBLKS = CA_TK // CA_TQ
CA_TAB = 1024


def _ca_kernel(tab_ref, q_ref, k0_ref, k1_ref, k2_ref, v0_ref, v1_ref, v2_ref, o_ref, bias_ref):
    b, hp, i = pl.program_id(0), pl.program_id(1), pl.program_id(2)

    @pl.when((b == 0) & (hp == 0) & (i == 0))
    def _build_bias():
        row = lax.broadcasted_iota(jnp.int32, (CA_TQ, CA_TAB), 0)
        qc = lax.broadcasted_iota(jnp.int32, (CA_TQ, CA_TK), 0) // CHUNK
        kc = lax.broadcasted_iota(jnp.int32, (CA_TQ, CA_TK), 1) // CHUNK
        band = (kc >= qc) & (kc <= qc + CA_LEFT_CHUNKS)
        for hd in range(CA_HEADS):
            t = jnp.broadcast_to(tab_ref[hd:hd + 1, :], (CA_TQ, CA_TAB))
            for bit in range(CA_TQ.bit_length() - 1):
                t = jnp.where(((row >> bit) & 1) == 1, pltpu.roll(t, 1 << bit, axis=1), t)
            bias_ref[hd] = jnp.where(band, t[:, :CA_TK], NEG_INF)

    lo = lax.broadcasted_iota(jnp.int32, (1, LANES), 1) < CA_HEAD_DIM
    kcat = jnp.concatenate([k0_ref[0], k1_ref[0], k2_ref[0]], axis=0)
    vcat = jnp.concatenate([v0_ref[0], v1_ref[0], v2_ref[0]], axis=0)
    q = q_ref[0]
    zero = jnp.zeros_like(q)
    col = lax.broadcasted_iota(jnp.int32, (1, CA_TK), 1)
    valid = col >= (CA_TK - CA_TQ) - i * CA_TQ
    outs = []
    for hh in range(2):
        qh = jnp.where(lo, q, zero) if hh == 0 else jnp.where(lo, zero, q)
        s = _dot_nt(qh, kcat) + bias_ref[2 * hp + hh]
        s = jnp.where(valid, s, NEG_INF)
        m = jnp.max(s, axis=1, keepdims=True)
        p = jnp.exp(s - m)
        l = jnp.sum(p, axis=1, keepdims=True)
        outs.append(_dot(p.astype(BF16), vcat) * (1.0 / l))
    o_ref[0] = jnp.where(lo, outs[0], outs[1]).astype(BF16)


def _ca(tab, q, k, v):
    bsz, s, _ = q.shape
    pairs = CA_HEADS // 2

    def kv_spec(back):
        return pl.BlockSpec((1, CA_TQ, LANES),
                            lambda b, p, i: (b, jnp.maximum(i - back, 0), p))

    blk = pl.BlockSpec((1, CA_TQ, LANES), lambda b, p, i: (b, i, p))
    return pl.pallas_call(
        _ca_kernel,
        grid=(bsz, pairs, s // CA_TQ),
        in_specs=[_const_spec(tab.shape), blk,
                  kv_spec(2), kv_spec(1), kv_spec(0),
                  kv_spec(2), kv_spec(1), kv_spec(0)],
        out_specs=blk,
        out_shape=jax.ShapeDtypeStruct((bsz, s, CA_WIDTH), BF16),
        scratch_shapes=[pltpu.VMEM((CA_HEADS, CA_TQ, CA_TK), F32)],
        compiler_params=pltpu.CompilerParams(
            dimension_semantics=("arbitrary", "arbitrary", "arbitrary"),
            vmem_limit_bytes=VMEM_LIMIT_BYTES),
        name="ca",
    )(tab, q, k, k, k, v, v, v)


def _tail_kernel(x_ref, ada_ref, aa_ref, ab_ref, ga_ref, gb_ref, wa_ref, wb_ref, wo_ref,
                 norm_ref, w_in_ref, w_out_ref, fnorm_ref, o_ref):
    ya = _dot(aa_ref[0], wa_ref[...])
    yb = _dot(ab_ref[0], wb_ref[...])
    merged = (ga_ref[0].astype(F32) * ya + gb_ref[0].astype(F32) * yb).astype(BF16)
    x = x_ref[0] + ada_ref[0, 5:6, :] * _dot(merged, wo_ref[...])
    x = _ffn_body(x, ada_ref, 6, norm_ref, w_in_ref, w_out_ref)
    o_ref[0] = _rmsnorm(x, fnorm_ref[...])


def _tail(x, ada, aa, ab, ga, gb, wa, wb, wo, norm, w_in, w_out, fnorm, tm):
    bsz, s, d = x.shape

    def row(width):
        return pl.BlockSpec((1, tm, width), lambda b, i: (b, i, 0))

    return pl.pallas_call(
        _tail_kernel,
        grid=(bsz, s // tm),
        in_specs=[row(d),
                  pl.BlockSpec((1, 9, d), lambda b, i: (b, 0, 0)),
                  row(MLA_OUT_WIDTH), row(CA_WIDTH), row(d), row(d),
                  _const_spec(wa.shape), _const_spec(wb.shape), _const_spec(wo.shape),
                  _const_spec((1, d)), _const_spec(w_in.shape), _const_spec(w_out.shape),
                  _const_spec((1, d))],
        out_specs=row(d),
        out_shape=jax.ShapeDtypeStruct(x.shape, F32),
        compiler_params=pltpu.CompilerParams(
            dimension_semantics=("parallel", "parallel"),
            vmem_limit_bytes=VMEM_LIMIT_BYTES),
        name="tail",
    )(x, ada, aa, ab, ga, gb, wa, wb, wo, norm, w_in, w_out, fnorm)


def _rotate_half_cols(w):
    half = w.shape[-1] // 2
    return jnp.concatenate([-w[..., half:], w[..., :half]], axis=-1)


def _prep_w_in(w_in):
    d = w_in.shape[0]
    o = 0
    pieces = {}
    for name, width in (("qlat", MLA_Q_RANK), ("kvlat", MLA_KV_RANK), ("kpe", MLA_ROPE),
                        ("caq", CA_WIDTH), ("cak", CA_WIDTH), ("cav", CA_WIDTH),
                        ("ga", D_MODEL), ("gb", D_MODEL)):
        pieces[name] = w_in[:, o:o + width]
        o += width
    zl = jnp.zeros((d, MLA_NOPE), w_in.dtype)
    zr = jnp.zeros((d, LANES - MLA_NOPE - MLA_ROPE), w_in.dtype)
    kpe_a = jnp.concatenate([zl, pieces["kpe"], zr], axis=1)
    kpe_b = jnp.concatenate([zl, _rotate_half_cols(pieces["kpe"]), zr], axis=1)
    ext = jnp.concatenate([pieces["qlat"], pieces["kvlat"], kpe_a, kpe_b, pieces["caq"],
                           pieces["cak"], pieces["cav"], pieces["ga"], pieces["gb"]], axis=1)
    assert ext.shape[1] == Z_COLS
    return ext.astype(BF16)


def _prep_w_uq(w_uq):
    r = w_uq.shape[0]
    w = w_uq.reshape(r, MLA_HEADS, MLA_NOPE + MLA_ROPE)
    nope, pe = w[..., :MLA_NOPE], w[..., MLA_NOPE:]
    zr = jnp.zeros((r, MLA_HEADS, HEAD_SLAB - MLA_NOPE - MLA_ROPE), w.dtype)
    a = jnp.concatenate([nope, pe, zr], axis=-1)
    b = jnp.concatenate([jnp.zeros_like(nope), _rotate_half_cols(pe), zr], axis=-1)
    return (a.reshape(r, MLA_HEADS * HEAD_SLAB).astype(BF16),
            b.reshape(r, MLA_HEADS * HEAD_SLAB).astype(BF16))


def _prep_w_ukv(w_ukv):
    r = w_ukv.shape[0]
    w = w_ukv.reshape(r, MLA_HEADS, MLA_NOPE + MLA_V)
    k_nope, v = w[..., :MLA_NOPE], w[..., MLA_NOPE:]
    zk = jnp.zeros((r, MLA_HEADS, HEAD_SLAB - MLA_NOPE), w.dtype)
    wuk = jnp.concatenate([k_nope, zk], axis=-1).reshape(r, MLA_HEADS * HEAD_SLAB)
    return wuk.astype(BF16), v.reshape(r, MLA_OUT_WIDTH).astype(BF16)


def _prep_bias_table(rel_bias):
    j = jnp.arange(CA_TAB)
    t = (j + CA_TQ - 1) % CA_TAB
    rel = (CA_TK - 1) - t
    idx = jnp.clip(rel, -MAX_REL_DIST, MAX_REL_DIST) + MAX_REL_DIST
    return rel_bias[idx].T.astype(F32)


def kernel(x, c, positions, w_ada, b_ada, ffn1_norm, ffn1_w_in, ffn1_w_out, mix_norm, w_in,
           mla_q_norm, mla_w_uq, mla_kv_norm, mla_w_ukv, rel_bias, w_branch_a, w_branch_b,
           w_out, ffn2_norm, ffn2_w_in, ffn2_w_out, final_norm):
    bsz, s, d = x.shape
    tm = 512
    for l in range(w_ada.shape[0]):
        ada = _ada(c, w_ada[l], b_ada[l]).reshape(bsz, 9, d)
        x = _ffn1(x, ada, ffn1_norm[l].reshape(1, d), ffn1_w_in[l].astype(BF16),
                  ffn1_w_out[l].astype(BF16), tm)

        freq = jnp.arange(0, MLA_ROPE, 2, dtype=F32) / MLA_ROPE
        inv_freq = ROPE_THETA ** (-freq)
        invf = jnp.concatenate([jnp.zeros((MLA_NOPE,), F32), inv_freq, inv_freq,
                                jnp.zeros((LANES - MLA_NOPE - MLA_ROPE,), F32)]).reshape(1, LANES)
        wqa, wqb = _prep_w_uq(mla_w_uq[l])
        wuk, wuv = _prep_w_ukv(mla_w_ukv[l])
        q, k, v, caq, cak, cav, ga, gb = _proj(
            x, ada, mix_norm[l].reshape(1, d), positions.reshape(bsz, s, 1), invf,
            _prep_w_in(w_in[l]), mla_q_norm[l].reshape(1, -1), mla_kv_norm[l].reshape(1, -1),
            wqa, wqb, wuk, wuv, tm)
        attn_a = _mla(q, k, v, 512)
        attn_b = _ca(_prep_bias_table(rel_bias[l]), caq, cak, cav)
        last = l == w_ada.shape[0] - 1
        assert last, "the fused tail applies the final norm; DEPTH is 1"
        x = _tail(x, ada, attn_a, attn_b, ga, gb, w_branch_a[l].astype(BF16),
                  w_branch_b[l].astype(BF16), w_out[l].astype(BF16),
                  ffn2_norm[l].reshape(1, d), ffn2_w_in[l].astype(BF16),
                  ffn2_w_out[l].astype(BF16), final_norm.reshape(1, d), tm)
    return x
```

```python
import functools

import jax
import jax.numpy as jnp
from jax import lax
from jax.experimental import pallas as pl
from jax.experimental.pallas import tpu as pltpu

D_MODEL = 1024
CHUNK = 64
D_FF = 2816
FFN_RES_WEIGHT = 0.5
MLA_HEADS = 8
MLA_Q_RANK = 256
MLA_KV_RANK = 128
MLA_NOPE = 64
MLA_ROPE = 32
MLA_V = 64
ROPE_THETA = 10000.0
CA_HEADS = 8
CA_HEAD_DIM = 64
CA_LEFT_CHUNKS = 8
MAX_REL_DIST = 256
CA_WIDTH = CA_HEADS * CA_HEAD_DIM
MLA_OUT_WIDTH = MLA_HEADS * MLA_V
EPS = 1e-6
NEG_INF = -1e30
LOG2_E = 1.4426950408889634

LANES = 128
HEAD_SLAB = LANES
VMEM_LIMIT_BYTES = 56 * 1024 * 1024

ZC_QLAT = 0
ZC_KVLAT = ZC_QLAT + MLA_Q_RANK
ZC_KPE_A = ZC_KVLAT + MLA_KV_RANK
ZC_KPE_B = ZC_KPE_A + LANES
ZC_CAQ = ZC_KPE_B + LANES
ZC_CAK = ZC_CAQ + CA_WIDTH
ZC_CAV = ZC_CAK + CA_WIDTH
ZC_GA = ZC_CAV + CA_WIDTH
ZC_GB = ZC_GA + D_MODEL
Z_COLS = ZC_GB + D_MODEL

FFN_CHUNKS = ((0, 1024), (1024, 1024), (2048, 768))

BF16 = jnp.bfloat16
F32 = jnp.float32


def _const_spec(shape):
    nd = len(shape)
    return pl.BlockSpec(shape, lambda *_: (0,) * nd, pipeline_mode=pl.Buffered(1))


def _rmsnorm(x, g):
    return x * lax.rsqrt(jnp.mean(x * x, axis=-1, keepdims=True) + EPS) * g


def _dot(a, b):
    return jnp.dot(a, b, preferred_element_type=F32)


def _dot_nt(a, b):
    return lax.dot_general(a, b, (((1,), (1,)), ((), ())), preferred_element_type=F32)


def _ada_kernel(c_ref, w_ref, b_ref, o_ref):
    c = c_ref[...]
    c_act = c * jax.nn.sigmoid(c)
    o_ref[...] = jnp.dot(c_act, w_ref[...], preferred_element_type=F32,
                         precision=lax.Precision.HIGHEST) + b_ref[...]


def _ada(c, w_ada, b_ada):
    bsz, d = c.shape
    n = w_ada.shape[1]
    bn = 1152
    return pl.pallas_call(
        _ada_kernel,
        grid=(n // bn,),
        in_specs=[pl.BlockSpec((bsz, d), lambda j: (0, 0)),
                  pl.BlockSpec((d, bn), lambda j: (0, j)),
                  pl.BlockSpec((1, bn), lambda j: (0, j))],
        out_specs=pl.BlockSpec((bsz, bn), lambda j: (0, j)),
        out_shape=jax.ShapeDtypeStruct((bsz, n), F32),
        name="ada",
    )(c, w_ada, b_ada.reshape(1, n))


def _ffn_body(x, ada_ref, ada_base, norm_ref, w_in_ref, w_out_ref):
    shift = ada_ref[0, ada_base:ada_base + 1, :]
    scale = ada_ref[0, ada_base + 1:ada_base + 2, :]
    gate = ada_ref[0, ada_base + 2:ada_base + 3, :]
    h = (_rmsnorm(x, norm_ref[...]) * (1.0 + scale) + shift).astype(BF16)
    acc = None
    for c0, cw in FFN_CHUNKS:
        g = _dot(h, w_in_ref[:, c0:c0 + cw])
        u = _dot(h, w_in_ref[:, D_FF + c0:D_FF + c0 + cw])
        a = (g * jax.nn.sigmoid(g) * u).astype(BF16)
        part = _dot(a, w_out_ref[c0:c0 + cw, :])
        acc = part if acc is None else acc + part
    return x + (FFN_RES_WEIGHT * gate) * acc


def _ffn1_kernel(x_ref, ada_ref, norm_ref, w_in_ref, w_out_ref, o_ref):
    o_ref[0] = _ffn_body(x_ref[0], ada_ref, 0, norm_ref, w_in_ref, w_out_ref)


def _ffn1(x, ada, norm, w_in, w_out, tm):
    bsz, s, d = x.shape
    row = pl.BlockSpec((1, tm, d), lambda b, i: (b, i, 0))
    return pl.pallas_call(
        _ffn1_kernel,
        grid=(bsz, s // tm),
        in_specs=[row,
                  pl.BlockSpec((1, 9, d), lambda b, i: (b, 0, 0)),
                  _const_spec((1, d)),
                  _const_spec(w_in.shape),
                  _const_spec(w_out.shape)],
        out_specs=row,
        out_shape=jax.ShapeDtypeStruct(x.shape, F32),
        compiler_params=pltpu.CompilerParams(
            dimension_semantics=("parallel", "parallel"),
            vmem_limit_bytes=VMEM_LIMIT_BYTES),
        name="ffn1",
    )(x, ada, norm, w_in, w_out)


def _proj_kernel(x_ref, ada_ref, norm_ref, pos_ref, invf_ref, w_in_ref, qn_ref, kvn_ref,
                 wqa_ref, wqb_ref, wuk_ref, wuvt_ref, vone_ref,
                 q_ref, k_ref, vt_ref, caq_ref, cak_ref, cav_ref, ga_ref, gb_ref):
    x = x_ref[0]
    shift = ada_ref[0, 3:4, :]
    scale = ada_ref[0, 4:5, :]
    h = (_rmsnorm(x, norm_ref[...]) * (1.0 + scale) + shift).astype(BF16)
    z = _dot(h, w_in_ref[...])

    lane = lax.broadcasted_iota(jnp.int32, (1, LANES), 1)
    ang = pos_ref[0].astype(F32) * invf_ref[...]
    is_rope = (lane >= MLA_NOPE) & (lane < MLA_NOPE + MLA_ROPE)
    cos_t = jnp.where(lane < MLA_NOPE, 1.0, jnp.where(is_rope, jnp.cos(ang), 0.0))
    sin_t = jnp.where(is_rope, jnp.sin(ang), 0.0)

    cq = _rmsnorm(z[:, ZC_QLAT:ZC_QLAT + MLA_Q_RANK], qn_ref[...]).astype(BF16)
    qa = _dot(cq, wqa_ref[...])
    qb = _dot(cq, wqb_ref[...])
    qk_scale = (MLA_NOPE + MLA_ROPE) ** -0.5 * LOG2_E
    ckv = _rmsnorm(z[:, ZC_KVLAT:ZC_KVLAT + MLA_KV_RANK], kvn_ref[...]).astype(BF16)
    kn = _dot(ckv, wuk_ref[...])
    kpe = z[:, ZC_KPE_A:ZC_KPE_A + LANES] * cos_t + z[:, ZC_KPE_B:ZC_KPE_B + LANES] * sin_t
    for hd in range(MLA_HEADS):
        sl = slice(hd * HEAD_SLAB, (hd + 1) * HEAD_SLAB)
        q_ref[0, :, sl] = ((qa[:, sl] * cos_t + qb[:, sl] * sin_t) * qk_scale).astype(BF16)
        k_ref[0, :, sl] = (kn[:, sl] + kpe).astype(BF16)
    vt_ref[0] = (_dot_nt(wuvt_ref[...], ckv) + vone_ref[...]).astype(BF16)

    caq_ref[0] = (z[:, ZC_CAQ:ZC_CAQ + CA_WIDTH] * CA_HEAD_DIM ** -0.5).astype(BF16)
    cak_ref[0] = z[:, ZC_CAK:ZC_CAK + CA_WIDTH].astype(BF16)
    cav_ref[0] = z[:, ZC_CAV:ZC_CAV + CA_WIDTH].astype(BF16)
    ga_ref[0] = jax.nn.sigmoid(z[:, ZC_GA:ZC_GA + D_MODEL]).astype(BF16)
    gb_ref[0] = jax.nn.sigmoid(z[:, ZC_GB:ZC_GB + D_MODEL]).astype(BF16)


def _proj(x, ada, norm, pos, invf, w_in_ext, qn, kvn, wqa, wqb, wuk, wuvt, vone, tm):
    bsz, s, d = x.shape

    def row(width):
        return pl.BlockSpec((1, tm, width), lambda b, i: (b, i, 0))

    def out(width):
        return jax.ShapeDtypeStruct((bsz, s, width), BF16)

    slabs = MLA_HEADS * HEAD_SLAB
    widths = (slabs, slabs, None, CA_WIDTH, CA_WIDTH, CA_WIDTH, D_MODEL, D_MODEL)
    vt_spec = pl.BlockSpec((1, slabs, tm), lambda b, i: (b, 0, i))
    vt_shape = jax.ShapeDtypeStruct((bsz, slabs, s), BF16)
    return pl.pallas_call(
        _proj_kernel,
        grid=(bsz, s // tm),
        in_specs=[row(d),
                  pl.BlockSpec((1, 9, d), lambda b, i: (b, 0, 0)),
                  _const_spec((1, d)),
                  row(1),
                  _const_spec((1, LANES)),
                  _const_spec(w_in_ext.shape),
                  _const_spec(qn.shape), _const_spec(kvn.shape),
                  _const_spec(wqa.shape), _const_spec(wqb.shape),
                  _const_spec(wuk.shape), _const_spec(wuvt.shape), _const_spec(vone.shape)],
        out_specs=[vt_spec if w is None else row(w) for w in widths],
        out_shape=[vt_shape if w is None else out(w) for w in widths],
        compiler_params=pltpu.CompilerParams(
            dimension_semantics=("parallel", "parallel"),
            vmem_limit_bytes=VMEM_LIMIT_BYTES),
        name="proj",
    )(x, ada, norm, pos, invf, w_in_ext, qn, kvn, wqa, wqb, wuk, wuvt, vone)


MLA_HEADS_PER_STEP = 4
MLA_LOOKAHEAD = 2


def _mla_kernel(q_ref, k_ref, vt_ref, o_ref, m_ref, acc_ref, *, blk, hps):
    qi = pl.program_id(2)
    m_ref[...] = jnp.full(m_ref.shape, NEG_INF, F32)
    acc_ref[...] = jnp.zeros(acc_ref.shape, F32)

    def scores(hd, k0, mask):
        sl = slice(hd * HEAD_SLAB, (hd + 1) * HEAD_SLAB)
        s = _dot_nt(k_ref[0, pl.ds(k0, blk), sl], q_ref[0, :, sl])
        return s if mask is None else jnp.where(mask, s, NEG_INF)

    def step(j, mask):
        k0 = pl.multiple_of(j * blk, blk)
        pending = {hd: scores(hd, k0, mask) for hd in range(min(MLA_LOOKAHEAD, hps))}
        for hd in range(hps):
            s = pending.pop(hd)
            if hd + MLA_LOOKAHEAD < hps:
                pending[hd + MLA_LOOKAHEAD] = scores(hd + MLA_LOOKAHEAD, k0, mask)
            m_prev = m_ref[hd]
            m_new = jnp.maximum(m_prev, jnp.max(s, axis=0, keepdims=True))
            m_ref[hd] = m_new
            alpha = jnp.exp2(m_prev - m_new)
            p = jnp.exp2((s - m_new).astype(BF16))
            vt = vt_ref[0, hd * HEAD_SLAB:(hd + 1) * HEAD_SLAB, pl.ds(k0, blk)]
            acc_ref[hd] = acc_ref[hd] * alpha + _dot(vt, p)

    def body(j, carry):
        step(j, None)
        return carry

    lax.fori_loop(0, qi, body, 0)
    kc = lax.broadcasted_iota(jnp.int32, (blk, blk), 0) // CHUNK
    qc = lax.broadcasted_iota(jnp.int32, (blk, blk), 1) // CHUNK
    step(qi, kc <= qc)
    for pr in range(hps // 2):
        outs = []
        for hd in (2 * pr, 2 * pr + 1):
            a = acc_ref[hd]
            outs.append(a[:MLA_V] * (1.0 / a[MLA_V:MLA_V + 1]))
        o_ref[0, :, pr * LANES:(pr + 1) * LANES] = jnp.transpose(
            jnp.concatenate(outs, axis=0)).astype(BF16)


def _mla(q, k, vt, blk):
    bsz, s, _ = q.shape
    hps = MLA_HEADS_PER_STEP
    return pl.pallas_call(
        functools.partial(_mla_kernel, blk=blk, hps=hps),
        grid=(bsz, MLA_HEADS // hps, s // blk),
        in_specs=[pl.BlockSpec((1, blk, hps * HEAD_SLAB), lambda b, g, i: (b, i, g)),
                  pl.BlockSpec((1, s, hps * HEAD_SLAB), lambda b, g, i: (b, 0, g)),
                  pl.BlockSpec((1, hps * HEAD_SLAB, s), lambda b, g, i: (b, g, 0))],
        out_specs=pl.BlockSpec((1, blk, hps * MLA_V), lambda b, g, i: (b, i, g)),
        out_shape=jax.ShapeDtypeStruct((bsz, s, MLA_OUT_WIDTH), BF16),
        scratch_shapes=[pltpu.VMEM((hps, 1, blk), F32),
                        pltpu.VMEM((hps, HEAD_SLAB, blk), F32)],
        compiler_params=pltpu.CompilerParams(
            dimension_semantics=("parallel", "parallel", "arbitrary"),
            vmem_limit_bytes=VMEM_LIMIT_BYTES),
        name="mla",
    )(q, k, vt)


CA_TQ = 256
CA_TK = CA_TQ + CA_LEFT_CHUNKS * CHUNK
CA_KBLKS = CA_TK // CA_TQ
CA_TAB = 1024


def _ca_kernel(tab_ref, q_ref, k0_ref, k1_ref, k2_ref, v0_ref, v1_ref, v2_ref, o_ref, bias_ref):
    b, hp, i = pl.program_id(0), pl.program_id(1), pl.program_id(2)

    @pl.when((b == 0) & (hp == 0) & (i == 0))
    def _build_bias():
        row = lax.broadcasted_iota(jnp.int32, (CA_TQ, CA_TAB), 0)
        qc = lax.broadcasted_iota(jnp.int32, (CA_TQ, CA_TK), 0) // CHUNK
        kc = lax.broadcasted_iota(jnp.int32, (CA_TQ, CA_TK), 1) // CHUNK
        band = (kc >= qc) & (kc <= qc + CA_LEFT_CHUNKS)
        for hd in range(CA_HEADS):
            t = jnp.broadcast_to(tab_ref[hd:hd + 1, :], (CA_TQ, CA_TAB))
            for bit in range(CA_TQ.bit_length() - 1):
                t = jnp.where(((row >> bit) & 1) == 1, pltpu.roll(t, 1 << bit, axis=1), t)
            bias_ref[hd] = jnp.where(band, t[:, :CA_TK], NEG_INF)

    lo = lax.broadcasted_iota(jnp.int32, (1, LANES), 1) < CA_HEAD_DIM
    kcat = jnp.concatenate([k0_ref[0], k1_ref[0], k2_ref[0]], axis=0)
    vcat = jnp.concatenate([v0_ref[0], v1_ref[0], v2_ref[0]], axis=0)
    q = q_ref[0]
    zero = jnp.zeros_like(q)
    col = lax.broadcasted_iota(jnp.int32, (1, CA_TK), 1)
    valid = col >= (CA_TK - CA_TQ) - i * CA_TQ
    outs = []
    for hh in range(2):
        qh = jnp.where(lo, q, zero) if hh == 0 else jnp.where(lo, zero, q)
        s = _dot_nt(qh, kcat) + bias_ref[2 * hp + hh]
        s = jnp.where(valid, s, NEG_INF)
        m = jnp.max(s, axis=1, keepdims=True)
        p = jnp.exp(s - m)
        l = jnp.sum(p, axis=1, keepdims=True)
        outs.append(_dot(p.astype(BF16), vcat) * (1.0 / l))
    o_ref[0] = jnp.where(lo, outs[0], outs[1]).astype(BF16)


def _ca(tab, q, k, v):
    bsz, s, _ = q.shape
    pairs = CA_HEADS // 2

    def kv_spec(back):
        return pl.BlockSpec((1, CA_TQ, LANES),
                            lambda b, p, i: (b, jnp.maximum(i - back, 0), p))

    blk = pl.BlockSpec((1, CA_TQ, LANES), lambda b, p, i: (b, i, p))
    return pl.pallas_call(
        _ca_kernel,
        grid=(bsz, pairs, s // CA_TQ),
        in_specs=[_const_spec(tab.shape), blk,
                  kv_spec(2), kv_spec(1), kv_spec(0),
                  kv_spec(2), kv_spec(1), kv_spec(0)],
        out_specs=blk,
        out_shape=jax.ShapeDtypeStruct((bsz, s, CA_WIDTH), BF16),
        scratch_shapes=[pltpu.VMEM((CA_HEADS, CA_TQ, CA_TK), F32)],
        compiler_params=pltpu.CompilerParams(
            dimension_semantics=("arbitrary", "arbitrary", "arbitrary"),
            vmem_limit_bytes=VMEM_LIMIT_BYTES),
        name="ca",
    )(tab, q, k, k, k, v, v, v)


def _tail_kernel(x_ref, ada_ref, aa_ref, ab_ref, ga_ref, gb_ref, wa_ref, wb_ref, wo_ref,
                 norm_ref, w_in_ref, w_out_ref, fnorm_ref, o_ref):
    ya = _dot(aa_ref[0], wa_ref[...])
    yb = _dot(ab_ref[0], wb_ref[...])
    merged = (ga_ref[0].astype(F32) * ya + gb_ref[0].astype(F32) * yb).astype(BF16)
    x = x_ref[0] + ada_ref[0, 5:6, :] * _dot(merged, wo_ref[...])
    x = _ffn_body(x, ada_ref, 6, norm_ref, w_in_ref, w_out_ref)
    o_ref[0] = _rmsnorm(x, fnorm_ref[...])


def _tail(x, ada, aa, ab, ga, gb, wa, wb, wo, norm, w_in, w_out, fnorm, tm):
    bsz, s, d = x.shape

    def row(width):
        return pl.BlockSpec((1, tm, width), lambda b, i: (b, i, 0))

    return pl.pallas_call(
        _tail_kernel,
        grid=(bsz, s // tm),
        in_specs=[row(d),
                  pl.BlockSpec((1, 9, d), lambda b, i: (b, 0, 0)),
                  row(MLA_OUT_WIDTH), row(CA_WIDTH), row(d), row(d),
                  _const_spec(wa.shape), _const_spec(wb.shape), _const_spec(wo.shape),
                  _const_spec((1, d)), _const_spec(w_in.shape), _const_spec(w_out.shape),
                  _const_spec((1, d))],
        out_specs=row(d),
        out_shape=jax.ShapeDtypeStruct(x.shape, F32),
        compiler_params=pltpu.CompilerParams(
            dimension_semantics=("parallel", "parallel"),
            vmem_limit_bytes=VMEM_LIMIT_BYTES),
        name="tail",
    )(x, ada, aa, ab, ga, gb, wa, wb, wo, norm, w_in, w_out, fnorm)


def _rotate_half_cols(w):
    half = w.shape[-1] // 2
    return jnp.concatenate([-w[..., half:], w[..., :half]], axis=-1)


def _prep_w_in(w_in):
    d = w_in.shape[0]
    o = 0
    pieces = {}
    for name, width in (("qlat", MLA_Q_RANK), ("kvlat", MLA_KV_RANK), ("kpe", MLA_ROPE),
                        ("caq", CA_WIDTH), ("cak", CA_WIDTH), ("cav", CA_WIDTH),
                        ("ga", D_MODEL), ("gb", D_MODEL)):
        pieces[name] = w_in[:, o:o + width]
        o += width
    zl = jnp.zeros((d, MLA_NOPE), w_in.dtype)
    zr = jnp.zeros((d, LANES - MLA_NOPE - MLA_ROPE), w_in.dtype)
    kpe_a = jnp.concatenate([zl, pieces["kpe"], zr], axis=1)
    kpe_b = jnp.concatenate([zl, _rotate_half_cols(pieces["kpe"]), zr], axis=1)
    ext = jnp.concatenate([pieces["qlat"], pieces["kvlat"], kpe_a, kpe_b, pieces["caq"],
                           pieces["cak"], pieces["cav"], pieces["ga"], pieces["gb"]], axis=1)
    assert ext.shape[1] == Z_COLS
    return ext.astype(BF16)


def _prep_w_uq(w_uq):
    r = w_uq.shape[0]
    w = w_uq.reshape(r, MLA_HEADS, MLA_NOPE + MLA_ROPE)
    nope, pe = w[..., :MLA_NOPE], w[..., MLA_NOPE:]
    zr = jnp.zeros((r, MLA_HEADS, HEAD_SLAB - MLA_NOPE - MLA_ROPE), w.dtype)
    a = jnp.concatenate([nope, pe, zr], axis=-1)
    b = jnp.concatenate([jnp.zeros_like(nope), _rotate_half_cols(pe), zr], axis=-1)
    return (a.reshape(r, MLA_HEADS * HEAD_SLAB).astype(BF16),
            b.reshape(r, MLA_HEADS * HEAD_SLAB).astype(BF16))


def _prep_w_ukv(w_ukv):
    r = w_ukv.shape[0]
    w = w_ukv.reshape(r, MLA_HEADS, MLA_NOPE + MLA_V)
    k_nope, v = w[..., :MLA_NOPE], w[..., MLA_NOPE:]
    zk = jnp.zeros((r, MLA_HEADS, HEAD_SLAB - MLA_NOPE), w.dtype)
    wuk = jnp.concatenate([k_nope, zk], axis=-1).reshape(r, MLA_HEADS * HEAD_SLAB)
    zv = jnp.zeros((r, MLA_HEADS, HEAD_SLAB - MLA_V), w.dtype)
    wuvt = jnp.concatenate([v, zv], axis=-1).reshape(r, MLA_HEADS * HEAD_SLAB).T
    vone = (jnp.arange(MLA_HEADS * HEAD_SLAB) % HEAD_SLAB == MLA_V).astype(F32).reshape(-1, 1)
    return wuk.astype(BF16), wuvt.astype(BF16), vone


def _prep_bias_table(rel_bias):
    j = jnp.arange(CA_TAB)
    t = (j + CA_TQ - 1) % CA_TAB
    rel = (CA_TK - 1) - t
    idx = jnp.clip(rel, -MAX_REL_DIST, MAX_REL_DIST) + MAX_REL_DIST
    return rel_bias[idx].T.astype(F32)


def kernel(x, c, positions, w_ada, b_ada, ffn1_norm, ffn1_w_in, ffn1_w_out, mix_norm, w_in,
           mla_q_norm, mla_w_uq, mla_kv_norm, mla_w_ukv, rel_bias, w_branch_a, w_branch_b,
           w_out, ffn2_norm, ffn2_w_in, ffn2_w_out, final_norm):
    bsz, s, d = x.shape
    tm = 512
    for l in range(w_ada.shape[0]):
        ada = _ada(c, w_ada[l], b_ada[l]).reshape(bsz, 9, d)
        x = _ffn1(x, ada, ffn1_norm[l].reshape(1, d), ffn1_w_in[l].astype(BF16),
                  ffn1_w_out[l].astype(BF16), tm)

        freq = jnp.arange(0, MLA_ROPE, 2, dtype=F32) / MLA_ROPE
        inv_freq = ROPE_THETA ** (-freq)
        invf = jnp.concatenate([jnp.zeros((MLA_NOPE,), F32), inv_freq, inv_freq,
                                jnp.zeros((LANES - MLA_NOPE - MLA_ROPE,), F32)]).reshape(1, LANES)
        wqa, wqb = _prep_w_uq(mla_w_uq[l])
        wuk, wuvt, vone = _prep_w_ukv(mla_w_ukv[l])
        q, k, vt, caq, cak, cav, ga, gb = _proj(
            x, ada, mix_norm[l].reshape(1, d), positions.reshape(bsz, s, 1), invf,
            _prep_w_in(w_in[l]), mla_q_norm[l].reshape(1, -1), mla_kv_norm[l].reshape(1, -1),
            wqa, wqb, wuk, wuvt, vone, tm)
        attn_a = _mla(q, k, vt, 512)
        attn_b = _ca(_prep_bias_table(rel_bias[l]), caq, cak, cav)
        last = l == w_ada.shape[0] - 1
        assert last, "the fused tail applies the final norm; DEPTH is 1"
        x = _tail(x, ada, attn_a, attn_b, ga, gb, w_branch_a[l].astype(BF16),
                  w_branch_b[l].astype(BF16), w_out[l].astype(BF16),
                  ffn2_norm[l].reshape(1, d), ffn2_w_in[l].astype(BF16),
                  ffn2_w_out[l].astype(BF16), final_norm.reshape(1, d), tm)
    return x
```

```python
import functools

import jax
import jax.numpy as jnp
from jax import lax
from jax.experimental import pallas as pl
from jax.experimental.pallas import tpu as pltpu

D_MODEL = 1024
CHUNK = 64
D_FF = 2816
FFN_RES_WEIGHT = 0.5
MLA_HEADS = 8
MLA_Q_RANK = 256
MLA_KV_RANK = 128
MLA_NOPE = 64
MLA_ROPE = 32
MLA_V = 64
ROPE_THETA = 10000.0
CA_HEADS = 8
CA_HEAD_DIM = 64
CA_LEFT_CHUNKS = 8
MAX_REL_DIST = 256
CA_WIDTH = CA_HEADS * CA_HEAD_DIM
MLA_OUT_WIDTH = MLA_HEADS * MLA_V
EPS = 1e-6
NEG_INF = -1e30
LOG2_E = 1.4426950408889634

LANES = 128
HEAD_SLAB = LANES
VMEM_LIMIT_BYTES = 56 * 1024 * 1024

ZC_QLAT = 0
ZC_KVLAT = ZC_QLAT + MLA_Q_RANK
ZC_KPE_A = ZC_KVLAT + MLA_KV_RANK
ZC_KPE_B = ZC_KPE_A + LANES
ZC_CAQ = ZC_KPE_B + LANES
ZC_CAK = ZC_CAQ + CA_WIDTH
ZC_CAV = ZC_CAK + CA_WIDTH
ZC_GA = ZC_CAV + CA_WIDTH
ZC_GB = ZC_GA + D_MODEL
Z_COLS = ZC_GB + D_MODEL

FFN_CHUNKS = ((0, 1024), (1024, 1024), (2048, 768))

BF16 = jnp.bfloat16
F32 = jnp.float32


def _const_spec(shape):
    nd = len(shape)
    return pl.BlockSpec(shape, lambda *_: (0,) * nd, pipeline_mode=pl.Buffered(1))


def _rmsnorm(x, g):
    return x * lax.rsqrt(jnp.mean(x * x, axis=-1, keepdims=True) + EPS) * g


def _dot(a, b):
    return jnp.dot(a, b, preferred_element_type=F32)


def _dot_nt(a, b):
    return lax.dot_general(a, b, (((1,), (1,)), ((), ())), preferred_element_type=F32)


def _ada_kernel(c_ref, w_ref, b_ref, o_ref):
    c = c_ref[...]
    c_act = c * jax.nn.sigmoid(c)
    o_ref[...] = jnp.dot(c_act, w_ref[...], preferred_element_type=F32,
                         precision=lax.Precision.HIGHEST) + b_ref[...]


def _ada(c, w_ada, b_ada):
    bsz, d = c.shape
    n = w_ada.shape[1]
    bn = 1152
    return pl.pallas_call(
        _ada_kernel,
        grid=(n // bn,),
        in_specs=[pl.BlockSpec((bsz, d), lambda j: (0, 0)),
                  pl.BlockSpec((d, bn), lambda j: (0, j)),
                  pl.BlockSpec((1, bn), lambda j: (0, j))],
        out_specs=pl.BlockSpec((bsz, bn), lambda j: (0, j)),
        out_shape=jax.ShapeDtypeStruct((bsz, n), F32),
        name="ada",
    )(c, w_ada, b_ada.reshape(1, n))


def _ffn_body(x, ada_ref, ada_base, norm_ref, w_in_ref, w_out_ref):
    shift = ada_ref[0, ada_base:ada_base + 1, :]
    scale = ada_ref[0, ada_base + 1:ada_base + 2, :]
    gate = ada_ref[0, ada_base + 2:ada_base + 3, :]
    h = (_rmsnorm(x, norm_ref[...]) * (1.0 + scale) + shift).astype(BF16)
    acc = None
    for c0, cw in FFN_CHUNKS:
        g = _dot(h, w_in_ref[:, c0:c0 + cw])
        u = _dot(h, w_in_ref[:, D_FF + c0:D_FF + c0 + cw])
        a = (g * jax.nn.sigmoid(g) * u).astype(BF16)
        part = _dot(a, w_out_ref[c0:c0 + cw, :])
        acc = part if acc is None else acc + part
    return x + (FFN_RES_WEIGHT * gate) * acc


def _ffn1_kernel(x_ref, ada_ref, norm_ref, w_in_ref, w_out_ref, o_ref):
    o_ref[0] = _ffn_body(x_ref[0], ada_ref, 0, norm_ref, w_in_ref, w_out_ref)


def _ffn1(x, ada, norm, w_in, w_out, tm):
    bsz, s, d = x.shape
    row = pl.BlockSpec((1, tm, d), lambda b, i: (b, i, 0))
    return pl.pallas_call(
        _ffn1_kernel,
        grid=(bsz, s // tm),
        in_specs=[row,
                  pl.BlockSpec((1, 9, d), lambda b, i: (b, 0, 0)),
                  _const_spec((1, d)),
                  _const_spec(w_in.shape),
                  _const_spec(w_out.shape)],
        out_specs=row,
        out_shape=jax.ShapeDtypeStruct(x.shape, F32),
        compiler_params=pltpu.CompilerParams(
            dimension_semantics=("parallel", "parallel"),
            vmem_limit_bytes=VMEM_LIMIT_BYTES),
        name="ffn1",
    )(x, ada, norm, w_in, w_out)


def _proj_kernel(x_ref, ada_ref, norm_ref, pos_ref, invf_ref, w_in_ref, qn_ref, kvn_ref,
                 wqa_ref, wqb_ref, wuk_ref, wuvt_ref, vone_ref,
                 q_ref, k_ref, vt_ref, caq_ref, cak_ref, cavt_ref, ga_ref, gb_ref):
    x = x_ref[0]
    shift = ada_ref[0, 3:4, :]
    scale = ada_ref[0, 4:5, :]
    h = (_rmsnorm(x, norm_ref[...]) * (1.0 + scale) + shift).astype(BF16)
    z = _dot(h, w_in_ref[...])

    lane = lax.broadcasted_iota(jnp.int32, (1, LANES), 1)
    ang = pos_ref[0].astype(F32) * invf_ref[...]
    is_rope = (lane >= MLA_NOPE) & (lane < MLA_NOPE + MLA_ROPE)
    cos_t = jnp.where(lane < MLA_NOPE, 1.0, jnp.where(is_rope, jnp.cos(ang), 0.0))
    sin_t = jnp.where(is_rope, jnp.sin(ang), 0.0)

    cq = _rmsnorm(z[:, ZC_QLAT:ZC_QLAT + MLA_Q_RANK], qn_ref[...]).astype(BF16)
    qa = _dot(cq, wqa_ref[...])
    qb = _dot(cq, wqb_ref[...])
    qk_scale = (MLA_NOPE + MLA_ROPE) ** -0.5 * LOG2_E
    ckv = _rmsnorm(z[:, ZC_KVLAT:ZC_KVLAT + MLA_KV_RANK], kvn_ref[...]).astype(BF16)
    kn = _dot(ckv, wuk_ref[...])
    kpe = z[:, ZC_KPE_A:ZC_KPE_A + LANES] * cos_t + z[:, ZC_KPE_B:ZC_KPE_B + LANES] * sin_t
    for hd in range(MLA_HEADS):
        sl = slice(hd * HEAD_SLAB, (hd + 1) * HEAD_SLAB)
        q_ref[0, :, sl] = ((qa[:, sl] * cos_t + qb[:, sl] * sin_t) * qk_scale).astype(BF16)
        k_ref[0, :, sl] = (kn[:, sl] + kpe).astype(BF16)
    vt_ref[0] = (_dot_nt(wuvt_ref[...], ckv) + vone_ref[...]).astype(BF16)

    caq_ref[0] = (z[:, ZC_CAQ:ZC_CAQ + CA_WIDTH] * (CA_HEAD_DIM ** -0.5 * LOG2_E)).astype(BF16)
    cak_ref[0] = z[:, ZC_CAK:ZC_CAK + CA_WIDTH].astype(BF16)
    cavt = jnp.transpose(z[:, ZC_CAV:ZC_CAV + CA_WIDTH])
    tm = cavt.shape[1]
    ones_pad = (lax.broadcasted_iota(jnp.int32, (HEAD_SLAB - CA_HEAD_DIM, tm), 0) == 0).astype(F32)
    for hd in range(CA_HEADS):
        slab = jnp.concatenate([cavt[hd * CA_HEAD_DIM:(hd + 1) * CA_HEAD_DIM], ones_pad], axis=0)
        cavt_ref[0, hd * HEAD_SLAB:(hd + 1) * HEAD_SLAB, :] = slab.astype(BF16)
    ga_ref[0] = jax.nn.sigmoid(z[:, ZC_GA:ZC_GA + D_MODEL]).astype(BF16)
    gb_ref[0] = jax.nn.sigmoid(z[:, ZC_GB:ZC_GB + D_MODEL]).astype(BF16)


def _proj(x, ada, norm, pos, invf, w_in_ext, qn, kvn, wqa, wqb, wuk, wuvt, vone, tm):
    bsz, s, d = x.shape

    def row(width):
        return pl.BlockSpec((1, tm, width), lambda b, i: (b, i, 0))

    def out(width):
        return jax.ShapeDtypeStruct((bsz, s, width), BF16)

    slabs = MLA_HEADS * HEAD_SLAB
    widths = (slabs, slabs, None, CA_WIDTH, CA_WIDTH, None, D_MODEL, D_MODEL)
    vt_spec = pl.BlockSpec((1, slabs, tm), lambda b, i: (b, 0, i))
    vt_shape = jax.ShapeDtypeStruct((bsz, slabs, s), BF16)
    return pl.pallas_call(
        _proj_kernel,
        grid=(bsz, s // tm),
        in_specs=[row(d),
                  pl.BlockSpec((1, 9, d), lambda b, i: (b, 0, 0)),
                  _const_spec((1, d)),
                  row(1),
                  _const_spec((1, LANES)),
                  _const_spec(w_in_ext.shape),
                  _const_spec(qn.shape), _const_spec(kvn.shape),
                  _const_spec(wqa.shape), _const_spec(wqb.shape),
                  _const_spec(wuk.shape), _const_spec(wuvt.shape), _const_spec(vone.shape)],
        out_specs=[vt_spec if w is None else row(w) for w in widths],
        out_shape=[vt_shape if w is None else out(w) for w in widths],
        compiler_params=pltpu.CompilerParams(
            dimension_semantics=("parallel", "parallel"),
            vmem_limit_bytes=VMEM_LIMIT_BYTES),
        name="proj",
    )(x, ada, norm, pos, invf, w_in_ext, qn, kvn, wqa, wqb, wuk, wuvt, vone)


MLA_HEADS_PER_STEP = 4
MLA_LOOKAHEAD = 2


def _mla_kernel(q_ref, k_ref, vt_ref, o_ref, m_ref, acc_ref, *, blk, hps):
    qi = pl.program_id(2)
    m_ref[...] = jnp.full(m_ref.shape, NEG_INF, F32)
    acc_ref[...] = jnp.zeros(acc_ref.shape, F32)

    def scores(hd, k0, mask):
        sl = slice(hd * HEAD_SLAB, (hd + 1) * HEAD_SLAB)
        s = _dot_nt(k_ref[0, pl.ds(k0, blk), sl], q_ref[0, :, sl])
        return s if mask is None else jnp.where(mask, s, NEG_INF)

    def step(j, mask):
        k0 = pl.multiple_of(j * blk, blk)
        pending = {hd: scores(hd, k0, mask) for hd in range(min(MLA_LOOKAHEAD, hps))}
        for hd in range(hps):
            s = pending.pop(hd)
            if hd + MLA_LOOKAHEAD < hps:
                pending[hd + MLA_LOOKAHEAD] = scores(hd + MLA_LOOKAHEAD, k0, mask)
            m_prev = m_ref[hd]
            m_new = jnp.maximum(m_prev, jnp.max(s, axis=0, keepdims=True))
            m_ref[hd] = m_new
            alpha = jnp.exp2(m_prev - m_new)
            p = jnp.exp2((s - m_new).astype(BF16))
            vt = vt_ref[0, hd * HEAD_SLAB:(hd + 1) * HEAD_SLAB, pl.ds(k0, blk)]
            acc_ref[hd] = acc_ref[hd] * alpha + _dot(vt, p)

    def body(j, carry):
        step(j, None)
        return carry

    lax.fori_loop(0, qi, body, 0)
    kc = lax.broadcasted_iota(jnp.int32, (blk, blk), 0) // CHUNK
    qc = lax.broadcasted_iota(jnp.int32, (blk, blk), 1) // CHUNK
    step(qi, kc <= qc)
    for pr in range(hps // 2):
        outs = []
        for hd in (2 * pr, 2 * pr + 1):
            a = acc_ref[hd]
            outs.append(a[:MLA_V] * (1.0 / a[MLA_V:MLA_V + 1]))
        o_ref[0, :, pr * LANES:(pr + 1) * LANES] = jnp.transpose(
            jnp.concatenate(outs, axis=0)).astype(BF16)


def _mla(q, k, vt, blk):
    bsz, s, _ = q.shape
    hps = MLA_HEADS_PER_STEP
    return pl.pallas_call(
        functools.partial(_mla_kernel, blk=blk, hps=hps),
        grid=(bsz, MLA_HEADS // hps, s // blk),
        in_specs=[pl.BlockSpec((1, blk, hps * HEAD_SLAB), lambda b, g, i: (b, i, g)),
                  pl.BlockSpec((1, s, hps * HEAD_SLAB), lambda b, g, i: (b, 0, g)),
                  pl.BlockSpec((1, hps * HEAD_SLAB, s), lambda b, g, i: (b, g, 0))],
        out_specs=pl.BlockSpec((1, blk, hps * MLA_V), lambda b, g, i: (b, i, g)),
        out_shape=jax.ShapeDtypeStruct((bsz, s, MLA_OUT_WIDTH), BF16),
        scratch_shapes=[pltpu.VMEM((hps, 1, blk), F32),
                        pltpu.VMEM((hps, HEAD_SLAB, blk), F32)],
        compiler_params=pltpu.CompilerParams(
            dimension_semantics=("parallel", "parallel", "arbitrary"),
            vmem_limit_bytes=VMEM_LIMIT_BYTES),
        name="mla",
    )(q, k, vt)


CA_TQ = 256
CA_TK = CA_TQ + CA_LEFT_CHUNKS * CHUNK
CA_KBLKS = CA_TK // CA_TQ
CA_TAB = 1024
CA_LOOKAHEAD = 3


def _ca_kernel(tab_ref, q_ref, *refs):
    k_refs, vt_refs = refs[:CA_KBLKS], refs[CA_KBLKS:2 * CA_KBLKS]
    o_ref, bias_ref = refs[2 * CA_KBLKS], refs[2 * CA_KBLKS + 1]
    b, i = pl.program_id(0), pl.program_id(1)

    @pl.when((b == 0) & (i == 0))
    def _build_bias():
        row = lax.broadcasted_iota(jnp.int32, (CA_TQ, CA_TAB), 0)
        kc = lax.broadcasted_iota(jnp.int32, (CA_TK, CA_TQ), 0) // CHUNK
        qc = lax.broadcasted_iota(jnp.int32, (CA_TK, CA_TQ), 1) // CHUNK
        band = (kc >= qc) & (kc <= qc + CA_LEFT_CHUNKS)
        for hd in range(CA_HEADS):
            t = jnp.broadcast_to(tab_ref[hd:hd + 1, :], (CA_TQ, CA_TAB))
            for bit in range(CA_TQ.bit_length() - 1):
                t = jnp.where(((row >> bit) & 1) == 1, pltpu.roll(t, 1 << bit, axis=1), t)
            bias_ref[hd] = jnp.where(band, jnp.transpose(t[:, :CA_TK]) * LOG2_E, NEG_INF)

    lo = lax.broadcasted_iota(jnp.int32, (1, LANES), 1) < CA_HEAD_DIM

    def attend(valid):
        def scores(hd):
            sl = slice((hd // 2) * LANES, (hd // 2 + 1) * LANES)
            q = q_ref[0, :, sl]
            zero = jnp.zeros_like(q)
            qh = jnp.where(lo, q, zero) if hd % 2 == 0 else jnp.where(lo, zero, q)
            kcat = jnp.concatenate([r[0, :, sl] for r in k_refs], axis=0)
            s = _dot_nt(kcat, qh) + bias_ref[hd]
            return s if valid is None else jnp.where(valid, s, NEG_INF)

        pending = {hd: scores(hd) for hd in range(CA_LOOKAHEAD)}
        outs = []
        for hd in range(CA_HEADS):
            s = pending.pop(hd)
            if hd + CA_LOOKAHEAD < CA_HEADS:
                pending[hd + CA_LOOKAHEAD] = scores(hd + CA_LOOKAHEAD)
            m = jnp.max(s, axis=0, keepdims=True)
            p = jnp.exp2((s - m).astype(BF16))
            rows = slice(hd * HEAD_SLAB, (hd + 1) * HEAD_SLAB)
            vt = jnp.concatenate([r[0, rows, :] for r in vt_refs], axis=1)
            a = _dot(vt, p)
            outs.append(a[:CA_HEAD_DIM] * (1.0 / a[CA_HEAD_DIM:CA_HEAD_DIM + 1]))
            if hd % 2 == 1:
                pr = hd // 2
                o_ref[0, :, pr * LANES:(pr + 1) * LANES] = jnp.transpose(
                    jnp.concatenate(outs, axis=0)).astype(BF16)
                outs = []

    @pl.when(i >= CA_KBLKS - 1)
    def _interior():
        attend(None)

    @pl.when(i < CA_KBLKS - 1)
    def _sequence_start():
        krow = lax.broadcasted_iota(jnp.int32, (CA_TK, 1), 0)
        attend(krow >= (CA_TK - CA_TQ) - i * CA_TQ)


def _ca(tab, q, k, vt):
    bsz, s, _ = q.shape

    def k_spec(back):
        return pl.BlockSpec((1, CA_TQ, CA_WIDTH), lambda b, i: (b, jnp.maximum(i - back, 0), 0))

    def vt_spec(back):
        return pl.BlockSpec((1, CA_HEADS * HEAD_SLAB, CA_TQ),
                            lambda b, i: (b, 0, jnp.maximum(i - back, 0)))

    blk = pl.BlockSpec((1, CA_TQ, CA_WIDTH), lambda b, i: (b, i, 0))
    backs = tuple(range(CA_KBLKS - 1, -1, -1))
    return pl.pallas_call(
        _ca_kernel,
        grid=(bsz, s // CA_TQ),
        in_specs=[_const_spec(tab.shape), blk] + [k_spec(x) for x in backs]
                 + [vt_spec(x) for x in backs],
        out_specs=blk,
        out_shape=jax.ShapeDtypeStruct((bsz, s, CA_WIDTH), BF16),
        scratch_shapes=[pltpu.VMEM((CA_HEADS, CA_TK, CA_TQ), F32)],
        compiler_params=pltpu.CompilerParams(
            dimension_semantics=("arbitrary", "arbitrary"),
            vmem_limit_bytes=VMEM_LIMIT_BYTES),
        name="ca",
    )(tab, q, *([k] * CA_KBLKS), *([vt] * CA_KBLKS))


def _tail_kernel(x_ref, ada_ref, aa_ref, ab_ref, ga_ref, gb_ref, wa_ref, wb_ref, wo_ref,
                 norm_ref, w_in_ref, w_out_ref, fnorm_ref, o_ref):
    ya = _dot(aa_ref[0], wa_ref[...])
    yb = _dot(ab_ref[0], wb_ref[...])
    merged = (ga_ref[0].astype(F32) * ya + gb_ref[0].astype(F32) * yb).astype(BF16)
    x = x_ref[0] + ada_ref[0, 5:6, :] * _dot(merged, wo_ref[...])
    x = _ffn_body(x, ada_ref, 6, norm_ref, w_in_ref, w_out_ref)
    o_ref[0] = _rmsnorm(x, fnorm_ref[...])


def _tail(x, ada, aa, ab, ga, gb, wa, wb, wo, norm, w_in, w_out, fnorm, tm):
    bsz, s, d = x.shape

    def row(width):
        return pl.BlockSpec((1, tm, width), lambda b, i: (b, i, 0))

    return pl.pallas_call(
        _tail_kernel,
        grid=(bsz, s // tm),
        in_specs=[row(d),
                  pl.BlockSpec((1, 9, d), lambda b, i: (b, 0, 0)),
                  row(MLA_OUT_WIDTH), row(CA_WIDTH), row(d), row(d),
                  _const_spec(wa.shape), _const_spec(wb.shape), _const_spec(wo.shape),
                  _const_spec((1, d)), _const_spec(w_in.shape), _const_spec(w_out.shape),
                  _const_spec((1, d))],
        out_specs=row(d),
        out_shape=jax.ShapeDtypeStruct(x.shape, F32),
        compiler_params=pltpu.CompilerParams(
            dimension_semantics=("parallel", "parallel"),
            vmem_limit_bytes=VMEM_LIMIT_BYTES),
        name="tail",
    )(x, ada, aa, ab, ga, gb, wa, wb, wo, norm, w_in, w_out, fnorm)


def _rotate_half_cols(w):
    half = w.shape[-1] // 2
    return jnp.concatenate([-w[..., half:], w[..., :half]], axis=-1)


def _prep_w_in(w_in):
    d = w_in.shape[0]
    o = 0
    pieces = {}
    for name, width in (("qlat", MLA_Q_RANK), ("kvlat", MLA_KV_RANK), ("kpe", MLA_ROPE),
                        ("caq", CA_WIDTH), ("cak", CA_WIDTH), ("cav", CA_WIDTH),
                        ("ga", D_MODEL), ("gb", D_MODEL)):
        pieces[name] = w_in[:, o:o + width]
        o += width
    zl = jnp.zeros((d, MLA_NOPE), w_in.dtype)
    zr = jnp.zeros((d, LANES - MLA_NOPE - MLA_ROPE), w_in.dtype)
    kpe_a = jnp.concatenate([zl, pieces["kpe"], zr], axis=1)
    kpe_b = jnp.concatenate([zl, _rotate_half_cols(pieces["kpe"]), zr], axis=1)
    ext = jnp.concatenate([pieces["qlat"], pieces["kvlat"], kpe_a, kpe_b, pieces["caq"],
                           pieces["cak"], pieces["cav"], pieces["ga"], pieces["gb"]], axis=1)
    assert ext.shape[1] == Z_COLS
    return ext.astype(BF16)


def _prep_w_uq(w_uq):
    r = w_uq.shape[0]
    w = w_uq.reshape(r, MLA_HEADS, MLA_NOPE + MLA_ROPE)
    nope, pe = w[..., :MLA_NOPE], w[..., MLA_NOPE:]
    zr = jnp.zeros((r, MLA_HEADS, HEAD_SLAB - MLA_NOPE - MLA_ROPE), w.dtype)
    a = jnp.concatenate([nope, pe, zr], axis=-1)
    b = jnp.concatenate([jnp.zeros_like(nope), _rotate_half_cols(pe), zr], axis=-1)
    return (a.reshape(r, MLA_HEADS * HEAD_SLAB).astype(BF16),
            b.reshape(r, MLA_HEADS * HEAD_SLAB).astype(BF16))


def _prep_w_ukv(w_ukv):
    r = w_ukv.shape[0]
    w = w_ukv.reshape(r, MLA_HEADS, MLA_NOPE + MLA_V)
    k_nope, v = w[..., :MLA_NOPE], w[..., MLA_NOPE:]
    zk = jnp.zeros((r, MLA_HEADS, HEAD_SLAB - MLA_NOPE), w.dtype)
    wuk = jnp.concatenate([k_nope, zk], axis=-1).reshape(r, MLA_HEADS * HEAD_SLAB)
    zv = jnp.zeros((r, MLA_HEADS, HEAD_SLAB - MLA_V), w.dtype)
    wuvt = jnp.concatenate([v, zv], axis=-1).reshape(r, MLA_HEADS * HEAD_SLAB).T
    vone = (jnp.arange(MLA_HEADS * HEAD_SLAB) % HEAD_SLAB == MLA_V).astype(F32).reshape(-1, 1)
    return wuk.astype(BF16), wuvt.astype(BF16), vone


def _prep_bias_table(rel_bias):
    j = jnp.arange(CA_TAB)
    t = (j + CA_TQ - 1) % CA_TAB
    rel = (CA_TK - 1) - t
    idx = jnp.clip(rel, -MAX_REL_DIST, MAX_REL_DIST) + MAX_REL_DIST
    return rel_bias[idx].T.astype(F32)


def kernel(x, c, positions, w_ada, b_ada, ffn1_norm, ffn1_w_in, ffn1_w_out, mix_norm, w_in,
           mla_q_norm, mla_w_uq, mla_kv_norm, mla_w_ukv, rel_bias, w_branch_a, w_branch_b,
           w_out, ffn2_norm, ffn2_w_in, ffn2_w_out, final_norm):
    bsz, s, d = x.shape
    tm = 512
    for l in range(w_ada.shape[0]):
        ada = _ada(c, w_ada[l], b_ada[l]).reshape(bsz, 9, d)
        x = _ffn1(x, ada, ffn1_norm[l].reshape(1, d), ffn1_w_in[l].astype(BF16),
                  ffn1_w_out[l].astype(BF16), tm)

        freq = jnp.arange(0, MLA_ROPE, 2, dtype=F32) / MLA_ROPE
        inv_freq = ROPE_THETA ** (-freq)
        invf = jnp.concatenate([jnp.zeros((MLA_NOPE,), F32), inv_freq, inv_freq,
                                jnp.zeros((LANES - MLA_NOPE - MLA_ROPE,), F32)]).reshape(1, LANES)
        wqa, wqb = _prep_w_uq(mla_w_uq[l])
        wuk, wuvt, vone = _prep_w_ukv(mla_w_ukv[l])
        q, k, vt, caq, cak, cavt, ga, gb = _proj(
            x, ada, mix_norm[l].reshape(1, d), positions.reshape(bsz, s, 1), invf,
            _prep_w_in(w_in[l]), mla_q_norm[l].reshape(1, -1), mla_kv_norm[l].reshape(1, -1),
            wqa, wqb, wuk, wuvt, vone, tm)
        attn_a = _mla(q, k, vt, 512)
        attn_b = _ca(_prep_bias_table(rel_bias[l]), caq, cak, cavt)
        last = l == w_ada.shape[0] - 1
        assert last, "the fused tail applies the final norm; DEPTH is 1"
        x = _tail(x, ada, attn_a, attn_b, ga, gb, w_branch_a[l].astype(BF16),
                  w_branch_b[l].astype(BF16), w_out[l].astype(BF16),
                  ffn2_norm[l].reshape(1, d), ffn2_w_in[l].astype(BF16),
                  ffn2_w_out[l].astype(BF16), final_norm.reshape(1, d), tm)
    return x
```

```python
import functools

import jax
import jax.numpy as jnp
from jax import lax
from jax.experimental import pallas as pl
from jax.experimental.pallas import tpu as pltpu

D_MODEL = 1024
CHUNK = 64
D_FF = 2816
FFN_RES_WEIGHT = 0.5
MLA_HEADS = 8
MLA_Q_RANK = 256
MLA_KV_RANK = 128
MLA_NOPE = 64
MLA_ROPE = 32
MLA_V = 64
ROPE_THETA = 10000.0
CA_HEADS = 8
CA_HEAD_DIM = 64
CA_LEFT_CHUNKS = 8
MAX_REL_DIST = 256
CA_WIDTH = CA_HEADS * CA_HEAD_DIM
MLA_OUT_WIDTH = MLA_HEADS * MLA_V
EPS = 1e-6
NEG_INF = -1e30
LOG2_E = 1.4426950408889634

LANES = 128
HEAD_SLAB = LANES
VMEM_LIMIT_BYTES = 56 * 1024 * 1024

ZC_QLAT = 0
ZC_KVLAT = ZC_QLAT + MLA_Q_RANK
ZC_KPE = ZC_KVLAT + MLA_KV_RANK
ZC_CAQ = ZC_KPE + LANES
ZC_CAK = ZC_CAQ + CA_WIDTH
ZC_CAV = ZC_CAK + CA_WIDTH
ZC_GA = ZC_CAV + CA_WIDTH
ZC_GB = ZC_GA + D_MODEL
Z_COLS = ZC_GB + D_MODEL

FFN_CHUNKS = ((0, 1024), (1024, 1024), (2048, 768))

BF16 = jnp.bfloat16
F32 = jnp.float32


def _const_spec(shape):
    nd = len(shape)
    return pl.BlockSpec(shape, lambda *_: (0,) * nd, pipeline_mode=pl.Buffered(1))


def _rmsnorm(x, g):
    return x * lax.rsqrt(jnp.mean(x * x, axis=-1, keepdims=True) + EPS) * g


def _dot(a, b):
    return jnp.dot(a, b, preferred_element_type=F32)


def _dot_nt(a, b):
    return lax.dot_general(a, b, (((1,), (1,)), ((), ())), preferred_element_type=F32)


def _ada_kernel(c_ref, w_ref, b_ref, o_ref):
    c = c_ref[...]
    c_act = c * jax.nn.sigmoid(c)
    o_ref[...] = jnp.dot(c_act, w_ref[...], preferred_element_type=F32,
                         precision=lax.Precision.HIGHEST) + b_ref[...]


def _ada(c, w_ada, b_ada):
    bsz, d = c.shape
    n = w_ada.shape[1]
    bn = 1152
    return pl.pallas_call(
        _ada_kernel,
        grid=(n // bn,),
        in_specs=[pl.BlockSpec((bsz, d), lambda j: (0, 0)),
                  pl.BlockSpec((d, bn), lambda j: (0, j)),
                  pl.BlockSpec((1, bn), lambda j: (0, j))],
        out_specs=pl.BlockSpec((bsz, bn), lambda j: (0, j)),
        out_shape=jax.ShapeDtypeStruct((bsz, n), F32),
        name="ada",
    )(c, w_ada, b_ada.reshape(1, n))


def _ffn_body(x, ada_ref, ada_base, norm_ref, w_in_ref, w_out_ref):
    shift = ada_ref[0, ada_base:ada_base + 1, :]
    scale = ada_ref[0, ada_base + 1:ada_base + 2, :]
    gate = ada_ref[0, ada_base + 2:ada_base + 3, :]
    h = (_rmsnorm(x, norm_ref[...]) * (1.0 + scale) + shift).astype(BF16)
    acc = None
    for c0, cw in FFN_CHUNKS:
        g = _dot(h, w_in_ref[:, c0:c0 + cw])
        u = _dot(h, w_in_ref[:, D_FF + c0:D_FF + c0 + cw])
        a = (g * jax.nn.sigmoid(g) * u).astype(BF16)
        part = _dot(a, w_out_ref[c0:c0 + cw, :])
        acc = part if acc is None else acc + part
    return x + (FFN_RES_WEIGHT * gate) * acc


def _ffn1_kernel(x_ref, ada_ref, norm_ref, w_in_ref, w_out_ref, o_ref):
    o_ref[0] = _ffn_body(x_ref[0], ada_ref, 0, norm_ref, w_in_ref, w_out_ref)


def _ffn1(x, ada, norm, w_in, w_out, tm):
    bsz, s, d = x.shape
    row = pl.BlockSpec((1, tm, d), lambda b, i: (b, i, 0))
    return pl.pallas_call(
        _ffn1_kernel,
        grid=(bsz, s // tm),
        in_specs=[row,
                  pl.BlockSpec((1, 9, d), lambda b, i: (b, 0, 0)),
                  _const_spec((1, d)),
                  _const_spec(w_in.shape),
                  _const_spec(w_out.shape)],
        out_specs=row,
        out_shape=jax.ShapeDtypeStruct(x.shape, F32),
        compiler_params=pltpu.CompilerParams(
            dimension_semantics=("parallel", "parallel"),
            vmem_limit_bytes=VMEM_LIMIT_BYTES),
        name="ffn1",
    )(x, ada, norm, w_in, w_out)


PROJ_SUBTILES = 2


def _rope(v, cos_t, sin_t):
    return v * cos_t + pltpu.roll(v, LANES - MLA_ROPE, axis=1) * sin_t


def _proj_kernel(x_ref, ada_ref, norm_ref, pos_ref, invf_ref, w_in_ref, qn_ref, kvn_ref,
                 wq_ref, wuk_ref, wuvt_ref, vone_ref,
                 q_ref, k_ref, vt_ref, caq_ref, cak_ref, cavt_ref, ga_ref, gb_ref):
    shift = ada_ref[0, 3:4, :]
    scale = ada_ref[0, 4:5, :]
    lane = lax.broadcasted_iota(jnp.int32, (1, LANES), 1)
    is_rope = (lane >= MLA_NOPE) & (lane < MLA_NOPE + MLA_ROPE)
    qk_scale = (MLA_NOPE + MLA_ROPE) ** -0.5 * LOG2_E
    rows = x_ref.shape[1] // PROJ_SUBTILES
    ones_pad = (lax.broadcasted_iota(jnp.int32, (HEAD_SLAB - CA_HEAD_DIM, rows), 0) == 0).astype(F32)
    for sub in range(PROJ_SUBTILES):
        rs = slice(sub * rows, (sub + 1) * rows)
        h = (_rmsnorm(x_ref[0, rs, :], norm_ref[...]) * (1.0 + scale) + shift).astype(BF16)
        z = _dot(h, w_in_ref[...])

        ang = pos_ref[0, rs, :].astype(F32) * invf_ref[...]
        cos_t = jnp.where(lane < MLA_NOPE, 1.0, jnp.where(is_rope, jnp.cos(ang), 0.0))
        sin_t = jnp.where(is_rope, jnp.sin(ang), 0.0)

        cq = _rmsnorm(z[:, ZC_QLAT:ZC_QLAT + MLA_Q_RANK], qn_ref[...]).astype(BF16)
        qall = _dot(cq, wq_ref[...])
        ckv = _rmsnorm(z[:, ZC_KVLAT:ZC_KVLAT + MLA_KV_RANK], kvn_ref[...]).astype(BF16)
        kn = _dot(ckv, wuk_ref[...])
        kpe = _rope(z[:, ZC_KPE:ZC_KPE + LANES], cos_t, sin_t)
        for hd in range(MLA_HEADS):
            sl = slice(hd * HEAD_SLAB, (hd + 1) * HEAD_SLAB)
            q_ref[0, rs, sl] = (_rope(qall[:, sl], cos_t, sin_t) * qk_scale).astype(BF16)
            k_ref[0, rs, sl] = (kn[:, sl] + kpe).astype(BF16)
        vt_ref[0, :, rs] = (_dot_nt(wuvt_ref[...], ckv) + vone_ref[...]).astype(BF16)

        caq_ref[0, rs, :] = (z[:, ZC_CAQ:ZC_CAQ + CA_WIDTH]
                             * (CA_HEAD_DIM ** -0.5 * LOG2_E)).astype(BF16)
        cak_ref[0, rs, :] = z[:, ZC_CAK:ZC_CAK + CA_WIDTH].astype(BF16)
        cavt = jnp.transpose(z[:, ZC_CAV:ZC_CAV + CA_WIDTH])
        for hd in range(CA_HEADS):
            slab = jnp.concatenate([cavt[hd * CA_HEAD_DIM:(hd + 1) * CA_HEAD_DIM], ones_pad], axis=0)
            cavt_ref[0, hd * HEAD_SLAB:(hd + 1) * HEAD_SLAB, rs] = slab.astype(BF16)
        ga_ref[0, rs, :] = jax.nn.sigmoid(z[:, ZC_GA:ZC_GA + D_MODEL]).astype(BF16)
        gb_ref[0, rs, :] = jax.nn.sigmoid(z[:, ZC_GB:ZC_GB + D_MODEL]).astype(BF16)


def _proj(x, ada, norm, pos, invf, w_in_ext, qn, kvn, wq, wuk, wuvt, vone, tm):
    bsz, s, d = x.shape

    def row(width):
        return pl.BlockSpec((1, tm, width), lambda b, i: (b, i, 0))

    def out(width):
        return jax.ShapeDtypeStruct((bsz, s, width), BF16)

    slabs = MLA_HEADS * HEAD_SLAB
    widths = (slabs, slabs, None, CA_WIDTH, CA_WIDTH, None, D_MODEL, D_MODEL)
    vt_spec = pl.BlockSpec((1, slabs, tm), lambda b, i: (b, 0, i))
    vt_shape = jax.ShapeDtypeStruct((bsz, slabs, s), BF16)
    return pl.pallas_call(
        _proj_kernel,
        grid=(bsz, s // tm),
        in_specs=[row(d),
                  pl.BlockSpec((1, 9, d), lambda b, i: (b, 0, 0)),
                  _const_spec((1, d)),
                  row(1),
                  _const_spec((1, LANES)),
                  _const_spec(w_in_ext.shape),
                  _const_spec(qn.shape), _const_spec(kvn.shape),
                  _const_spec(wq.shape),
                  _const_spec(wuk.shape), _const_spec(wuvt.shape), _const_spec(vone.shape)],
        out_specs=[vt_spec if w is None else row(w) for w in widths],
        out_shape=[vt_shape if w is None else out(w) for w in widths],
        compiler_params=pltpu.CompilerParams(
            dimension_semantics=("parallel", "parallel"),
            vmem_limit_bytes=VMEM_LIMIT_BYTES),
        name="proj",
    )(x, ada, norm, pos, invf, w_in_ext, qn, kvn, wq, wuk, wuvt, vone)


MLA_HEADS_PER_STEP = 4
MLA_LOOKAHEAD = 2


def _mla_kernel(q_ref, k_ref, vt_ref, o_ref, m_ref, acc_ref, *, blk, hps):
    qi = pl.program_id(2)
    m_ref[...] = jnp.full(m_ref.shape, NEG_INF, F32)
    acc_ref[...] = jnp.zeros(acc_ref.shape, F32)

    def scores(hd, k0, mask):
        sl = slice(hd * HEAD_SLAB, (hd + 1) * HEAD_SLAB)
        s = _dot_nt(k_ref[0, pl.ds(k0, blk), sl], q_ref[0, :, sl])
        return s if mask is None else jnp.where(mask, s, NEG_INF)

    def step(j, mask):
        k0 = pl.multiple_of(j * blk, blk)
        pending = {hd: scores(hd, k0, mask) for hd in range(min(MLA_LOOKAHEAD, hps))}
        for hd in range(hps):
            s = pending.pop(hd)
            if hd + MLA_LOOKAHEAD < hps:
                pending[hd + MLA_LOOKAHEAD] = scores(hd + MLA_LOOKAHEAD, k0, mask)
            m_prev = m_ref[hd]
            m_new = jnp.maximum(m_prev, jnp.max(s, axis=0, keepdims=True))
            m_ref[hd] = m_new
            alpha = jnp.exp2(m_prev - m_new)
            p = jnp.exp2((s - m_new).astype(BF16))
            vt = vt_ref[0, hd * HEAD_SLAB:(hd + 1) * HEAD_SLAB, pl.ds(k0, blk)]
            acc_ref[hd] = acc_ref[hd] * alpha + _dot(vt, p)

    def body(j, carry):
        step(j, None)
        return carry

    lax.fori_loop(0, qi, body, 0)
    kc = lax.broadcasted_iota(jnp.int32, (blk, blk), 0) // CHUNK
    qc = lax.broadcasted_iota(jnp.int32, (blk, blk), 1) // CHUNK
    step(qi, kc <= qc)
    for pr in range(hps // 2):
        outs = []
        for hd in (2 * pr, 2 * pr + 1):
            a = acc_ref[hd]
            outs.append(a[:MLA_V] * (1.0 / a[MLA_V:MLA_V + 1]))
        o_ref[0, :, pr * LANES:(pr + 1) * LANES] = jnp.transpose(
            jnp.concatenate(outs, axis=0)).astype(BF16)


def _mla(q, k, vt, blk):
    bsz, s, _ = q.shape
    hps = MLA_HEADS_PER_STEP
    return pl.pallas_call(
        functools.partial(_mla_kernel, blk=blk, hps=hps),
        grid=(bsz, MLA_HEADS // hps, s // blk),
        in_specs=[pl.BlockSpec((1, blk, hps * HEAD_SLAB), lambda b, g, i: (b, i, g)),
                  pl.BlockSpec((1, s, hps * HEAD_SLAB), lambda b, g, i: (b, 0, g)),
                  pl.BlockSpec((1, hps * HEAD_SLAB, s), lambda b, g, i: (b, g, 0))],
        out_specs=pl.BlockSpec((1, blk, hps * MLA_V), lambda b, g, i: (b, i, g)),
        out_shape=jax.ShapeDtypeStruct((bsz, s, MLA_OUT_WIDTH), BF16),
        scratch_shapes=[pltpu.VMEM((hps, 1, blk), F32),
                        pltpu.VMEM((hps, HEAD_SLAB, blk), F32)],
        compiler_params=pltpu.CompilerParams(
            dimension_semantics=("parallel", "parallel", "arbitrary"),
            vmem_limit_bytes=VMEM_LIMIT_BYTES),
        name="mla",
    )(q, k, vt)


CA_TQ = 256
CA_TK = CA_TQ + CA_LEFT_CHUNKS * CHUNK
CA_KBLKS = CA_TK // CA_TQ
CA_TAB = 1024
CA_LOOKAHEAD = 3


def _ca_kernel(tab_ref, q_ref, *refs):
    k_refs, vt_refs = refs[:CA_KBLKS], refs[CA_KBLKS:2 * CA_KBLKS]
    o_ref, bias_ref = refs[2 * CA_KBLKS], refs[2 * CA_KBLKS + 1]
    b, i = pl.program_id(0), pl.program_id(1)

    @pl.when((b == 0) & (i == 0))
    def _build_bias():
        row = lax.broadcasted_iota(jnp.int32, (CA_TQ, CA_TAB), 0)
        kc = lax.broadcasted_iota(jnp.int32, (CA_TK, CA_TQ), 0) // CHUNK
        qc = lax.broadcasted_iota(jnp.int32, (CA_TK, CA_TQ), 1) // CHUNK
        band = (kc >= qc) & (kc <= qc + CA_LEFT_CHUNKS)
        for hd in range(CA_HEADS):
            t = jnp.broadcast_to(tab_ref[hd:hd + 1, :], (CA_TQ, CA_TAB))
            for bit in range(CA_TQ.bit_length() - 1):
                t = jnp.where(((row >> bit) & 1) == 1, pltpu.roll(t, 1 << bit, axis=1), t)
            bias_ref[hd] = jnp.where(band, jnp.transpose(t[:, :CA_TK]) * LOG2_E, NEG_INF)

    lo = lax.broadcasted_iota(jnp.int32, (1, LANES), 1) < CA_HEAD_DIM

    def attend(valid):
        def scores(hd):
            sl = slice((hd // 2) * LANES, (hd // 2 + 1) * LANES)
            q = q_ref[0, :, sl]
            zero = jnp.zeros_like(q)
            qh = jnp.where(lo, q, zero) if hd % 2 == 0 else jnp.where(lo, zero, q)
            kcat = jnp.concatenate([r[0, :, sl] for r in k_refs], axis=0)
            s = _dot_nt(kcat, qh) + bias_ref[hd]
            return s if valid is None else jnp.where(valid, s, NEG_INF)

        pending = {hd: scores(hd) for hd in range(CA_LOOKAHEAD)}
        outs = []
        for hd in range(CA_HEADS):
            s = pending.pop(hd)
            if hd + CA_LOOKAHEAD < CA_HEADS:
                pending[hd + CA_LOOKAHEAD] = scores(hd + CA_LOOKAHEAD)
            m = jnp.max(s, axis=0, keepdims=True)
            p = jnp.exp2((s - m).astype(BF16))
            rows = slice(hd * HEAD_SLAB, (hd + 1) * HEAD_SLAB)
            vt = jnp.concatenate([r[0, rows, :] for r in vt_refs], axis=1)
            a = _dot(vt, p)
            outs.append(a[:CA_HEAD_DIM] * (1.0 / a[CA_HEAD_DIM:CA_HEAD_DIM + 1]))
            if hd % 2 == 1:
                pr = hd // 2
                o_ref[0, :, pr * LANES:(pr + 1) * LANES] = jnp.transpose(
                    jnp.concatenate(outs, axis=0)).astype(BF16)
                outs = []

    @pl.when(i >= CA_KBLKS - 1)
    def _interior():
        attend(None)

    @pl.when(i < CA_KBLKS - 1)
    def _sequence_start():
        krow = lax.broadcasted_iota(jnp.int32, (CA_TK, 1), 0)
        attend(krow >= (CA_TK - CA_TQ) - i * CA_TQ)


def _ca(tab, q, k, vt):
    bsz, s, _ = q.shape

    def k_spec(back):
        return pl.BlockSpec((1, CA_TQ, CA_WIDTH), lambda b, i: (b, jnp.maximum(i - back, 0), 0))

    def vt_spec(back):
        return pl.BlockSpec((1, CA_HEADS * HEAD_SLAB, CA_TQ),
                            lambda b, i: (b, 0, jnp.maximum(i - back, 0)))

    blk = pl.BlockSpec((1, CA_TQ, CA_WIDTH), lambda b, i: (b, i, 0))
    backs = tuple(range(CA_KBLKS - 1, -1, -1))
    return pl.pallas_call(
        _ca_kernel,
        grid=(bsz, s // CA_TQ),
        in_specs=[_const_spec(tab.shape), blk] + [k_spec(x) for x in backs]
                 + [vt_spec(x) for x in backs],
        out_specs=blk,
        out_shape=jax.ShapeDtypeStruct((bsz, s, CA_WIDTH), BF16),
        scratch_shapes=[pltpu.VMEM((CA_HEADS, CA_TK, CA_TQ), F32)],
        compiler_params=pltpu.CompilerParams(
            dimension_semantics=("arbitrary", "arbitrary"),
            vmem_limit_bytes=VMEM_LIMIT_BYTES),
        name="ca",
    )(tab, q, *([k] * CA_KBLKS), *([vt] * CA_KBLKS))


def _tail_kernel(x_ref, ada_ref, aa_ref, ab_ref, ga_ref, gb_ref, wa_ref, wb_ref, wo_ref,
                 norm_ref, w_in_ref, w_out_ref, fnorm_ref, o_ref):
    ya = _dot(aa_ref[0], wa_ref[...])
    yb = _dot(ab_ref[0], wb_ref[...])
    merged = (ga_ref[0].astype(F32) * ya + gb_ref[0].astype(F32) * yb).astype(BF16)
    x = x_ref[0] + ada_ref[0, 5:6, :] * _dot(merged, wo_ref[...])
    x = _ffn_body(x, ada_ref, 6, norm_ref, w_in_ref, w_out_ref)
    o_ref[0] = _rmsnorm(x, fnorm_ref[...])


def _tail(x, ada, aa, ab, ga, gb, wa, wb, wo, norm, w_in, w_out, fnorm, tm):
    bsz, s, d = x.shape

    def row(width):
        return pl.BlockSpec((1, tm, width), lambda b, i: (b, i, 0))

    return pl.pallas_call(
        _tail_kernel,
        grid=(bsz, s // tm),
        in_specs=[row(d),
                  pl.BlockSpec((1, 9, d), lambda b, i: (b, 0, 0)),
                  row(MLA_OUT_WIDTH), row(CA_WIDTH), row(d), row(d),
                  _const_spec(wa.shape), _const_spec(wb.shape), _const_spec(wo.shape),
                  _const_spec((1, d)), _const_spec(w_in.shape), _const_spec(w_out.shape),
                  _const_spec((1, d))],
        out_specs=row(d),
        out_shape=jax.ShapeDtypeStruct(x.shape, F32),
        compiler_params=pltpu.CompilerParams(
            dimension_semantics=("parallel", "parallel"),
            vmem_limit_bytes=VMEM_LIMIT_BYTES),
        name="tail",
    )(x, ada, aa, ab, ga, gb, wa, wb, wo, norm, w_in, w_out, fnorm)


def _rotate_half_cols(w):
    half = w.shape[-1] // 2
    return jnp.concatenate([-w[..., half:], w[..., :half]], axis=-1)


def _prep_w_in(w_in):
    d = w_in.shape[0]
    o = 0
    pieces = {}
    for name, width in (("qlat", MLA_Q_RANK), ("kvlat", MLA_KV_RANK), ("kpe", MLA_ROPE),
                        ("caq", CA_WIDTH), ("cak", CA_WIDTH), ("cav", CA_WIDTH),
                        ("ga", D_MODEL), ("gb", D_MODEL)):
        pieces[name] = w_in[:, o:o + width]
        o += width
    zl = jnp.zeros((d, MLA_NOPE), w_in.dtype)
    kpe = jnp.concatenate([zl, pieces["kpe"], _rotate_half_cols(pieces["kpe"])], axis=1)
    ext = jnp.concatenate([pieces["qlat"], pieces["kvlat"], kpe, pieces["caq"],
                           pieces["cak"], pieces["cav"], pieces["ga"], pieces["gb"]], axis=1)
    assert ext.shape[1] == Z_COLS
    return ext.astype(BF16)


def _prep_w_uq(w_uq):
    r = w_uq.shape[0]
    w = w_uq.reshape(r, MLA_HEADS, MLA_NOPE + MLA_ROPE)
    nope, pe = w[..., :MLA_NOPE], w[..., MLA_NOPE:]
    slab = jnp.concatenate([nope, pe, _rotate_half_cols(pe)], axis=-1)
    return slab.reshape(r, MLA_HEADS * HEAD_SLAB).astype(BF16)


def _prep_w_ukv(w_ukv):
    r = w_ukv.shape[0]
    w = w_ukv.reshape(r, MLA_HEADS, MLA_NOPE + MLA_V)
    k_nope, v = w[..., :MLA_NOPE], w[..., MLA_NOPE:]
    zk = jnp.zeros((r, MLA_HEADS, HEAD_SLAB - MLA_NOPE), w.dtype)
    wuk = jnp.concatenate([k_nope, zk], axis=-1).reshape(r, MLA_HEADS * HEAD_SLAB)
    zv = jnp.zeros((r, MLA_HEADS, HEAD_SLAB - MLA_V), w.dtype)
    wuvt = jnp.concatenate([v, zv], axis=-1).reshape(r, MLA_HEADS * HEAD_SLAB).T
    vone = (jnp.arange(MLA_HEADS * HEAD_SLAB) % HEAD_SLAB == MLA_V).astype(F32).reshape(-1, 1)
    return wuk.astype(BF16), wuvt.astype(BF16), vone


def _prep_bias_table(rel_bias):
    j = jnp.arange(CA_TAB)
    t = (j + CA_TQ - 1) % CA_TAB
    rel = (CA_TK - 1) - t
    idx = jnp.clip(rel, -MAX_REL_DIST, MAX_REL_DIST) + MAX_REL_DIST
    return rel_bias[idx].T.astype(F32)


def kernel(x, c, positions, w_ada, b_ada, ffn1_norm, ffn1_w_in, ffn1_w_out, mix_norm, w_in,
           mla_q_norm, mla_w_uq, mla_kv_norm, mla_w_ukv, rel_bias, w_branch_a, w_branch_b,
           w_out, ffn2_norm, ffn2_w_in, ffn2_w_out, final_norm):
    bsz, s, d = x.shape
    tm = 512
    for l in range(w_ada.shape[0]):
        ada = _ada(c, w_ada[l], b_ada[l]).reshape(bsz, 9, d)
        x = _ffn1(x, ada, ffn1_norm[l].reshape(1, d), ffn1_w_in[l].astype(BF16),
                  ffn1_w_out[l].astype(BF16), tm)

        freq = jnp.arange(0, MLA_ROPE, 2, dtype=F32) / MLA_ROPE
        inv_freq = ROPE_THETA ** (-freq)
        invf = jnp.concatenate([jnp.zeros((MLA_NOPE,), F32), inv_freq, inv_freq,
                                jnp.zeros((LANES - MLA_NOPE - MLA_ROPE,), F32)]).reshape(1, LANES)
        wq = _prep_w_uq(mla_w_uq[l])
        wuk, wuvt, vone = _prep_w_ukv(mla_w_ukv[l])
        q, k, vt, caq, cak, cavt, ga, gb = _proj(
            x, ada, mix_norm[l].reshape(1, d), positions.reshape(bsz, s, 1), invf,
            _prep_w_in(w_in[l]), mla_q_norm[l].reshape(1, -1), mla_kv_norm[l].reshape(1, -1),
            wq, wuk, wuvt, vone, tm)
        attn_a = _mla(q, k, vt, 512)
        attn_b = _ca(_prep_bias_table(rel_bias[l]), caq, cak, cavt)
        last = l == w_ada.shape[0] - 1
        assert last, "the fused tail applies the final norm; DEPTH is 1"
        x = _tail(x, ada, attn_a, attn_b, ga, gb, w_branch_a[l].astype(BF16),
                  w_branch_b[l].astype(BF16), w_out[l].astype(BF16),
                  ffn2_norm[l].reshape(1, d), ffn2_w_in[l].astype(BF16),
                  ffn2_w_out[l].astype(BF16), final_norm.reshape(1, d), tm)
    return x
```

```python
import functools

import jax
import jax.numpy as jnp
from jax import lax
from jax.experimental import pallas as pl
from jax.experimental.pallas import tpu as pltpu

D_MODEL = 1024
CHUNK = 64
D_FF = 2816
FFN_RES_WEIGHT = 0.5
MLA_HEADS = 8
MLA_Q_RANK = 256
MLA_KV_RANK = 128
MLA_NOPE = 64
MLA_ROPE = 32
MLA_V = 64
ROPE_THETA = 10000.0
CA_HEADS = 8
CA_HEAD_DIM = 64
CA_LEFT_CHUNKS = 8
MAX_REL_DIST = 256
CA_WIDTH = CA_HEADS * CA_HEAD_DIM
MLA_OUT_WIDTH = MLA_HEADS * MLA_V
EPS = 1e-6
NEG_INF = -1e30
LOG2_E = 1.4426950408889634

LANES = 128
HEAD_SLAB = LANES
VMEM_LIMIT_BYTES = 56 * 1024 * 1024

ZC_QLAT = 0
ZC_KVLAT = ZC_QLAT + MLA_Q_RANK
ZC_KPE = ZC_KVLAT + MLA_KV_RANK
ZC_CAQ = ZC_KPE + LANES
ZC_CAK = ZC_CAQ + CA_WIDTH
ZC_CAV = ZC_CAK + CA_WIDTH
ZC_GA = ZC_CAV + CA_WIDTH
ZC_GB = ZC_GA + D_MODEL
Z_COLS = ZC_GB + D_MODEL

FFN_CHUNKS = ((0, 1024), (1024, 1024), (2048, 768))

BF16 = jnp.bfloat16
F32 = jnp.float32


def _const_spec(shape):
    nd = len(shape)
    return pl.BlockSpec(shape, lambda *_: (0,) * nd, pipeline_mode=pl.Buffered(1))


def _rmsnorm(x, g):
    return x * lax.rsqrt(jnp.mean(x * x, axis=-1, keepdims=True) + EPS) * g


def _dot(a, b):
    return jnp.dot(a, b, preferred_element_type=F32)


def _dot_nt(a, b):
    return lax.dot_general(a, b, (((1,), (1,)), ((), ())), preferred_element_type=F32)


def _ada_kernel(c_ref, w_ref, b_ref, o_ref):
    c = c_ref[...]
    c_act = c * jax.nn.sigmoid(c)
    o_ref[...] = jnp.dot(c_act, w_ref[...], preferred_element_type=F32,
                         precision=lax.Precision.HIGHEST) + b_ref[...]


def _ada(c, w_ada, b_ada):
    bsz, d = c.shape
    n = w_ada.shape[1]
    bn = 1152
    return pl.pallas_call(
        _ada_kernel,
        grid=(n // bn,),
        in_specs=[pl.BlockSpec((bsz, d), lambda j: (0, 0)),
                  pl.BlockSpec((d, bn), lambda j: (0, j)),
                  pl.BlockSpec((1, bn), lambda j: (0, j))],
        out_specs=pl.BlockSpec((bsz, bn), lambda j: (0, j)),
        out_shape=jax.ShapeDtypeStruct((bsz, n), F32),
        name="ada",
    )(c, w_ada, b_ada.reshape(1, n))


def _ffn_body(x, ada_ref, ada_base, norm_ref, w_in_ref, w_out_ref):
    shift = ada_ref[0, ada_base:ada_base + 1, :]
    scale = ada_ref[0, ada_base + 1:ada_base + 2, :]
    gate = ada_ref[0, ada_base + 2:ada_base + 3, :]
    h = (_rmsnorm(x, norm_ref[...]) * (1.0 + scale) + shift).astype(BF16)
    acc = None
    for c0, cw in FFN_CHUNKS:
        g = _dot(h, w_in_ref[:, c0:c0 + cw])
        u = _dot(h, w_in_ref[:, D_FF + c0:D_FF + c0 + cw])
        a = (g * jax.nn.sigmoid(g) * u).astype(BF16)
        part = _dot(a, w_out_ref[c0:c0 + cw, :])
        acc = part if acc is None else acc + part
    return x + (FFN_RES_WEIGHT * gate) * acc


def _ffn1_kernel(x_ref, ada_ref, norm_ref, w_in_ref, w_out_ref, o_ref):
    o_ref[0] = _ffn_body(x_ref[0], ada_ref, 0, norm_ref, w_in_ref, w_out_ref)


def _ffn1(x, ada, norm, w_in, w_out, tm):
    bsz, s, d = x.shape
    row = pl.BlockSpec((1, tm, d), lambda b, i: (b, i, 0))
    return pl.pallas_call(
        _ffn1_kernel,
        grid=(bsz, s // tm),
        in_specs=[row,
                  pl.BlockSpec((1, 9, d), lambda b, i: (b, 0, 0)),
                  _const_spec((1, d)),
                  _const_spec(w_in.shape),
                  _const_spec(w_out.shape)],
        out_specs=row,
        out_shape=jax.ShapeDtypeStruct(x.shape, F32),
        compiler_params=pltpu.CompilerParams(
            dimension_semantics=("parallel", "parallel"),
            vmem_limit_bytes=VMEM_LIMIT_BYTES),
        name="ffn1",
    )(x, ada, norm, w_in, w_out)


PROJ_SUBTILES = 2


def _rope(v, cos_t, sin_t):
    return v * cos_t + pltpu.roll(v, LANES - MLA_ROPE, axis=1) * sin_t


def _proj_kernel(x_ref, ada_ref, norm_ref, pos_ref, invf_ref, w_in_ref, qn_ref, kvn_ref,
                 wq_ref, wuk_ref, wuvt_ref, vone_ref,
                 q_ref, k_ref, vt_ref, caq_ref, cak_ref, cavt_ref, ga_ref, gb_ref):
    shift = ada_ref[0, 3:4, :]
    scale = ada_ref[0, 4:5, :]
    lane = lax.broadcasted_iota(jnp.int32, (1, LANES), 1)
    is_rope = (lane >= MLA_NOPE) & (lane < MLA_NOPE + MLA_ROPE)
    qk_scale = (MLA_NOPE + MLA_ROPE) ** -0.5 * LOG2_E
    rows = x_ref.shape[1] // PROJ_SUBTILES
    ones_pad = (lax.broadcasted_iota(jnp.int32, (HEAD_SLAB - CA_HEAD_DIM, rows), 0) == 0).astype(F32)
    for sub in range(PROJ_SUBTILES):
        rs = slice(sub * rows, (sub + 1) * rows)
        h = (_rmsnorm(x_ref[0, rs, :], norm_ref[...]) * (1.0 + scale) + shift).astype(BF16)
        z = _dot(h, w_in_ref[...])

        ang = pos_ref[0, rs, :].astype(F32) * invf_ref[...]
        cos_t = jnp.where(lane < MLA_NOPE, 1.0, jnp.where(is_rope, jnp.cos(ang), 0.0))
        sin_t = jnp.where(is_rope, jnp.sin(ang), 0.0)

        cq = _rmsnorm(z[:, ZC_QLAT:ZC_QLAT + MLA_Q_RANK], qn_ref[...]).astype(BF16)
        qall = _dot(cq, wq_ref[...])
        ckv = _rmsnorm(z[:, ZC_KVLAT:ZC_KVLAT + MLA_KV_RANK], kvn_ref[...]).astype(BF16)
        kn = _dot(ckv, wuk_ref[...])
        kpe = _rope(z[:, ZC_KPE:ZC_KPE + LANES], cos_t, sin_t)
        for hd in range(MLA_HEADS):
            sl = slice(hd * HEAD_SLAB, (hd + 1) * HEAD_SLAB)
            q_ref[0, rs, sl] = (_rope(qall[:, sl], cos_t, sin_t) * qk_scale).astype(BF16)
            k_ref[0, rs, sl] = (kn[:, sl] + kpe).astype(BF16)
        vt_ref[0, :, rs] = (_dot_nt(wuvt_ref[...], ckv) + vone_ref[...]).astype(BF16)

        caq_ref[0, rs, :] = (z[:, ZC_CAQ:ZC_CAQ + CA_WIDTH]
                             * (CA_HEAD_DIM ** -0.5 * LOG2_E)).astype(BF16)
        cak_ref[0, rs, :] = z[:, ZC_CAK:ZC_CAK + CA_WIDTH].astype(BF16)
        cavt = jnp.transpose(z[:, ZC_CAV:ZC_CAV + CA_WIDTH])
        for hd in range(CA_HEADS):
            slab = jnp.concatenate([cavt[hd * CA_HEAD_DIM:(hd + 1) * CA_HEAD_DIM], ones_pad], axis=0)
            cavt_ref[0, hd * HEAD_SLAB:(hd + 1) * HEAD_SLAB, rs] = slab.astype(BF16)
        ga_ref[0, rs, :] = jax.nn.sigmoid(z[:, ZC_GA:ZC_GA + D_MODEL]).astype(BF16)
        gb_ref[0, rs, :] = jax.nn.sigmoid(z[:, ZC_GB:ZC_GB + D_MODEL]).astype(BF16)


def _proj(x, ada, norm, pos, invf, w_in_ext, qn, kvn, wq, wuk, wuvt, vone, tm):
    bsz, s, d = x.shape

    def row(width):
        return pl.BlockSpec((1, tm, width), lambda b, i: (b, i, 0))

    def out(width):
        return jax.ShapeDtypeStruct((bsz, s, width), BF16)

    slabs = MLA_HEADS * HEAD_SLAB
    widths = (slabs, slabs, None, CA_WIDTH, CA_WIDTH, None, D_MODEL, D_MODEL)
    vt_spec = pl.BlockSpec((1, slabs, tm), lambda b, i: (b, 0, i))
    vt_shape = jax.ShapeDtypeStruct((bsz, slabs, s), BF16)
    return pl.pallas_call(
        _proj_kernel,
        grid=(bsz, s // tm),
        in_specs=[row(d),
                  pl.BlockSpec((1, 9, d), lambda b, i: (b, 0, 0)),
                  _const_spec((1, d)),
                  row(1),
                  _const_spec((1, LANES)),
                  _const_spec(w_in_ext.shape),
                  _const_spec(qn.shape), _const_spec(kvn.shape),
                  _const_spec(wq.shape),
                  _const_spec(wuk.shape), _const_spec(wuvt.shape), _const_spec(vone.shape)],
        out_specs=[vt_spec if w is None else row(w) for w in widths],
        out_shape=[vt_shape if w is None else out(w) for w in widths],
        compiler_params=pltpu.CompilerParams(
            dimension_semantics=("parallel", "parallel"),
            vmem_limit_bytes=VMEM_LIMIT_BYTES),
        name="proj",
    )(x, ada, norm, pos, invf, w_in_ext, qn, kvn, wq, wuk, wuvt, vone)


MLA_HEADS_PER_STEP = 8
MLA_LOOKAHEAD = 3


def _mla_kernel(q_ref, k_ref, vt_ref, o_ref, m_ref, acc_ref, *s_refs, blk, hps):
    qi = pl.program_id(2)
    m_ref[...] = jnp.full(m_ref.shape, NEG_INF, F32)
    acc_ref[...] = jnp.zeros(acc_ref.shape, F32)

    def scores(hd, k0, mask):
        sl = slice(hd * HEAD_SLAB, (hd + 1) * HEAD_SLAB)
        s = _dot_nt(k_ref[0, pl.ds(k0, blk), sl], q_ref[0, :, sl])
        s_refs[hd % len(s_refs)][...] = s if mask is None else jnp.where(mask, s, NEG_INF)

    def step(j, mask):
        k0 = pl.multiple_of(j * blk, blk)
        for hd in range(min(MLA_LOOKAHEAD, hps)):
            scores(hd, k0, mask)
        for hd in range(hps):
            if hd + MLA_LOOKAHEAD < hps:
                scores(hd + MLA_LOOKAHEAD, k0, mask)
            s = s_refs[hd % len(s_refs)][...]
            m_prev = m_ref[hd]
            m_new = jnp.maximum(m_prev, jnp.max(s, axis=0, keepdims=True))
            m_ref[hd] = m_new
            alpha = jnp.exp2(m_prev - m_new)
            p = jnp.exp2((s - m_new).astype(BF16))
            vt = vt_ref[0, hd * HEAD_SLAB:(hd + 1) * HEAD_SLAB, pl.ds(k0, blk)]
            acc_ref[hd] = acc_ref[hd] * alpha + _dot(vt, p)

    def body(j, carry):
        step(j, None)
        return carry

    lax.fori_loop(0, qi, body, 0)
    kc = lax.broadcasted_iota(jnp.int32, (blk, blk), 0) // CHUNK
    qc = lax.broadcasted_iota(jnp.int32, (blk, blk), 1) // CHUNK
    step(qi, kc <= qc)
    for pr in range(hps // 2):
        outs = []
        for hd in (2 * pr, 2 * pr + 1):
            a = acc_ref[hd]
            outs.append(a[:MLA_V] * (1.0 / a[MLA_V:MLA_V + 1]))
        o_ref[0, :, pr * LANES:(pr + 1) * LANES] = jnp.transpose(
            jnp.concatenate(outs, axis=0)).astype(BF16)


def _mla(q, k, vt, blk):
    bsz, s, _ = q.shape
    hps = MLA_HEADS_PER_STEP
    return pl.pallas_call(
        functools.partial(_mla_kernel, blk=blk, hps=hps),
        grid=(bsz, MLA_HEADS // hps, s // blk),
        in_specs=[pl.BlockSpec((1, blk, hps * HEAD_SLAB), lambda b, g, i: (b, i, g)),
                  pl.BlockSpec((1, s, hps * HEAD_SLAB), lambda b, g, i: (b, 0, g)),
                  pl.BlockSpec((1, hps * HEAD_SLAB, s), lambda b, g, i: (b, g, 0))],
        out_specs=pl.BlockSpec((1, blk, hps * MLA_V), lambda b, g, i: (b, i, g)),
        out_shape=jax.ShapeDtypeStruct((bsz, s, MLA_OUT_WIDTH), BF16),
        scratch_shapes=[pltpu.VMEM((hps, 1, blk), F32),
                        pltpu.VMEM((hps, HEAD_SLAB, blk), F32)]
                       + [pltpu.VMEM((blk, blk), F32)] * (MLA_LOOKAHEAD + 1),
        compiler_params=pltpu.CompilerParams(
            dimension_semantics=("parallel", "parallel", "arbitrary"),
            vmem_limit_bytes=VMEM_LIMIT_BYTES),
        name="mla",
    )(q, k, vt)


CA_TQ = 256
CA_TK = CA_TQ + CA_LEFT_CHUNKS * CHUNK
CA_KBLKS = CA_TK // CA_TQ
CA_TAB = 1024
CA_LOOKAHEAD = 3


def _ca_kernel(tab_ref, q_ref, *refs):
    k_refs, vt_refs = refs[:CA_KBLKS], refs[CA_KBLKS:2 * CA_KBLKS]
    o_ref, bias_ref = refs[2 * CA_KBLKS], refs[2 * CA_KBLKS + 1]
    b, i = pl.program_id(0), pl.program_id(1)

    @pl.when((b == 0) & (i == 0))
    def _build_bias():
        row = lax.broadcasted_iota(jnp.int32, (CA_TQ, CA_TAB), 0)
        kc = lax.broadcasted_iota(jnp.int32, (CA_TK, CA_TQ), 0) // CHUNK
        qc = lax.broadcasted_iota(jnp.int32, (CA_TK, CA_TQ), 1) // CHUNK
        band = (kc >= qc) & (kc <= qc + CA_LEFT_CHUNKS)
        for hd in range(CA_HEADS):
            t = jnp.broadcast_to(tab_ref[hd:hd + 1, :], (CA_TQ, CA_TAB))
            for bit in range(CA_TQ.bit_length() - 1):
                t = jnp.where(((row >> bit) & 1) == 1, pltpu.roll(t, 1 << bit, axis=1), t)
            bias_ref[hd] = jnp.where(band, jnp.transpose(t[:, :CA_TK]) * LOG2_E, NEG_INF)

    lo = lax.broadcasted_iota(jnp.int32, (1, LANES), 1) < CA_HEAD_DIM

    def attend(valid):
        def scores(hd):
            sl = slice((hd // 2) * LANES, (hd // 2 + 1) * LANES)
            q = q_ref[0, :, sl]
            zero = jnp.zeros_like(q)
            qh = jnp.where(lo, q, zero) if hd % 2 == 0 else jnp.where(lo, zero, q)
            kcat = jnp.concatenate([r[0, :, sl] for r in k_refs], axis=0)
            s = _dot_nt(kcat, qh) + bias_ref[hd]
            return s if valid is None else jnp.where(valid, s, NEG_INF)

        pending = {hd: scores(hd) for hd in range(CA_LOOKAHEAD)}
        outs = []
        for hd in range(CA_HEADS):
            s = pending.pop(hd)
            if hd + CA_LOOKAHEAD < CA_HEADS:
                pending[hd + CA_LOOKAHEAD] = scores(hd + CA_LOOKAHEAD)
            m = jnp.max(s, axis=0, keepdims=True)
            p = jnp.exp2((s - m).astype(BF16))
            rows = slice(hd * HEAD_SLAB, (hd + 1) * HEAD_SLAB)
            vt = jnp.concatenate([r[0, rows, :] for r in vt_refs], axis=1)
            a = _dot(vt, p)
            outs.append(a[:CA_HEAD_DIM] * (1.0 / a[CA_HEAD_DIM:CA_HEAD_DIM + 1]))
            if hd % 2 == 1:
                pr = hd // 2
                o_ref[0, :, pr * LANES:(pr + 1) * LANES] = jnp.transpose(
                    jnp.concatenate(outs, axis=0)).astype(BF16)
                outs = []

    @pl.when(i >= CA_KBLKS - 1)
    def _interior():
        attend(None)

    @pl.when(i < CA_KBLKS - 1)
    def _sequence_start():
        krow = lax.broadcasted_iota(jnp.int32, (CA_TK, 1), 0)
        attend(krow >= (CA_TK - CA_TQ) - i * CA_TQ)


def _ca(tab, q, k, vt):
    bsz, s, _ = q.shape

    def k_spec(back):
        return pl.BlockSpec((1, CA_TQ, CA_WIDTH), lambda b, i: (b, jnp.maximum(i - back, 0), 0))

    def vt_spec(back):
        return pl.BlockSpec((1, CA_HEADS * HEAD_SLAB, CA_TQ),
                            lambda b, i: (b, 0, jnp.maximum(i - back, 0)))

    blk = pl.BlockSpec((1, CA_TQ, CA_WIDTH), lambda b, i: (b, i, 0))
    backs = tuple(range(CA_KBLKS - 1, -1, -1))
    return pl.pallas_call(
        _ca_kernel,
        grid=(bsz, s // CA_TQ),
        in_specs=[_const_spec(tab.shape), blk] + [k_spec(x) for x in backs]
                 + [vt_spec(x) for x in backs],
        out_specs=blk,
        out_shape=jax.ShapeDtypeStruct((bsz, s, CA_WIDTH), BF16),
        scratch_shapes=[pltpu.VMEM((CA_HEADS, CA_TK, CA_TQ), F32)],
        compiler_params=pltpu.CompilerParams(
            dimension_semantics=("arbitrary", "arbitrary"),
            vmem_limit_bytes=VMEM_LIMIT_BYTES),
        name="ca",
    )(tab, q, *([k] * CA_KBLKS), *([vt] * CA_KBLKS))


def _tail_kernel(x_ref, ada_ref, aa_ref, ab_ref, ga_ref, gb_ref, wa_ref, wb_ref, wo_ref,
                 norm_ref, w_in_ref, w_out_ref, fnorm_ref, o_ref):
    ya = _dot(aa_ref[0], wa_ref[...])
    yb = _dot(ab_ref[0], wb_ref[...])
    merged = (ga_ref[0].astype(F32) * ya + gb_ref[0].astype(F32) * yb).astype(BF16)
    x = x_ref[0] + ada_ref[0, 5:6, :] * _dot(merged, wo_ref[...])
    x = _ffn_body(x, ada_ref, 6, norm_ref, w_in_ref, w_out_ref)
    o_ref[0] = _rmsnorm(x, fnorm_ref[...])


def _tail(x, ada, aa, ab, ga, gb, wa, wb, wo, norm, w_in, w_out, fnorm, tm):
    bsz, s, d = x.shape

    def row(width):
        return pl.BlockSpec((1, tm, width), lambda b, i: (b, i, 0))

    return pl.pallas_call(
        _tail_kernel,
        grid=(bsz, s // tm),
        in_specs=[row(d),
                  pl.BlockSpec((1, 9, d), lambda b, i: (b, 0, 0)),
                  row(MLA_OUT_WIDTH), row(CA_WIDTH), row(d), row(d),
                  _const_spec(wa.shape), _const_spec(wb.shape), _const_spec(wo.shape),
                  _const_spec((1, d)), _const_spec(w_in.shape), _const_spec(w_out.shape),
                  _const_spec((1, d))],
        out_specs=row(d),
        out_shape=jax.ShapeDtypeStruct(x.shape, F32),
        compiler_params=pltpu.CompilerParams(
            dimension_semantics=("parallel", "parallel"),
            vmem_limit_bytes=VMEM_LIMIT_BYTES),
        name="tail",
    )(x, ada, aa, ab, ga, gb, wa, wb, wo, norm, w_in, w_out, fnorm)


def _rotate_half_cols(w):
    half = w.shape[-1] // 2
    return jnp.concatenate([-w[..., half:], w[..., :half]], axis=-1)


def _prep_w_in(w_in):
    d = w_in.shape[0]
    o = 0
    pieces = {}
    for name, width in (("qlat", MLA_Q_RANK), ("kvlat", MLA_KV_RANK), ("kpe", MLA_ROPE),
                        ("caq", CA_WIDTH), ("cak", CA_WIDTH), ("cav", CA_WIDTH),
                        ("ga", D_MODEL), ("gb", D_MODEL)):
        pieces[name] = w_in[:, o:o + width]
        o += width
    zl = jnp.zeros((d, MLA_NOPE), w_in.dtype)
    kpe = jnp.concatenate([zl, pieces["kpe"], _rotate_half_cols(pieces["kpe"])], axis=1)
    ext = jnp.concatenate([pieces["qlat"], pieces["kvlat"], kpe, pieces["caq"],
                           pieces["cak"], pieces["cav"], pieces["ga"], pieces["gb"]], axis=1)
    assert ext.shape[1] == Z_COLS
    return ext.astype(BF16)


def _prep_w_uq(w_uq):
    r = w_uq.shape[0]
    w = w_uq.reshape(r, MLA_HEADS, MLA_NOPE + MLA_ROPE)
    nope, pe = w[..., :MLA_NOPE], w[..., MLA_NOPE:]
    slab = jnp.concatenate([nope, pe, _rotate_half_cols(pe)], axis=-1)
    return slab.reshape(r, MLA_HEADS * HEAD_SLAB).astype(BF16)


def _prep_w_ukv(w_ukv):
    r = w_ukv.shape[0]
    w = w_ukv.reshape(r, MLA_HEADS, MLA_NOPE + MLA_V)
    k_nope, v = w[..., :MLA_NOPE], w[..., MLA_NOPE:]
    zk = jnp.zeros((r, MLA_HEADS, HEAD_SLAB - MLA_NOPE), w.dtype)
    wuk = jnp.concatenate([k_nope, zk], axis=-1).reshape(r, MLA_HEADS * HEAD_SLAB)
    zv = jnp.zeros((r, MLA_HEADS, HEAD_SLAB - MLA_V), w.dtype)
    wuvt = jnp.concatenate([v, zv], axis=-1).reshape(r, MLA_HEADS * HEAD_SLAB).T
    vone = (jnp.arange(MLA_HEADS * HEAD_SLAB) % HEAD_SLAB == MLA_V).astype(F32).reshape(-1, 1)
    return wuk.astype(BF16), wuvt.astype(BF16), vone


def _prep_bias_table(rel_bias):
    j = jnp.arange(CA_TAB)
    t = (j + CA_TQ - 1) % CA_TAB
    rel = (CA_TK - 1) - t
    idx = jnp.clip(rel, -MAX_REL_DIST, MAX_REL_DIST) + MAX_REL_DIST
    return rel_bias[idx].T.astype(F32)


def kernel(x, c, positions, w_ada, b_ada, ffn1_norm, ffn1_w_in, ffn1_w_out, mix_norm, w_in,
           mla_q_norm, mla_w_uq, mla_kv_norm, mla_w_ukv, rel_bias, w_branch_a, w_branch_b,
           w_out, ffn2_norm, ffn2_w_in, ffn2_w_out, final_norm):
    bsz, s, d = x.shape
    tm = 512
    for l in range(w_ada.shape[0]):
        ada = _ada(c, w_ada[l], b_ada[l]).reshape(bsz, 9, d)
        x = _ffn1(x, ada, ffn1_norm[l].reshape(1, d), ffn1_w_in[l].astype(BF16),
                  ffn1_w_out[l].astype(BF16), tm)

        freq = jnp.arange(0, MLA_ROPE, 2, dtype=F32) / MLA_ROPE
        inv_freq = ROPE_THETA ** (-freq)
        invf = jnp.concatenate([jnp.zeros((MLA_NOPE,), F32), inv_freq, inv_freq,
                                jnp.zeros((LANES - MLA_NOPE - MLA_ROPE,), F32)]).reshape(1, LANES)
        wq = _prep_w_uq(mla_w_uq[l])
        wuk, wuvt, vone = _prep_w_ukv(mla_w_ukv[l])
        q, k, vt, caq, cak, cavt, ga, gb = _proj(
            x, ada, mix_norm[l].reshape(1, d), positions.reshape(bsz, s, 1), invf,
            _prep_w_in(w_in[l]), mla_q_norm[l].reshape(1, -1), mla_kv_norm[l].reshape(1, -1),
            wq, wuk, wuvt, vone, tm)
        attn_a = _mla(q, k, vt, 512)
        attn_b = _ca(_prep_bias_table(rel_bias[l]), caq, cak, cavt)
        last = l == w_ada.shape[0] - 1
        assert last, "the fused tail applies the final norm; DEPTH is 1"
        x = _tail(x, ada, attn_a, attn_b, ga, gb, w_branch_a[l].astype(BF16),
                  w_branch_b[l].astype(BF16), w_out[l].astype(BF16),
                  ffn2_norm[l].reshape(1, d), ffn2_w_in[l].astype(BF16),
                  ffn2_w_out[l].astype(BF16), final_norm.reshape(1, d), tm)
    return x
```

```python
import functools

import jax
import jax.numpy as jnp
from jax import lax
from jax.experimental import pallas as pl
from jax.experimental.pallas import tpu as pltpu

D_MODEL = 1024
CHUNK = 64
D_FF = 2816
FFN_RES_WEIGHT = 0.5
MLA_HEADS = 8
MLA_Q_RANK = 256
MLA_KV_RANK = 128
MLA_NOPE = 64
MLA_ROPE = 32
MLA_V = 64
ROPE_THETA = 10000.0
CA_HEADS = 8
CA_HEAD_DIM = 64
CA_LEFT_CHUNKS = 8
MAX_REL_DIST = 256
CA_WIDTH = CA_HEADS * CA_HEAD_DIM
MLA_OUT_WIDTH = MLA_HEADS * MLA_V
EPS = 1e-6
NEG_INF = -1e30
LOG2_E = 1.4426950408889634

LANES = 128
HEAD_SLAB = LANES
VMEM_LIMIT_BYTES = 56 * 1024 * 1024

ZC_QLAT = 0
ZC_KVLAT = ZC_QLAT + MLA_Q_RANK
ZC_KPE = ZC_KVLAT + MLA_KV_RANK
ZC_CAQ = ZC_KPE + LANES
ZC_CAK = ZC_CAQ + CA_WIDTH
ZC_CAV = ZC_CAK + CA_WIDTH
ZC_GA = ZC_CAV + CA_WIDTH
ZC_GB = ZC_GA + D_MODEL
Z_COLS = ZC_GB + D_MODEL

FFN_CHUNKS = ((0, 1024), (1024, 1024), (2048, 768))

BF16 = jnp.bfloat16
F32 = jnp.float32


def _const_spec(shape):
    nd = len(shape)
    return pl.BlockSpec(shape, lambda *_: (0,) * nd, pipeline_mode=pl.Buffered(1))


def _rmsnorm(x, g):
    return x * lax.rsqrt(jnp.mean(x * x, axis=-1, keepdims=True) + EPS) * g


def _dot(a, b):
    return jnp.dot(a, b, preferred_element_type=F32)


def _dot_nt(a, b):
    return lax.dot_general(a, b, (((1,), (1,)), ((), ())), preferred_element_type=F32)


def _ada_kernel(c_ref, w_ref, b_ref, o_ref):
    c = c_ref[...]
    c_act = c * jax.nn.sigmoid(c)
    o_ref[...] = jnp.dot(c_act, w_ref[...], preferred_element_type=F32,
                         precision=lax.Precision.HIGHEST) + b_ref[...]


def _ada(c, w_ada, b_ada):
    bsz, d = c.shape
    n = w_ada.shape[1]
    bn = 1152
    return pl.pallas_call(
        _ada_kernel,
        grid=(n // bn,),
        in_specs=[pl.BlockSpec((bsz, d), lambda j: (0, 0)),
                  pl.BlockSpec((d, bn), lambda j: (0, j)),
                  pl.BlockSpec((1, bn), lambda j: (0, j))],
        out_specs=pl.BlockSpec((bsz, bn), lambda j: (0, j)),
        out_shape=jax.ShapeDtypeStruct((bsz, n), F32),
        name="ada",
    )(c, w_ada, b_ada.reshape(1, n))


def _ffn_body(x, ada_ref, ada_base, norm_ref, w_in_ref, w_out_ref):
    shift = ada_ref[0, ada_base:ada_base + 1, :]
    scale = ada_ref[0, ada_base + 1:ada_base + 2, :]
    gate = ada_ref[0, ada_base + 2:ada_base + 3, :]
    h = (_rmsnorm(x, norm_ref[...]) * (1.0 + scale) + shift).astype(BF16)
    acc = None
    for c0, cw in FFN_CHUNKS:
        g = _dot(h, w_in_ref[:, c0:c0 + cw])
        u = _dot(h, w_in_ref[:, D_FF + c0:D_FF + c0 + cw])
        a = (g * jax.nn.sigmoid(g) * u).astype(BF16)
        part = _dot(a, w_out_ref[c0:c0 + cw, :])
        acc = part if acc is None else acc + part
    return x + (FFN_RES_WEIGHT * gate) * acc


def _ffn1_kernel(x_ref, ada_ref, norm_ref, w_in_ref, w_out_ref, o_ref):
    o_ref[0] = _ffn_body(x_ref[0], ada_ref, 0, norm_ref, w_in_ref, w_out_ref)


def _ffn1(x, ada, norm, w_in, w_out, tm):
    bsz, s, d = x.shape
    row = pl.BlockSpec((1, tm, d), lambda b, i: (b, i, 0))
    return pl.pallas_call(
        _ffn1_kernel,
        grid=(bsz, s // tm),
        in_specs=[row,
                  pl.BlockSpec((1, 9, d), lambda b, i: (b, 0, 0)),
                  _const_spec((1, d)),
                  _const_spec(w_in.shape),
                  _const_spec(w_out.shape)],
        out_specs=row,
        out_shape=jax.ShapeDtypeStruct(x.shape, F32),
        compiler_params=pltpu.CompilerParams(
            dimension_semantics=("parallel", "parallel"),
            vmem_limit_bytes=VMEM_LIMIT_BYTES),
        name="ffn1",
    )(x, ada, norm, w_in, w_out)


PROJ_SUBTILES = 2


def _rope(v, cos_t, sin_t):
    return v * cos_t + pltpu.roll(v, LANES - MLA_ROPE, axis=1) * sin_t


def _proj_kernel(x_ref, ada_ref, norm_ref, pos_ref, invf_ref, w_in_ref, qn_ref, kvn_ref,
                 wq_ref, wuk_ref, wuvt_ref, vone_ref,
                 q_ref, k_ref, vt_ref, caq_ref, cak_ref, cavt_ref, ga_ref, gb_ref):
    shift = ada_ref[0, 3:4, :]
    scale = ada_ref[0, 4:5, :]
    lane = lax.broadcasted_iota(jnp.int32, (1, LANES), 1)
    is_rope = (lane >= MLA_NOPE) & (lane < MLA_NOPE + MLA_ROPE)
    qk_scale = (MLA_NOPE + MLA_ROPE) ** -0.5 * LOG2_E
    rows = x_ref.shape[1] // PROJ_SUBTILES
    ones_pad = (lax.broadcasted_iota(jnp.int32, (HEAD_SLAB - CA_HEAD_DIM, rows), 0) == 0).astype(F32)
    for sub in range(PROJ_SUBTILES):
        rs = slice(sub * rows, (sub + 1) * rows)
        h = (_rmsnorm(x_ref[0, rs, :], norm_ref[...]) * (1.0 + scale) + shift).astype(BF16)
        z = _dot_nt(h, w_in_ref[...])

        ang = pos_ref[0, rs, :].astype(F32) * invf_ref[...]
        cos_t = jnp.where(lane < MLA_NOPE, 1.0, jnp.where(is_rope, jnp.cos(ang), 0.0))
        sin_t = jnp.where(is_rope, jnp.sin(ang), 0.0)

        cq = _rmsnorm(z[:, ZC_QLAT:ZC_QLAT + MLA_Q_RANK], qn_ref[...]).astype(BF16)
        qall = _dot(cq, wq_ref[...])
        ckv = _rmsnorm(z[:, ZC_KVLAT:ZC_KVLAT + MLA_KV_RANK], kvn_ref[...]).astype(BF16)
        kn = _dot(ckv, wuk_ref[...])
        kpe = _rope(z[:, ZC_KPE:ZC_KPE + LANES], cos_t, sin_t)
        for hd in range(MLA_HEADS):
            sl = slice(hd * HEAD_SLAB, (hd + 1) * HEAD_SLAB)
            q_ref[0, rs, sl] = (_rope(qall[:, sl], cos_t, sin_t) * qk_scale).astype(BF16)
            k_ref[0, rs, sl] = (kn[:, sl] + kpe).astype(BF16)
        vt_ref[0, :, rs] = (_dot_nt(wuvt_ref[...], ckv) + vone_ref[...]).astype(BF16)

        caq_ref[0, rs, :] = (z[:, ZC_CAQ:ZC_CAQ + CA_WIDTH]
                             * (CA_HEAD_DIM ** -0.5 * LOG2_E)).astype(BF16)
        cak_ref[0, rs, :] = z[:, ZC_CAK:ZC_CAK + CA_WIDTH].astype(BF16)
        cavt = jnp.transpose(z[:, ZC_CAV:ZC_CAV + CA_WIDTH])
        for hd in range(CA_HEADS):
            slab = jnp.concatenate([cavt[hd * CA_HEAD_DIM:(hd + 1) * CA_HEAD_DIM], ones_pad], axis=0)
            cavt_ref[0, hd * HEAD_SLAB:(hd + 1) * HEAD_SLAB, rs] = slab.astype(BF16)
        ga_ref[0, rs, :] = jax.nn.sigmoid(z[:, ZC_GA:ZC_GA + D_MODEL]).astype(BF16)
        gb_ref[0, rs, :] = jax.nn.sigmoid(z[:, ZC_GB:ZC_GB + D_MODEL]).astype(BF16)


def _proj(x, ada, norm, pos, invf, w_in_ext, qn, kvn, wq, wuk, wuvt, vone, tm):
    bsz, s, d = x.shape

    def row(width):
        return pl.BlockSpec((1, tm, width), lambda b, i: (b, i, 0))

    def out(width):
        return jax.ShapeDtypeStruct((bsz, s, width), BF16)

    slabs = MLA_HEADS * HEAD_SLAB
    widths = (slabs, slabs, None, CA_WIDTH, CA_WIDTH, None, D_MODEL, D_MODEL)
    vt_spec = pl.BlockSpec((1, slabs, tm), lambda b, i: (b, 0, i))
    vt_shape = jax.ShapeDtypeStruct((bsz, slabs, s), BF16)
    return pl.pallas_call(
        _proj_kernel,
        grid=(bsz, s // tm),
        in_specs=[row(d),
                  pl.BlockSpec((1, 9, d), lambda b, i: (b, 0, 0)),
                  _const_spec((1, d)),
                  row(1),
                  _const_spec((1, LANES)),
                  _const_spec(w_in_ext.shape),
                  _const_spec(qn.shape), _const_spec(kvn.shape),
                  _const_spec(wq.shape),
                  _const_spec(wuk.shape), _const_spec(wuvt.shape), _const_spec(vone.shape)],
        out_specs=[vt_spec if w is None else row(w) for w in widths],
        out_shape=[vt_shape if w is None else out(w) for w in widths],
        compiler_params=pltpu.CompilerParams(
            dimension_semantics=("parallel", "parallel"),
            vmem_limit_bytes=VMEM_LIMIT_BYTES),
        name="proj",
    )(x, ada, norm, pos, invf, w_in_ext, qn, kvn, wq, wuk, wuvt, vone)


MLA_HEADS_PER_STEP = 8
MLA_LOOKAHEAD = 3


def _mla_kernel(q_ref, k_ref, vt_ref, o_ref, m_ref, acc_ref, *s_refs, blk, hps):
    qi = pl.program_id(2)
    m_ref[...] = jnp.full(m_ref.shape, NEG_INF, F32)
    acc_ref[...] = jnp.zeros(acc_ref.shape, F32)

    def scores(hd, k0, mask):
        sl = slice(hd * HEAD_SLAB, (hd + 1) * HEAD_SLAB)
        s = _dot_nt(k_ref[0, pl.ds(k0, blk), sl], q_ref[0, :, sl])
        s_refs[hd % len(s_refs)][...] = s if mask is None else jnp.where(mask, s, NEG_INF)

    def step(j, mask):
        k0 = pl.multiple_of(j * blk, blk)
        for hd in range(min(MLA_LOOKAHEAD, hps)):
            scores(hd, k0, mask)
        for hd in range(hps):
            if hd + MLA_LOOKAHEAD < hps:
                scores(hd + MLA_LOOKAHEAD, k0, mask)
            s = s_refs[hd % len(s_refs)][...]
            m_prev = m_ref[hd]
            m_new = jnp.maximum(m_prev, jnp.max(s, axis=0, keepdims=True))
            m_ref[hd] = m_new
            alpha = jnp.exp2(m_prev - m_new)
            p = jnp.exp2((s - m_new).astype(BF16))
            vt = vt_ref[0, hd * HEAD_SLAB:(hd + 1) * HEAD_SLAB, pl.ds(k0, blk)]
            acc_ref[hd] = acc_ref[hd] * alpha + _dot(vt, p)

    def body(j, carry):
        step(j, None)
        return carry

    lax.fori_loop(0, qi, body, 0)
    kc = lax.broadcasted_iota(jnp.int32, (blk, blk), 0) // CHUNK
    qc = lax.broadcasted_iota(jnp.int32, (blk, blk), 1) // CHUNK
    step(qi, kc <= qc)
    for pr in range(hps // 2):
        outs = []
        for hd in (2 * pr, 2 * pr + 1):
            a = acc_ref[hd]
            outs.append(a[:MLA_V] * (1.0 / a[MLA_V:MLA_V + 1]))
        o_ref[0, :, pr * LANES:(pr + 1) * LANES] = jnp.transpose(
            jnp.concatenate(outs, axis=0)).astype(BF16)


def _mla(q, k, vt, blk):
    bsz, s, _ = q.shape
    hps = MLA_HEADS_PER_STEP
    return pl.pallas_call(
        functools.partial(_mla_kernel, blk=blk, hps=hps),
        grid=(bsz, MLA_HEADS // hps, s // blk),
        in_specs=[pl.BlockSpec((1, blk, hps * HEAD_SLAB), lambda b, g, i: (b, i, g)),
                  pl.BlockSpec((1, s, hps * HEAD_SLAB), lambda b, g, i: (b, 0, g)),
                  pl.BlockSpec((1, hps * HEAD_SLAB, s), lambda b, g, i: (b, g, 0))],
        out_specs=pl.BlockSpec((1, blk, hps * MLA_V), lambda b, g, i: (b, i, g)),
        out_shape=jax.ShapeDtypeStruct((bsz, s, MLA_OUT_WIDTH), BF16),
        scratch_shapes=[pltpu.VMEM((hps, 1, blk), F32),
                        pltpu.VMEM((hps, HEAD_SLAB, blk), F32)]
                       + [pltpu.VMEM((blk, blk), F32)] * (MLA_LOOKAHEAD + 1),
        compiler_params=pltpu.CompilerParams(
            dimension_semantics=("parallel", "parallel", "arbitrary"),
            vmem_limit_bytes=VMEM_LIMIT_BYTES),
        name="mla",
    )(q, k, vt)


CA_TQ = 256
CA_TK = CA_TQ + CA_LEFT_CHUNKS * CHUNK
CA_KBLKS = CA_TK // CA_TQ
CA_TAB = 1024
CA_LOOKAHEAD = 3


def _ca_kernel(tab_ref, q_ref, *refs):
    k_refs, vt_refs = refs[:CA_KBLKS], refs[CA_KBLKS:2 * CA_KBLKS]
    o_ref, bias_ref = refs[2 * CA_KBLKS], refs[2 * CA_KBLKS + 1]
    b, i = pl.program_id(0), pl.program_id(1)

    @pl.when((b == 0) & (i == 0))
    def _build_bias():
        row = lax.broadcasted_iota(jnp.int32, (CA_TQ, CA_TAB), 0)
        kc = lax.broadcasted_iota(jnp.int32, (CA_TK, CA_TQ), 0) // CHUNK
        qc = lax.broadcasted_iota(jnp.int32, (CA_TK, CA_TQ), 1) // CHUNK
        band = (kc >= qc) & (kc <= qc + CA_LEFT_CHUNKS)
        for hd in range(CA_HEADS):
            t = jnp.broadcast_to(tab_ref[hd:hd + 1, :], (CA_TQ, CA_TAB))
            for bit in range(CA_TQ.bit_length() - 1):
                t = jnp.where(((row >> bit) & 1) == 1, pltpu.roll(t, 1 << bit, axis=1), t)
            bias_ref[hd] = jnp.where(band, jnp.transpose(t[:, :CA_TK]) * LOG2_E, NEG_INF)

    lo = lax.broadcasted_iota(jnp.int32, (1, LANES), 1) < CA_HEAD_DIM

    def attend(valid):
        def scores(hd):
            sl = slice((hd // 2) * LANES, (hd // 2 + 1) * LANES)
            q = q_ref[0, :, sl]
            zero = jnp.zeros_like(q)
            qh = jnp.where(lo, q, zero) if hd % 2 == 0 else jnp.where(lo, zero, q)
            kcat = jnp.concatenate([r[0, :, sl] for r in k_refs], axis=0)
            s = _dot_nt(kcat, qh) + bias_ref[hd]
            return s if valid is None else jnp.where(valid, s, NEG_INF)

        pending = {hd: scores(hd) for hd in range(CA_LOOKAHEAD)}
        outs = []
        for hd in range(CA_HEADS):
            s = pending.pop(hd)
            if hd + CA_LOOKAHEAD < CA_HEADS:
                pending[hd + CA_LOOKAHEAD] = scores(hd + CA_LOOKAHEAD)
            m = jnp.max(s, axis=0, keepdims=True)
            p = jnp.exp2((s - m).astype(BF16))
            rows = slice(hd * HEAD_SLAB, (hd + 1) * HEAD_SLAB)
            vt = jnp.concatenate([r[0, rows, :] for r in vt_refs], axis=1)
            a = _dot(vt, p)
            outs.append(a[:CA_HEAD_DIM] * (1.0 / a[CA_HEAD_DIM:CA_HEAD_DIM + 1]))
            if hd % 2 == 1:
                pr = hd // 2
                o_ref[0, :, pr * LANES:(pr + 1) * LANES] = jnp.transpose(
                    jnp.concatenate(outs, axis=0)).astype(BF16)
                outs = []

    @pl.when(i >= CA_KBLKS - 1)
    def _interior():
        attend(None)

    @pl.when(i < CA_KBLKS - 1)
    def _sequence_start():
        krow = lax.broadcasted_iota(jnp.int32, (CA_TK, 1), 0)
        attend(krow >= (CA_TK - CA_TQ) - i * CA_TQ)


def _ca(tab, q, k, vt):
    bsz, s, _ = q.shape

    def k_spec(back):
        return pl.BlockSpec((1, CA_TQ, CA_WIDTH), lambda b, i: (b, jnp.maximum(i - back, 0), 0))

    def vt_spec(back):
        return pl.BlockSpec((1, CA_HEADS * HEAD_SLAB, CA_TQ),
                            lambda b, i: (b, 0, jnp.maximum(i - back, 0)))

    blk = pl.BlockSpec((1, CA_TQ, CA_WIDTH), lambda b, i: (b, i, 0))
    backs = tuple(range(CA_KBLKS - 1, -1, -1))
    return pl.pallas_call(
        _ca_kernel,
        grid=(bsz, s // CA_TQ),
        in_specs=[_const_spec(tab.shape), blk] + [k_spec(x) for x in backs]
                 + [vt_spec(x) for x in backs],
        out_specs=blk,
        out_shape=jax.ShapeDtypeStruct((bsz, s, CA_WIDTH), BF16),
        scratch_shapes=[pltpu.VMEM((CA_HEADS, CA_TK, CA_TQ), F32)],
        compiler_params=pltpu.CompilerParams(
            dimension_semantics=("arbitrary", "arbitrary"),
            vmem_limit_bytes=VMEM_LIMIT_BYTES),
        name="ca",
    )(tab, q, *([k] * CA_KBLKS), *([vt] * CA_KBLKS))


def _tail_kernel(x_ref, ada_ref, aa_ref, ab_ref, ga_ref, gb_ref, wa_ref, wb_ref, wo_ref,
                 norm_ref, w_in_ref, w_out_ref, fnorm_ref, o_ref):
    ya = _dot(aa_ref[0], wa_ref[...])
    yb = _dot(ab_ref[0], wb_ref[...])
    merged = (ga_ref[0].astype(F32) * ya + gb_ref[0].astype(F32) * yb).astype(BF16)
    x = x_ref[0] + ada_ref[0, 5:6, :] * _dot(merged, wo_ref[...])
    x = _ffn_body(x, ada_ref, 6, norm_ref, w_in_ref, w_out_ref)
    o_ref[0] = _rmsnorm(x, fnorm_ref[...])


def _tail(x, ada, aa, ab, ga, gb, wa, wb, wo, norm, w_in, w_out, fnorm, tm):
    bsz, s, d = x.shape

    def row(width):
        return pl.BlockSpec((1, tm, width), lambda b, i: (b, i, 0))

    return pl.pallas_call(
        _tail_kernel,
        grid=(bsz, s // tm),
        in_specs=[row(d),
                  pl.BlockSpec((1, 9, d), lambda b, i: (b, 0, 0)),
                  row(MLA_OUT_WIDTH), row(CA_WIDTH), row(d), row(d),
                  _const_spec(wa.shape), _const_spec(wb.shape), _const_spec(wo.shape),
                  _const_spec((1, d)), _const_spec(w_in.shape), _const_spec(w_out.shape),
                  _const_spec((1, d))],
        out_specs=row(d),
        out_shape=jax.ShapeDtypeStruct(x.shape, F32),
        compiler_params=pltpu.CompilerParams(
            dimension_semantics=("parallel", "parallel"),
            vmem_limit_bytes=VMEM_LIMIT_BYTES),
        name="tail",
    )(x, ada, aa, ab, ga, gb, wa, wb, wo, norm, w_in, w_out, fnorm)


def _rotate_half_cols(w):
    half = w.shape[-1] // 2
    return jnp.concatenate([-w[..., half:], w[..., :half]], axis=-1)


def _prep_w_in(w_in):
    d = w_in.shape[0]
    o = 0
    pieces = {}
    for name, width in (("qlat", MLA_Q_RANK), ("kvlat", MLA_KV_RANK), ("kpe", MLA_ROPE),
                        ("caq", CA_WIDTH), ("cak", CA_WIDTH), ("cav", CA_WIDTH),
                        ("ga", D_MODEL), ("gb", D_MODEL)):
        pieces[name] = w_in[:, o:o + width]
        o += width
    zl = jnp.zeros((d, MLA_NOPE), w_in.dtype)
    kpe = jnp.concatenate([zl, pieces["kpe"], _rotate_half_cols(pieces["kpe"])], axis=1)
    ext = jnp.concatenate([pieces["qlat"].T, pieces["kvlat"].T, kpe.T, pieces["caq"].T,
                           pieces["cak"].T, pieces["cav"].T, pieces["ga"].T, pieces["gb"].T], axis=0)
    assert ext.shape[0] == Z_COLS
    return ext.astype(BF16)


def _prep_w_uq(w_uq):
    r = w_uq.shape[0]
    w = w_uq.reshape(r, MLA_HEADS, MLA_NOPE + MLA_ROPE)
    nope, pe = w[..., :MLA_NOPE], w[..., MLA_NOPE:]
    slab = jnp.concatenate([nope, pe, _rotate_half_cols(pe)], axis=-1)
    return slab.reshape(r, MLA_HEADS * HEAD_SLAB).astype(BF16)


def _prep_w_ukv(w_ukv):
    r = w_ukv.shape[0]
    w = w_ukv.reshape(r, MLA_HEADS, MLA_NOPE + MLA_V)
    k_nope, v = w[..., :MLA_NOPE], w[..., MLA_NOPE:]
    zk = jnp.zeros((r, MLA_HEADS, HEAD_SLAB - MLA_NOPE), w.dtype)
    wuk = jnp.concatenate([k_nope, zk], axis=-1).reshape(r, MLA_HEADS * HEAD_SLAB)
    zv = jnp.zeros((r, MLA_HEADS, HEAD_SLAB - MLA_V), w.dtype)
    wuvt = jnp.concatenate([v, zv], axis=-1).reshape(r, MLA_HEADS * HEAD_SLAB).T
    vone = (jnp.arange(MLA_HEADS * HEAD_SLAB) % HEAD_SLAB == MLA_V).astype(F32).reshape(-1, 1)
    return wuk.astype(BF16), wuvt.astype(BF16), vone


def _prep_bias_table(rel_bias):
    j = jnp.arange(CA_TAB)
    t = (j + CA_TQ - 1) % CA_TAB
    rel = (CA_TK - 1) - t
    idx = jnp.clip(rel, -MAX_REL_DIST, MAX_REL_DIST) + MAX_REL_DIST
    return rel_bias[idx].T.astype(F32)


def kernel(x, c, positions, w_ada, b_ada, ffn1_norm, ffn1_w_in, ffn1_w_out, mix_norm, w_in,
           mla_q_norm, mla_w_uq, mla_kv_norm, mla_w_ukv, rel_bias, w_branch_a, w_branch_b,
           w_out, ffn2_norm, ffn2_w_in, ffn2_w_out, final_norm):
    bsz, s, d = x.shape
    tm = 512
    for l in range(w_ada.shape[0]):
        ada = _ada(c, w_ada[l], b_ada[l]).reshape(bsz, 9, d)
        x = _ffn1(x, ada, ffn1_norm[l].reshape(1, d), ffn1_w_in[l].astype(BF16),
                  ffn1_w_out[l].astype(BF16), tm)

        freq = jnp.arange(0, MLA_ROPE, 2, dtype=F32) / MLA_ROPE
        inv_freq = ROPE_THETA ** (-freq)
        invf = jnp.concatenate([jnp.zeros((MLA_NOPE,), F32), inv_freq, inv_freq,
                                jnp.zeros((LANES - MLA_NOPE - MLA_ROPE,), F32)]).reshape(1, LANES)
        wq = _prep_w_uq(mla_w_uq[l])
        wuk, wuvt, vone = _prep_w_ukv(mla_w_ukv[l])
        q, k, vt, caq, cak, cavt, ga, gb = _proj(
            x, ada, mix_norm[l].reshape(1, d), positions.reshape(bsz, s, 1), invf,
            _prep_w_in(w_in[l]), mla_q_norm[l].reshape(1, -1), mla_kv_norm[l].reshape(1, -1),
            wq, wuk, wuvt, vone, tm)
        attn_a = _mla(q, k, vt, 512)
        attn_b = _ca(_prep_bias_table(rel_bias[l]), caq, cak, cavt)
        last = l == w_ada.shape[0] - 1
        assert last, "the fused tail applies the final norm; DEPTH is 1"
        x = _tail(x, ada, attn_a, attn_b, ga, gb, w_branch_a[l].astype(BF16),
                  w_branch_b[l].astype(BF16), w_out[l].astype(BF16),
                  ffn2_norm[l].reshape(1, d), ffn2_w_in[l].astype(BF16),
                  ffn2_w_out[l].astype(BF16), final_norm.reshape(1, d), tm)
    return x
```

```python
import functools

import jax
import jax.numpy as jnp
from jax import lax
from jax.experimental import pallas as pl
from jax.experimental.pallas import tpu as pltpu

D_MODEL = 1024
CHUNK = 64
D_FF = 2816
FFN_RES_WEIGHT = 0.5
MLA_HEADS = 8
MLA_Q_RANK = 256
MLA_KV_RANK = 128
MLA_NOPE = 64
MLA_ROPE = 32
MLA_V = 64
ROPE_THETA = 10000.0
CA_HEADS = 8
CA_HEAD_DIM = 64
CA_LEFT_CHUNKS = 8
MAX_REL_DIST = 256
CA_WIDTH = CA_HEADS * CA_HEAD_DIM
MLA_OUT_WIDTH = MLA_HEADS * MLA_V
EPS = 1e-6
NEG_INF = -1e30
LOG2_E = 1.4426950408889634

LANES = 128
HEAD_SLAB = LANES
VMEM_LIMIT_BYTES = 56 * 1024 * 1024

ZC_QLAT = 0
ZC_KVLAT = ZC_QLAT + MLA_Q_RANK
ZC_KPE = ZC_KVLAT + MLA_KV_RANK
ZC_CAQ = ZC_KPE + LANES
ZC_CAK = ZC_CAQ + CA_WIDTH
ZC_CAV = ZC_CAK + CA_WIDTH
ZC_GA = ZC_CAV + CA_WIDTH
ZC_GB = ZC_GA + D_MODEL
Z_COLS = ZC_GB + D_MODEL

FFN_CHUNKS = ((0, 1024), (1024, 1024), (2048, 768))

BF16 = jnp.bfloat16
F32 = jnp.float32


def _const_spec(shape):
    nd = len(shape)
    return pl.BlockSpec(shape, lambda *_: (0,) * nd, pipeline_mode=pl.Buffered(1))


def _rmsnorm(x, g):
    return x * lax.rsqrt(jnp.mean(x * x, axis=-1, keepdims=True) + EPS) * g


def _dot(a, b):
    return jnp.dot(a, b, preferred_element_type=F32)


def _dot_nt(a, b):
    return lax.dot_general(a, b, (((1,), (1,)), ((), ())), preferred_element_type=F32)


def _ada_kernel(c_ref, w_ref, b_ref, o_ref):
    c = c_ref[...]
    c_act = c * jax.nn.sigmoid(c)
    o_ref[...] = jnp.dot(c_act, w_ref[...], preferred_element_type=F32,
                         precision=lax.Precision.HIGHEST) + b_ref[...]


def _ada(c, w_ada, b_ada):
    bsz, d = c.shape
    n = w_ada.shape[1]
    bn = 1152
    return pl.pallas_call(
        _ada_kernel,
        grid=(n // bn,),
        in_specs=[pl.BlockSpec((bsz, d), lambda j: (0, 0)),
                  pl.BlockSpec((d, bn), lambda j: (0, j)),
                  pl.BlockSpec((1, bn), lambda j: (0, j))],
        out_specs=pl.BlockSpec((bsz, bn), lambda j: (0, j)),
        out_shape=jax.ShapeDtypeStruct((bsz, n), F32),
        name="ada",
    )(c, w_ada, b_ada.reshape(1, n))


def _ffn_body(x, ada_ref, ada_base, norm_ref, w_in_ref, w_out_ref):
    shift = ada_ref[0, ada_base:ada_base + 1, :]
    scale = ada_ref[0, ada_base + 1:ada_base + 2, :]
    gate = ada_ref[0, ada_base + 2:ada_base + 3, :]
    h = (_rmsnorm(x, norm_ref[...]) * (1.0 + scale) + shift).astype(BF16)
    acc = None
    for c0, cw in FFN_CHUNKS:
        g = _dot(h, w_in_ref[:, c0:c0 + cw])
        u = _dot(h, w_in_ref[:, D_FF + c0:D_FF + c0 + cw])
        a = (g * jax.nn.sigmoid(g) * u).astype(BF16)
        part = _dot(a, w_out_ref[c0:c0 + cw, :])
        acc = part if acc is None else acc + part
    return x + (FFN_RES_WEIGHT * gate) * acc


def _ffn1_kernel(x_ref, ada_ref, norm_ref, w_in_ref, w_out_ref, o_ref):
    o_ref[0] = _ffn_body(x_ref[0], ada_ref, 0, norm_ref, w_in_ref, w_out_ref)


def _ffn1(x, ada, norm, w_in, w_out, tm):
    bsz, s, d = x.shape
    row = pl.BlockSpec((1, tm, d), lambda b, i: (b, i, 0))
    return pl.pallas_call(
        _ffn1_kernel,
        grid=(bsz, s // tm),
        in_specs=[row,
                  pl.BlockSpec((1, 9, d), lambda b, i: (b, 0, 0)),
                  _const_spec((1, d)),
                  _const_spec(w_in.shape),
                  _const_spec(w_out.shape)],
        out_specs=row,
        out_shape=jax.ShapeDtypeStruct(x.shape, F32),
        compiler_params=pltpu.CompilerParams(
            dimension_semantics=("parallel", "parallel"),
            vmem_limit_bytes=VMEM_LIMIT_BYTES),
        name="ffn1",
    )(x, ada, norm, w_in, w_out)


PROJ_SUBTILES = 2


def _rope(v, cos_t, sin_t):
    return v * cos_t + pltpu.roll(v, LANES - MLA_ROPE, axis=1) * sin_t


def _proj_kernel(x_ref, ada_ref, norm_ref, pos_ref, invf_ref, w_in_ref, qn_ref, kvn_ref,
                 wq_ref, wuk_ref, wuvt_ref, vone_ref,
                 q_ref, k_ref, vt_ref, caq_ref, cak_ref, cavt_ref, ga_ref, gb_ref):
    shift = ada_ref[0, 3:4, :]
    scale = ada_ref[0, 4:5, :]
    qk_scale = (MLA_NOPE + MLA_ROPE) ** -0.5 * LOG2_E
    rows = x_ref.shape[1] // PROJ_SUBTILES
    one_rows = jnp.ones((MLA_NOPE, rows), F32)
    zero_head = jnp.zeros((MLA_NOPE, rows), F32)
    zero_tail = jnp.zeros((LANES - MLA_NOPE - MLA_ROPE, rows), F32)
    ones_pad = (lax.broadcasted_iota(jnp.int32, (HEAD_SLAB - CA_HEAD_DIM, rows), 0) == 0).astype(F32)
    for sub in range(PROJ_SUBTILES):
        rs = slice(sub * rows, (sub + 1) * rows)
        h = (_rmsnorm(x_ref[0, rs, :], norm_ref[...]) * (1.0 + scale) + shift).astype(BF16)
        z = _dot_nt(h, w_in_ref[...])

        ang = invf_ref[...] * pos_ref[0, 0, :, rs].astype(F32)
        cos_h, sin_h = jnp.cos(ang), jnp.sin(ang)
        cos_t = jnp.transpose(jnp.concatenate([one_rows, cos_h, cos_h, zero_tail], axis=0))
        sin_t = jnp.transpose(jnp.concatenate([zero_head, sin_h, sin_h, zero_tail], axis=0))

        cq = _rmsnorm(z[:, ZC_QLAT:ZC_QLAT + MLA_Q_RANK], qn_ref[...]).astype(BF16)
        qall = _dot(cq, wq_ref[...])
        ckv = _rmsnorm(z[:, ZC_KVLAT:ZC_KVLAT + MLA_KV_RANK], kvn_ref[...]).astype(BF16)
        kn = _dot(ckv, wuk_ref[...])
        kpe = _rope(z[:, ZC_KPE:ZC_KPE + LANES], cos_t, sin_t)
        for hd in range(MLA_HEADS):
            sl = slice(hd * HEAD_SLAB, (hd + 1) * HEAD_SLAB)
            q_ref[0, rs, sl] = (_rope(qall[:, sl], cos_t, sin_t) * qk_scale).astype(BF16)
            k_ref[0, rs, sl] = (kn[:, sl] + kpe).astype(BF16)
        vt_ref[0, :, rs] = (_dot_nt(wuvt_ref[...], ckv) + vone_ref[...]).astype(BF16)

        caq_ref[0, rs, :] = (z[:, ZC_CAQ:ZC_CAQ + CA_WIDTH]
                             * (CA_HEAD_DIM ** -0.5 * LOG2_E)).astype(BF16)
        cak_ref[0, rs, :] = z[:, ZC_CAK:ZC_CAK + CA_WIDTH].astype(BF16)
        cavt = jnp.transpose(z[:, ZC_CAV:ZC_CAV + CA_WIDTH])
        for hd in range(CA_HEADS):
            slab = jnp.concatenate([cavt[hd * CA_HEAD_DIM:(hd + 1) * CA_HEAD_DIM], ones_pad], axis=0)
            cavt_ref[0, hd * HEAD_SLAB:(hd + 1) * HEAD_SLAB, rs] = slab.astype(BF16)
        ga_ref[0, rs, :] = jax.nn.sigmoid(z[:, ZC_GA:ZC_GA + D_MODEL]).astype(BF16)
        gb_ref[0, rs, :] = jax.nn.sigmoid(z[:, ZC_GB:ZC_GB + D_MODEL]).astype(BF16)


def _proj(x, ada, norm, pos, invf, w_in_ext, qn, kvn, wq, wuk, wuvt, vone, tm):
    bsz, s, d = x.shape

    def row(width):
        return pl.BlockSpec((1, tm, width), lambda b, i: (b, i, 0))

    def out(width):
        return jax.ShapeDtypeStruct((bsz, s, width), BF16)

    slabs = MLA_HEADS * HEAD_SLAB
    widths = (slabs, slabs, None, CA_WIDTH, CA_WIDTH, None, D_MODEL, D_MODEL)
    vt_spec = pl.BlockSpec((1, slabs, tm), lambda b, i: (b, 0, i))
    vt_shape = jax.ShapeDtypeStruct((bsz, slabs, s), BF16)
    return pl.pallas_call(
        _proj_kernel,
        grid=(bsz, s // tm),
        in_specs=[row(d),
                  pl.BlockSpec((1, 9, d), lambda b, i: (b, 0, 0)),
                  _const_spec((1, d)),
                  pl.BlockSpec((1, 1, 1, tm), lambda b, i: (b, i, 0, 0)),
                  _const_spec(invf.shape),
                  _const_spec(w_in_ext.shape),
                  _const_spec(qn.shape), _const_spec(kvn.shape),
                  _const_spec(wq.shape),
                  _const_spec(wuk.shape), _const_spec(wuvt.shape), _const_spec(vone.shape)],
        out_specs=[vt_spec if w is None else row(w) for w in widths],
        out_shape=[vt_shape if w is None else out(w) for w in widths],
        compiler_params=pltpu.CompilerParams(
            dimension_semantics=("parallel", "parallel"),
            vmem_limit_bytes=VMEM_LIMIT_BYTES),
        name="proj",
    )(x, ada, norm, pos, invf, w_in_ext, qn, kvn, wq, wuk, wuvt, vone)


MLA_HEADS_PER_STEP = 8
MLA_LOOKAHEAD = 3


def _mla_kernel(q_ref, k_ref, vt_ref, o_ref, m_ref, acc_ref, *s_refs, blk, hps):
    qi = pl.program_id(2)
    m_ref[...] = jnp.full(m_ref.shape, NEG_INF, F32)
    acc_ref[...] = jnp.zeros(acc_ref.shape, F32)

    def scores(hd, k0, mask):
        sl = slice(hd * HEAD_SLAB, (hd + 1) * HEAD_SLAB)
        s = _dot_nt(k_ref[0, pl.ds(k0, blk), sl], q_ref[0, :, sl])
        s_refs[hd % len(s_refs)][...] = s if mask is None else jnp.where(mask, s, NEG_INF)

    def step(j, mask):
        k0 = pl.multiple_of(j * blk, blk)
        for hd in range(min(MLA_LOOKAHEAD, hps)):
            scores(hd, k0, mask)
        for hd in range(hps):
            if hd + MLA_LOOKAHEAD < hps:
                scores(hd + MLA_LOOKAHEAD, k0, mask)
            s = s_refs[hd % len(s_refs)][...]
            m_prev = m_ref[hd]
            m_new = jnp.maximum(m_prev, jnp.max(s, axis=0, keepdims=True))
            m_ref[hd] = m_new
            alpha = jnp.exp2(m_prev - m_new)
            p = jnp.exp2((s - m_new).astype(BF16))
            vt = vt_ref[0, hd * HEAD_SLAB:(hd + 1) * HEAD_SLAB, pl.ds(k0, blk)]
            acc_ref[hd] = acc_ref[hd] * alpha + _dot(vt, p)

    def body(j, carry):
        step(j, None)
        return carry

    lax.fori_loop(0, qi, body, 0)
    kc = lax.broadcasted_iota(jnp.int32, (blk, blk), 0) // CHUNK
    qc = lax.broadcasted_iota(jnp.int32, (blk, blk), 1) // CHUNK
    step(qi, kc <= qc)
    for pr in range(hps // 2):
        outs = []
        for hd in (2 * pr, 2 * pr + 1):
            a = acc_ref[hd]
            outs.append(a[:MLA_V] * (1.0 / a[MLA_V:MLA_V + 1]))
        o_ref[0, :, pr * LANES:(pr + 1) * LANES] = jnp.transpose(
            jnp.concatenate(outs, axis=0)).astype(BF16)


def _mla(q, k, vt, blk):
    bsz, s, _ = q.shape
    hps = MLA_HEADS_PER_STEP
    return pl.pallas_call(
        functools.partial(_mla_kernel, blk=blk, hps=hps),
        grid=(bsz, MLA_HEADS // hps, s // blk),
        in_specs=[pl.BlockSpec((1, blk, hps * HEAD_SLAB), lambda b, g, i: (b, i, g)),
                  pl.BlockSpec((1, s, hps * HEAD_SLAB), lambda b, g, i: (b, 0, g)),
                  pl.BlockSpec((1, hps * HEAD_SLAB, s), lambda b, g, i: (b, g, 0))],
        out_specs=pl.BlockSpec((1, blk, hps * MLA_V), lambda b, g, i: (b, i, g)),
        out_shape=jax.ShapeDtypeStruct((bsz, s, MLA_OUT_WIDTH), BF16),
        scratch_shapes=[pltpu.VMEM((hps, 1, blk), F32),
                        pltpu.VMEM((hps, HEAD_SLAB, blk), F32)]
                       + [pltpu.VMEM((blk, blk), F32)] * (MLA_LOOKAHEAD + 1),
        compiler_params=pltpu.CompilerParams(
            dimension_semantics=("parallel", "parallel", "arbitrary"),
            vmem_limit_bytes=VMEM_LIMIT_BYTES),
        name="mla",
    )(q, k, vt)


CA_TQ = 256
CA_TK = CA_TQ + CA_LEFT_CHUNKS * CHUNK
CA_KBLKS = CA_TK // CA_TQ
CA_TAB = 1024
CA_LOOKAHEAD = 3


def _ca_kernel(tab_ref, q_ref, *refs):
    k_refs, vt_refs = refs[:CA_KBLKS], refs[CA_KBLKS:2 * CA_KBLKS]
    o_ref, bias_ref = refs[2 * CA_KBLKS], refs[2 * CA_KBLKS + 1]
    b, i = pl.program_id(0), pl.program_id(1)

    @pl.when((b == 0) & (i == 0))
    def _build_bias():
        row = lax.broadcasted_iota(jnp.int32, (CA_TQ, CA_TAB), 0)
        kc = lax.broadcasted_iota(jnp.int32, (CA_TK, CA_TQ), 0) // CHUNK
        qc = lax.broadcasted_iota(jnp.int32, (CA_TK, CA_TQ), 1) // CHUNK
        band = (kc >= qc) & (kc <= qc + CA_LEFT_CHUNKS)
        for hd in range(CA_HEADS):
            t = jnp.broadcast_to(tab_ref[hd:hd + 1, :], (CA_TQ, CA_TAB))
            for bit in range(CA_TQ.bit_length() - 1):
                t = jnp.where(((row >> bit) & 1) == 1, pltpu.roll(t, 1 << bit, axis=1), t)
            bias_ref[hd] = jnp.where(band, jnp.transpose(t[:, :CA_TK]) * LOG2_E, NEG_INF)

    lo = lax.broadcasted_iota(jnp.int32, (1, LANES), 1) < CA_HEAD_DIM

    def attend(valid):
        def scores(hd):
            sl = slice((hd // 2) * LANES, (hd // 2 + 1) * LANES)
            q = q_ref[0, :, sl]
            zero = jnp.zeros_like(q)
            qh = jnp.where(lo, q, zero) if hd % 2 == 0 else jnp.where(lo, zero, q)
            kcat = jnp.concatenate([r[0, :, sl] for r in k_refs], axis=0)
            s = _dot_nt(kcat, qh) + bias_ref[hd]
            return s if valid is None else jnp.where(valid, s, NEG_INF)

        pending = {hd: scores(hd) for hd in range(CA_LOOKAHEAD)}
        outs = []
        for hd in range(CA_HEADS):
            s = pending.pop(hd)
            if hd + CA_LOOKAHEAD < CA_HEADS:
                pending[hd + CA_LOOKAHEAD] = scores(hd + CA_LOOKAHEAD)
            m = jnp.max(s, axis=0, keepdims=True)
            p = jnp.exp2((s - m).astype(BF16))
            rows = slice(hd * HEAD_SLAB, (hd + 1) * HEAD_SLAB)
            vt = jnp.concatenate([r[0, rows, :] for r in vt_refs], axis=1)
            a = _dot(vt, p)
            outs.append(a[:CA_HEAD_DIM] * (1.0 / a[CA_HEAD_DIM:CA_HEAD_DIM + 1]))
            if hd % 2 == 1:
                pr = hd // 2
                o_ref[0, :, pr * LANES:(pr + 1) * LANES] = jnp.transpose(
                    jnp.concatenate(outs, axis=0)).astype(BF16)
                outs = []

    @pl.when(i >= CA_KBLKS - 1)
    def _interior():
        attend(None)

    @pl.when(i < CA_KBLKS - 1)
    def _sequence_start():
        krow = lax.broadcasted_iota(jnp.int32, (CA_TK, 1), 0)
        attend(krow >= (CA_TK - CA_TQ) - i * CA_TQ)


def _ca(tab, q, k, vt):
    bsz, s, _ = q.shape

    def k_spec(back):
        return pl.BlockSpec((1, CA_TQ, CA_WIDTH), lambda b, i: (b, jnp.maximum(i - back, 0), 0))

    def vt_spec(back):
        return pl.BlockSpec((1, CA_HEADS * HEAD_SLAB, CA_TQ),
                            lambda b, i: (b, 0, jnp.maximum(i - back, 0)))

    blk = pl.BlockSpec((1, CA_TQ, CA_WIDTH), lambda b, i: (b, i, 0))
    backs = tuple(range(CA_KBLKS - 1, -1, -1))
    return pl.pallas_call(
        _ca_kernel,
        grid=(bsz, s // CA_TQ),
        in_specs=[_const_spec(tab.shape), blk] + [k_spec(x) for x in backs]
                 + [vt_spec(x) for x in backs],
        out_specs=blk,
        out_shape=jax.ShapeDtypeStruct((bsz, s, CA_WIDTH), BF16),
        scratch_shapes=[pltpu.VMEM((CA_HEADS, CA_TK, CA_TQ), F32)],
        compiler_params=pltpu.CompilerParams(
            dimension_semantics=("arbitrary", "arbitrary"),
            vmem_limit_bytes=VMEM_LIMIT_BYTES),
        name="ca",
    )(tab, q, *([k] * CA_KBLKS), *([vt] * CA_KBLKS))


def _tail_kernel(x_ref, ada_ref, aa_ref, ab_ref, ga_ref, gb_ref, wa_ref, wb_ref, wo_ref,
                 norm_ref, w_in_ref, w_out_ref, fnorm_ref, o_ref):
    ya = _dot(aa_ref[0], wa_ref[...])
    yb = _dot(ab_ref[0], wb_ref[...])
    merged = (ga_ref[0].astype(F32) * ya + gb_ref[0].astype(F32) * yb).astype(BF16)
    x = x_ref[0] + ada_ref[0, 5:6, :] * _dot(merged, wo_ref[...])
    x = _ffn_body(x, ada_ref, 6, norm_ref, w_in_ref, w_out_ref)
    o_ref[0] = _rmsnorm(x, fnorm_ref[...])


def _tail(x, ada, aa, ab, ga, gb, wa, wb, wo, norm, w_in, w_out, fnorm, tm):
    bsz, s, d = x.shape

    def row(width):
        return pl.BlockSpec((1, tm, width), lambda b, i: (b, i, 0))

    return pl.pallas_call(
        _tail_kernel,
        grid=(bsz, s // tm),
        in_specs=[row(d),
                  pl.BlockSpec((1, 9, d), lambda b, i: (b, 0, 0)),
                  row(MLA_OUT_WIDTH), row(CA_WIDTH), row(d), row(d),
                  _const_spec(wa.shape), _const_spec(wb.shape), _const_spec(wo.shape),
                  _const_spec((1, d)), _const_spec(w_in.shape), _const_spec(w_out.shape),
                  _const_spec((1, d))],
        out_specs=row(d),
        out_shape=jax.ShapeDtypeStruct(x.shape, F32),
        compiler_params=pltpu.CompilerParams(
            dimension_semantics=("parallel", "parallel"),
            vmem_limit_bytes=VMEM_LIMIT_BYTES),
        name="tail",
    )(x, ada, aa, ab, ga, gb, wa, wb, wo, norm, w_in, w_out, fnorm)


def _rotate_half_cols(w):
    half = w.shape[-1] // 2
    return jnp.concatenate([-w[..., half:], w[..., :half]], axis=-1)


def _prep_w_in(w_in):
    d = w_in.shape[0]
    o = 0
    pieces = {}
    for name, width in (("qlat", MLA_Q_RANK), ("kvlat", MLA_KV_RANK), ("kpe", MLA_ROPE),
                        ("caq", CA_WIDTH), ("cak", CA_WIDTH), ("cav", CA_WIDTH),
                        ("ga", D_MODEL), ("gb", D_MODEL)):
        pieces[name] = w_in[:, o:o + width]
        o += width
    zl = jnp.zeros((d, MLA_NOPE), w_in.dtype)
    kpe = jnp.concatenate([zl, pieces["kpe"], _rotate_half_cols(pieces["kpe"])], axis=1)
    ext = jnp.concatenate([pieces["qlat"].T, pieces["kvlat"].T, kpe.T, pieces["caq"].T,
                           pieces["cak"].T, pieces["cav"].T, pieces["ga"].T, pieces["gb"].T], axis=0)
    assert ext.shape[0] == Z_COLS
    return ext.astype(BF16)


def _prep_w_uq(w_uq):
    r = w_uq.shape[0]
    w = w_uq.reshape(r, MLA_HEADS, MLA_NOPE + MLA_ROPE)
    nope, pe = w[..., :MLA_NOPE], w[..., MLA_NOPE:]
    slab = jnp.concatenate([nope, pe, _rotate_half_cols(pe)], axis=-1)
    return slab.reshape(r, MLA_HEADS * HEAD_SLAB).astype(BF16)


def _prep_w_ukv(w_ukv):
    r = w_ukv.shape[0]
    w = w_ukv.reshape(r, MLA_HEADS, MLA_NOPE + MLA_V)
    k_nope, v = w[..., :MLA_NOPE], w[..., MLA_NOPE:]
    zk = jnp.zeros((r, MLA_HEADS, HEAD_SLAB - MLA_NOPE), w.dtype)
    wuk = jnp.concatenate([k_nope, zk], axis=-1).reshape(r, MLA_HEADS * HEAD_SLAB)
    zv = jnp.zeros((r, MLA_HEADS, HEAD_SLAB - MLA_V), w.dtype)
    wuvt = jnp.concatenate([v, zv], axis=-1).reshape(r, MLA_HEADS * HEAD_SLAB).T
    vone = (jnp.arange(MLA_HEADS * HEAD_SLAB) % HEAD_SLAB == MLA_V).astype(F32).reshape(-1, 1)
    return wuk.astype(BF16), wuvt.astype(BF16), vone


def _prep_bias_table(rel_bias):
    j = jnp.arange(CA_TAB)
    t = (j + CA_TQ - 1) % CA_TAB
    rel = (CA_TK - 1) - t
    idx = jnp.clip(rel, -MAX_REL_DIST, MAX_REL_DIST) + MAX_REL_DIST
    return rel_bias[idx].T.astype(F32)


def kernel(x, c, positions, w_ada, b_ada, ffn1_norm, ffn1_w_in, ffn1_w_out, mix_norm, w_in,
           mla_q_norm, mla_w_uq, mla_kv_norm, mla_w_ukv, rel_bias, w_branch_a, w_branch_b,
           w_out, ffn2_norm, ffn2_w_in, ffn2_w_out, final_norm):
    bsz, s, d = x.shape
    tm = 512
    for l in range(w_ada.shape[0]):
        ada = _ada(c, w_ada[l], b_ada[l]).reshape(bsz, 9, d)
        x = _ffn1(x, ada, ffn1_norm[l].reshape(1, d), ffn1_w_in[l].astype(BF16),
                  ffn1_w_out[l].astype(BF16), tm)

        freq = jnp.arange(0, MLA_ROPE, 2, dtype=F32) / MLA_ROPE
        inv_freq = ROPE_THETA ** (-freq)
        invf = inv_freq.reshape(MLA_ROPE // 2, 1)
        wq = _prep_w_uq(mla_w_uq[l])
        wuk, wuvt, vone = _prep_w_ukv(mla_w_ukv[l])
        q, k, vt, caq, cak, cavt, ga, gb = _proj(
            x, ada, mix_norm[l].reshape(1, d), positions.reshape(bsz, s // tm, 1, tm), invf,
            _prep_w_in(w_in[l]), mla_q_norm[l].reshape(1, -1), mla_kv_norm[l].reshape(1, -1),
            wq, wuk, wuvt, vone, tm)
        attn_a = _mla(q, k, vt, 512)
        attn_b = _ca(_prep_bias_table(rel_bias[l]), caq, cak, cavt)
        last = l == w_ada.shape[0] - 1
        assert last, "the fused tail applies the final norm; DEPTH is 1"
        x = _tail(x, ada, attn_a, attn_b, ga, gb, w_branch_a[l].astype(BF16),
                  w_branch_b[l].astype(BF16), w_out[l].astype(BF16),
                  ffn2_norm[l].reshape(1, d), ffn2_w_in[l].astype(BF16),
                  ffn2_w_out[l].astype(BF16), final_norm.reshape(1, d), tm)
    return x
```

```python
import functools

import jax
import jax.numpy as jnp
from jax import lax
from jax.experimental import pallas as pl
from jax.experimental.pallas import tpu as pltpu

D_MODEL = 1024
CHUNK = 64
D_FF = 2816
FFN_RES_WEIGHT = 0.5
MLA_HEADS = 8
MLA_Q_RANK = 256
MLA_KV_RANK = 128
MLA_NOPE = 64
MLA_ROPE = 32
MLA_V = 64
ROPE_THETA = 10000.0
CA_HEADS = 8
CA_HEAD_DIM = 64
CA_LEFT_CHUNKS = 8
MAX_REL_DIST = 256
CA_WIDTH = CA_HEADS * CA_HEAD_DIM
MLA_OUT_WIDTH = MLA_HEADS * MLA_V
EPS = 1e-6
NEG_INF = -1e30
LOG2_E = 1.4426950408889634

LANES = 128
HEAD_SLAB = LANES
VMEM_LIMIT_BYTES = 56 * 1024 * 1024

ZC_QLAT = 0
ZC_KVLAT = ZC_QLAT + MLA_Q_RANK
ZC_KPE = ZC_KVLAT + MLA_KV_RANK
ZC_CAQ = ZC_KPE + LANES
ZC_CAK = ZC_CAQ + CA_WIDTH
ZC_CAV = ZC_CAK + CA_WIDTH
ZC_GA = ZC_CAV + CA_WIDTH
ZC_GB = ZC_GA + D_MODEL
Z_COLS = ZC_GB + D_MODEL

FFN_CHUNKS = ((0, 1024), (1024, 1024), (2048, 768))

BF16 = jnp.bfloat16
F32 = jnp.float32


def _const_spec(shape):
    nd = len(shape)
    return pl.BlockSpec(shape, lambda *_: (0,) * nd, pipeline_mode=pl.Buffered(1))


def _rmsnorm(x, g):
    return x * lax.rsqrt(jnp.mean(x * x, axis=-1, keepdims=True) + EPS) * g


def _dot(a, b):
    return jnp.dot(a, b, preferred_element_type=F32)


def _dot_nt(a, b):
    return lax.dot_general(a, b, (((1,), (1,)), ((), ())), preferred_element_type=F32)


def _ada_kernel(c_ref, w_ref, b_ref, o_ref):
    c = c_ref[...]
    c_act = c * jax.nn.sigmoid(c)
    o_ref[...] = jnp.dot(c_act, w_ref[...], preferred_element_type=F32,
                         precision=lax.Precision.HIGHEST) + b_ref[...]


def _ada(c, w_ada, b_ada):
    bsz, d = c.shape
    n = w_ada.shape[1]
    bn = 1152
    return pl.pallas_call(
        _ada_kernel,
        grid=(n // bn,),
        in_specs=[pl.BlockSpec((bsz, d), lambda j: (0, 0)),
                  pl.BlockSpec((d, bn), lambda j: (0, j)),
                  pl.BlockSpec((1, bn), lambda j: (0, j))],
        out_specs=pl.BlockSpec((bsz, bn), lambda j: (0, j)),
        out_shape=jax.ShapeDtypeStruct((bsz, n), F32),
        name="ada",
    )(c, w_ada, b_ada.reshape(1, n))


def _ffn_body(x, ada_ref, ada_base, norm_ref, w_in_ref, w_out_ref):
    shift = ada_ref[0, ada_base:ada_base + 1, :]
    scale = ada_ref[0, ada_base + 1:ada_base + 2, :]
    gate = ada_ref[0, ada_base + 2:ada_base + 3, :]
    h = (_rmsnorm(x, norm_ref[...]) * (1.0 + scale) + shift).astype(BF16)
    acc = None
    for c0, cw in FFN_CHUNKS:
        g = _dot(h, w_in_ref[:, c0:c0 + cw])
        u = _dot(h, w_in_ref[:, D_FF + c0:D_FF + c0 + cw])
        a = (g * jax.nn.sigmoid(g) * u).astype(BF16)
        part = _dot(a, w_out_ref[c0:c0 + cw, :])
        acc = part if acc is None else acc + part
    return x + (FFN_RES_WEIGHT * gate) * acc


def _ffn1_kernel(x_ref, ada_ref, norm_ref, w_in_ref, w_out_ref, o_ref):
    o_ref[0] = _ffn_body(x_ref[0], ada_ref, 0, norm_ref, w_in_ref, w_out_ref)


def _ffn1(x, ada, norm, w_in, w_out, tm):
    bsz, s, d = x.shape
    row = pl.BlockSpec((1, tm, d), lambda b, i: (b, i, 0))
    return pl.pallas_call(
        _ffn1_kernel,
        grid=(bsz, s // tm),
        in_specs=[row,
                  pl.BlockSpec((1, 9, d), lambda b, i: (b, 0, 0)),
                  _const_spec((1, d)),
                  _const_spec(w_in.shape),
                  _const_spec(w_out.shape)],
        out_specs=row,
        out_shape=jax.ShapeDtypeStruct(x.shape, F32),
        compiler_params=pltpu.CompilerParams(
            dimension_semantics=("parallel", "parallel"),
            vmem_limit_bytes=VMEM_LIMIT_BYTES),
        name="ffn1",
    )(x, ada, norm, w_in, w_out)


PROJ_SUBTILES = 2


def _rope(v, cos_t, sin_t):
    return v * cos_t + pltpu.roll(v, LANES - MLA_ROPE, axis=1) * sin_t


def _proj_kernel(x_ref, ada_ref, norm_ref, pos_ref, invf_ref, w_in_ref, qn_ref, kvn_ref,
                 wq_ref, wuk_ref, wuvt_ref, vone_ref,
                 q_ref, k_ref, vt_ref, caq_ref, cak_ref, cavt_ref, ga_ref, gb_ref):
    shift = ada_ref[0, 3:4, :]
    scale = ada_ref[0, 4:5, :]
    qk_scale = (MLA_NOPE + MLA_ROPE) ** -0.5 * LOG2_E
    rows = x_ref.shape[1] // PROJ_SUBTILES
    one_rows = jnp.ones((MLA_NOPE, rows), F32)
    zero_head = jnp.zeros((MLA_NOPE, rows), F32)
    zero_tail = jnp.zeros((LANES - MLA_NOPE - MLA_ROPE, rows), F32)
    ones_pad = (lax.broadcasted_iota(jnp.int32, (HEAD_SLAB - CA_HEAD_DIM, rows), 0) == 0).astype(F32)
    for sub in range(PROJ_SUBTILES):
        rs = slice(sub * rows, (sub + 1) * rows)
        h = (_rmsnorm(x_ref[0, rs, :], norm_ref[...]) * (1.0 + scale) + shift).astype(BF16)
        z = _dot_nt(h, w_in_ref[...])

        ang = invf_ref[...] * pos_ref[0, 0, :, rs].astype(F32)
        cos_h, sin_h = jnp.cos(ang), jnp.sin(ang)
        cos_t = jnp.transpose(jnp.concatenate([one_rows, cos_h, cos_h, zero_tail], axis=0))
        sin_t = jnp.transpose(jnp.concatenate([zero_head, sin_h, sin_h, zero_tail], axis=0))

        cq = _rmsnorm(z[:, ZC_QLAT:ZC_QLAT + MLA_Q_RANK], qn_ref[...]).astype(BF16)
        qall = _dot(cq, wq_ref[...])
        ckv = _rmsnorm(z[:, ZC_KVLAT:ZC_KVLAT + MLA_KV_RANK], kvn_ref[...]).astype(BF16)
        kn = _dot(ckv, wuk_ref[...])
        kpe = _rope(z[:, ZC_KPE:ZC_KPE + LANES], cos_t, sin_t)
        for hd in range(MLA_HEADS):
            sl = slice(hd * HEAD_SLAB, (hd + 1) * HEAD_SLAB)
            q_ref[0, rs, sl] = (_rope(qall[:, sl], cos_t, sin_t) * qk_scale).astype(BF16)
            k_ref[0, rs, sl] = (kn[:, sl] + kpe).astype(BF16)
        vt_ref[0, :, rs] = (_dot_nt(wuvt_ref[...], ckv) + vone_ref[...]).astype(BF16)

        caq_ref[0, rs, :] = (z[:, ZC_CAQ:ZC_CAQ + CA_WIDTH]
                             * (CA_HEAD_DIM ** -0.5 * LOG2_E)).astype(BF16)
        cak_ref[0, rs, :] = z[:, ZC_CAK:ZC_CAK + CA_WIDTH].astype(BF16)
        cavt = jnp.transpose(z[:, ZC_CAV:ZC_CAV + CA_WIDTH])
        for hd in range(CA_HEADS):
            slab = jnp.concatenate([cavt[hd * CA_HEAD_DIM:(hd + 1) * CA_HEAD_DIM], ones_pad], axis=0)
            cavt_ref[0, hd * HEAD_SLAB:(hd + 1) * HEAD_SLAB, rs] = slab.astype(BF16)
        ga_ref[0, rs, :] = jax.nn.sigmoid(z[:, ZC_GA:ZC_GA + D_MODEL]).astype(BF16)
        gb_ref[0, rs, :] = jax.nn.sigmoid(z[:, ZC_GB:ZC_GB + D_MODEL]).astype(BF16)


def _proj(x, ada, norm, pos, invf, w_in_ext, qn, kvn, wq, wuk, wuvt, vone, tm):
    bsz, s, d = x.shape

    def row(width):
        return pl.BlockSpec((1, tm, width), lambda b, i: (b, i, 0))

    def out(width):
        return jax.ShapeDtypeStruct((bsz, s, width), BF16)

    slabs = MLA_HEADS * HEAD_SLAB
    widths = (slabs, slabs, None, CA_WIDTH, CA_WIDTH, None, D_MODEL, D_MODEL)
    vt_spec = pl.BlockSpec((1, slabs, tm), lambda b, i: (b, 0, i))
    vt_shape = jax.ShapeDtypeStruct((bsz, slabs, s), BF16)
    return pl.pallas_call(
        _proj_kernel,
        grid=(bsz, s // tm),
        in_specs=[row(d),
                  pl.BlockSpec((1, 9, d), lambda b, i: (b, 0, 0)),
                  _const_spec((1, d)),
                  pl.BlockSpec((1, 1, 1, tm), lambda b, i: (b, i, 0, 0)),
                  _const_spec(invf.shape),
                  _const_spec(w_in_ext.shape),
                  _const_spec(qn.shape), _const_spec(kvn.shape),
                  _const_spec(wq.shape),
                  _const_spec(wuk.shape), _const_spec(wuvt.shape), _const_spec(vone.shape)],
        out_specs=[vt_spec if w is None else row(w) for w in widths],
        out_shape=[vt_shape if w is None else out(w) for w in widths],
        compiler_params=pltpu.CompilerParams(
            dimension_semantics=("parallel", "parallel"),
            vmem_limit_bytes=VMEM_LIMIT_BYTES),
        name="proj",
    )(x, ada, norm, pos, invf, w_in_ext, qn, kvn, wq, wuk, wuvt, vone)


MLA_HEADS_PER_STEP = 8
MLA_LOOKAHEAD = 3


def _mla_kernel(q_ref, k_ref, vt_ref, o_ref, m_ref, acc_ref, *s_refs, blk, hps):
    qi = pl.program_id(2)
    m_ref[...] = jnp.full(m_ref.shape, NEG_INF, F32)
    acc_ref[...] = jnp.zeros(acc_ref.shape, F32)

    def scores(hd, k0, mask):
        sl = slice(hd * HEAD_SLAB, (hd + 1) * HEAD_SLAB)
        s = _dot_nt(k_ref[0, pl.ds(k0, blk), sl], q_ref[0, :, sl])
        s_refs[hd % len(s_refs)][...] = s if mask is None else jnp.where(mask, s, NEG_INF)

    def step(j, mask):
        k0 = pl.multiple_of(j * blk, blk)
        for hd in range(min(MLA_LOOKAHEAD, hps)):
            scores(hd, k0, mask)
        for hd in range(hps):
            if hd + MLA_LOOKAHEAD < hps:
                scores(hd + MLA_LOOKAHEAD, k0, mask)
            s = s_refs[hd % len(s_refs)][...]
            m_prev = m_ref[hd]
            m_new = jnp.maximum(m_prev, jnp.max(s, axis=0, keepdims=True))
            m_ref[hd] = m_new
            alpha = jnp.exp2(m_prev - m_new)
            p = jnp.exp2((s - m_new).astype(BF16))
            vt = vt_ref[0, hd * HEAD_SLAB:(hd + 1) * HEAD_SLAB, pl.ds(k0, blk)]
            acc_ref[hd] = acc_ref[hd] * alpha + _dot(vt, p)

    def body(j, carry):
        step(j, None)
        return carry

    lax.fori_loop(0, qi, body, 0)
    kc = lax.broadcasted_iota(jnp.int32, (blk, blk), 0) // CHUNK
    qc = lax.broadcasted_iota(jnp.int32, (blk, blk), 1) // CHUNK
    step(qi, kc <= qc)
    for pr in range(hps // 2):
        outs = []
        for hd in (2 * pr, 2 * pr + 1):
            a = acc_ref[hd]
            outs.append(a[:MLA_V] * (1.0 / a[MLA_V:MLA_V + 1]))
        o_ref[0, :, pr * LANES:(pr + 1) * LANES] = jnp.transpose(
            jnp.concatenate(outs, axis=0)).astype(BF16)


def _mla(q, k, vt, blk):
    bsz, s, _ = q.shape
    hps = MLA_HEADS_PER_STEP
    return pl.pallas_call(
        functools.partial(_mla_kernel, blk=blk, hps=hps),
        grid=(bsz, MLA_HEADS // hps, s // blk),
        in_specs=[pl.BlockSpec((1, blk, hps * HEAD_SLAB), lambda b, g, i: (b, i, g)),
                  pl.BlockSpec((1, s, hps * HEAD_SLAB), lambda b, g, i: (b, 0, g)),
                  pl.BlockSpec((1, hps * HEAD_SLAB, s), lambda b, g, i: (b, g, 0))],
        out_specs=pl.BlockSpec((1, blk, hps * MLA_V), lambda b, g, i: (b, i, g)),
        out_shape=jax.ShapeDtypeStruct((bsz, s, MLA_OUT_WIDTH), BF16),
        scratch_shapes=[pltpu.VMEM((hps, 1, blk), F32),
                        pltpu.VMEM((hps, HEAD_SLAB, blk), F32)]
                       + [pltpu.VMEM((blk, blk), F32)] * (MLA_LOOKAHEAD + 1),
        compiler_params=pltpu.CompilerParams(
            dimension_semantics=("parallel", "parallel", "arbitrary"),
            vmem_limit_bytes=VMEM_LIMIT_BYTES),
        name="mla",
    )(q, k, vt)


CA_TQ = 256
CA_TK = CA_TQ + CA_LEFT_CHUNKS * CHUNK
CA_KBLKS = CA_TK // CA_TQ
CA_TAB = 1024
CA_LOOKAHEAD = 3


def _ca_kernel(tab_ref, q_ref, k_ref, vt_ref, o_ref, bias_ref):
    b, i = pl.program_id(0), pl.program_id(1)

    @pl.when((b == 0) & (i == 0))
    def _build_bias():
        row = lax.broadcasted_iota(jnp.int32, (CA_TQ, CA_TAB), 0)
        kc = lax.broadcasted_iota(jnp.int32, (CA_TK, CA_TQ), 0) // CHUNK
        qc = lax.broadcasted_iota(jnp.int32, (CA_TK, CA_TQ), 1) // CHUNK
        band = (kc >= qc) & (kc <= qc + CA_LEFT_CHUNKS)
        for hd in range(CA_HEADS):
            t = jnp.broadcast_to(tab_ref[hd:hd + 1, :], (CA_TQ, CA_TAB))
            for bit in range(CA_TQ.bit_length() - 1):
                t = jnp.where(((row >> bit) & 1) == 1, pltpu.roll(t, 1 << bit, axis=1), t)
            bias_ref[hd] = jnp.where(band, jnp.transpose(t[:, :CA_TK]) * LOG2_E, NEG_INF)

    lo = lax.broadcasted_iota(jnp.int32, (1, LANES), 1) < CA_HEAD_DIM

    def attend(nkb):
        nk = nkb * CA_TQ
        k0 = pl.multiple_of((i - (nkb - 1)) * CA_TQ, CA_TQ)

        def scores(hd):
            sl = slice((hd // 2) * LANES, (hd // 2 + 1) * LANES)
            q = q_ref[0, :, sl]
            zero = jnp.zeros_like(q)
            qh = jnp.where(lo, q, zero) if hd % 2 == 0 else jnp.where(lo, zero, q)
            return _dot_nt(k_ref[0, pl.ds(k0, nk), sl], qh) + bias_ref[hd, CA_TK - nk:, :]

        pending = {hd: scores(hd) for hd in range(CA_LOOKAHEAD)}
        outs = []
        for hd in range(CA_HEADS):
            s = pending.pop(hd)
            if hd + CA_LOOKAHEAD < CA_HEADS:
                pending[hd + CA_LOOKAHEAD] = scores(hd + CA_LOOKAHEAD)
            m = jnp.max(s, axis=0, keepdims=True)
            p = jnp.exp2((s - m).astype(BF16))
            vt = vt_ref[0, hd * HEAD_SLAB:(hd + 1) * HEAD_SLAB, pl.ds(k0, nk)]
            a = _dot(vt, p)
            outs.append(a[:CA_HEAD_DIM] * (1.0 / a[CA_HEAD_DIM:CA_HEAD_DIM + 1]))
            if hd % 2 == 1:
                pr = hd // 2
                o_ref[0, :, pr * LANES:(pr + 1) * LANES] = jnp.transpose(
                    jnp.concatenate(outs, axis=0)).astype(BF16)
                outs = []

    for nkb in range(1, CA_KBLKS):
        pl.when(i == nkb - 1)(functools.partial(attend, nkb))
    pl.when(i >= CA_KBLKS - 1)(functools.partial(attend, CA_KBLKS))


def _ca(tab, q, k, vt):
    bsz, s, _ = q.shape
    blk = pl.BlockSpec((1, CA_TQ, CA_WIDTH), lambda b, i: (b, i, 0))
    return pl.pallas_call(
        _ca_kernel,
        grid=(bsz, s // CA_TQ),
        in_specs=[_const_spec(tab.shape), blk,
                  pl.BlockSpec((1, s, CA_WIDTH), lambda b, i: (b, 0, 0)),
                  pl.BlockSpec((1, CA_HEADS * HEAD_SLAB, s), lambda b, i: (b, 0, 0))],
        out_specs=blk,
        out_shape=jax.ShapeDtypeStruct((bsz, s, CA_WIDTH), BF16),
        scratch_shapes=[pltpu.VMEM((CA_HEADS, CA_TK, CA_TQ), F32)],
        compiler_params=pltpu.CompilerParams(
            dimension_semantics=("arbitrary", "arbitrary"),
            vmem_limit_bytes=VMEM_LIMIT_BYTES),
        name="ca",
    )(tab, q, k, vt)


def _tail_kernel(x_ref, ada_ref, aa_ref, ab_ref, ga_ref, gb_ref, wa_ref, wb_ref, wo_ref,
                 norm_ref, w_in_ref, w_out_ref, fnorm_ref, o_ref):
    ya = _dot(aa_ref[0], wa_ref[...])
    yb = _dot(ab_ref[0], wb_ref[...])
    merged = (ga_ref[0].astype(F32) * ya + gb_ref[0].astype(F32) * yb).astype(BF16)
    x = x_ref[0] + ada_ref[0, 5:6, :] * _dot(merged, wo_ref[...])
    x = _ffn_body(x, ada_ref, 6, norm_ref, w_in_ref, w_out_ref)
    o_ref[0] = _rmsnorm(x, fnorm_ref[...])


def _tail(x, ada, aa, ab, ga, gb, wa, wb, wo, norm, w_in, w_out, fnorm, tm):
    bsz, s, d = x.shape

    def row(width):
        return pl.BlockSpec((1, tm, width), lambda b, i: (b, i, 0))

    return pl.pallas_call(
        _tail_kernel,
        grid=(bsz, s // tm),
        in_specs=[row(d),
                  pl.BlockSpec((1, 9, d), lambda b, i: (b, 0, 0)),
                  row(MLA_OUT_WIDTH), row(CA_WIDTH), row(d), row(d),
                  _const_spec(wa.shape), _const_spec(wb.shape), _const_spec(wo.shape),
                  _const_spec((1, d)), _const_spec(w_in.shape), _const_spec(w_out.shape),
                  _const_spec((1, d))],
        out_specs=row(d),
        out_shape=jax.ShapeDtypeStruct(x.shape, F32),
        compiler_params=pltpu.CompilerParams(
            dimension_semantics=("parallel", "parallel"),
            vmem_limit_bytes=VMEM_LIMIT_BYTES),
        name="tail",
    )(x, ada, aa, ab, ga, gb, wa, wb, wo, norm, w_in, w_out, fnorm)


def _rotate_half_cols(w):
    half = w.shape[-1] // 2
    return jnp.concatenate([-w[..., half:], w[..., :half]], axis=-1)


def _prep_w_in(w_in):
    d = w_in.shape[0]
    o = 0
    pieces = {}
    for name, width in (("qlat", MLA_Q_RANK), ("kvlat", MLA_KV_RANK), ("kpe", MLA_ROPE),
                        ("caq", CA_WIDTH), ("cak", CA_WIDTH), ("cav", CA_WIDTH),
                        ("ga", D_MODEL), ("gb", D_MODEL)):
        pieces[name] = w_in[:, o:o + width]
        o += width
    zl = jnp.zeros((d, MLA_NOPE), w_in.dtype)
    kpe = jnp.concatenate([zl, pieces["kpe"], _rotate_half_cols(pieces["kpe"])], axis=1)
    ext = jnp.concatenate([pieces["qlat"].T, pieces["kvlat"].T, kpe.T, pieces["caq"].T,
                           pieces["cak"].T, pieces["cav"].T, pieces["ga"].T, pieces["gb"].T], axis=0)
    assert ext.shape[0] == Z_COLS
    return ext.astype(BF16)


def _prep_w_uq(w_uq):
    r = w_uq.shape[0]
    w = w_uq.reshape(r, MLA_HEADS, MLA_NOPE + MLA_ROPE)
    nope, pe = w[..., :MLA_NOPE], w[..., MLA_NOPE:]
    slab = jnp.concatenate([nope, pe, _rotate_half_cols(pe)], axis=-1)
    return slab.reshape(r, MLA_HEADS * HEAD_SLAB).astype(BF16)


def _prep_w_ukv(w_ukv):
    r = w_ukv.shape[0]
    w = w_ukv.reshape(r, MLA_HEADS, MLA_NOPE + MLA_V)
    k_nope, v = w[..., :MLA_NOPE], w[..., MLA_NOPE:]
    zk = jnp.zeros((r, MLA_HEADS, HEAD_SLAB - MLA_NOPE), w.dtype)
    wuk = jnp.concatenate([k_nope, zk], axis=-1).reshape(r, MLA_HEADS * HEAD_SLAB)
    zv = jnp.zeros((r, MLA_HEADS, HEAD_SLAB - MLA_V), w.dtype)
    wuvt = jnp.concatenate([v, zv], axis=-1).reshape(r, MLA_HEADS * HEAD_SLAB).T
    vone = (jnp.arange(MLA_HEADS * HEAD_SLAB) % HEAD_SLAB == MLA_V).astype(F32).reshape(-1, 1)
    return wuk.astype(BF16), wuvt.astype(BF16), vone


def _prep_bias_table(rel_bias):
    j = jnp.arange(CA_TAB)
    t = (j + CA_TQ - 1) % CA_TAB
    rel = (CA_TK - 1) - t
    idx = jnp.clip(rel, -MAX_REL_DIST, MAX_REL_DIST) + MAX_REL_DIST
    return rel_bias[idx].T.astype(F32)


def kernel(x, c, positions, w_ada, b_ada, ffn1_norm, ffn1_w_in, ffn1_w_out, mix_norm, w_in,
           mla_q_norm, mla_w_uq, mla_kv_norm, mla_w_ukv, rel_bias, w_branch_a, w_branch_b,
           w_out, ffn2_norm, ffn2_w_in, ffn2_w_out, final_norm):
    bsz, s, d = x.shape
    tm = 512
    for l in range(w_ada.shape[0]):
        ada = _ada(c, w_ada[l], b_ada[l]).reshape(bsz, 9, d)
        x = _ffn1(x, ada, ffn1_norm[l].reshape(1, d), ffn1_w_in[l].astype(BF16),
                  ffn1_w_out[l].astype(BF16), tm)

        freq = jnp.arange(0, MLA_ROPE, 2, dtype=F32) / MLA_ROPE
        inv_freq = ROPE_THETA ** (-freq)
        invf = inv_freq.reshape(MLA_ROPE // 2, 1)
        wq = _prep_w_uq(mla_w_uq[l])
        wuk, wuvt, vone = _prep_w_ukv(mla_w_ukv[l])
        q, k, vt, caq, cak, cavt, ga, gb = _proj(
            x, ada, mix_norm[l].reshape(1, d), positions.reshape(bsz, s // tm, 1, tm), invf,
            _prep_w_in(w_in[l]), mla_q_norm[l].reshape(1, -1), mla_kv_norm[l].reshape(1, -1),
            wq, wuk, wuvt, vone, tm)
        attn_a = _mla(q, k, vt, 512)
        attn_b = _ca(_prep_bias_table(rel_bias[l]), caq, cak, cavt)
        last = l == w_ada.shape[0] - 1
        assert last, "the fused tail applies the final norm; DEPTH is 1"
        x = _tail(x, ada, attn_a, attn_b, ga, gb, w_branch_a[l].astype(BF16),
                  w_branch_b[l].astype(BF16), w_out[l].astype(BF16),
                  ffn2_norm[l].reshape(1, d), ffn2_w_in[l].astype(BF16),
                  ffn2_w_out[l].astype(BF16), final_norm.reshape(1, d), tm)
    return x
```

```python
import functools

import jax
import jax.numpy as jnp
from jax import lax
from jax.experimental import pallas as pl
from jax.experimental.pallas import tpu as pltpu

D_MODEL = 1024
CHUNK = 64
D_FF = 2816
FFN_RES_WEIGHT = 0.5
MLA_HEADS = 8
MLA_Q_RANK = 256
MLA_KV_RANK = 128
MLA_NOPE = 64
MLA_ROPE = 32
MLA_V = 64
ROPE_THETA = 10000.0
CA_HEADS = 8
CA_HEAD_DIM = 64
CA_LEFT_CHUNKS = 8
MAX_REL_DIST = 256
CA_WIDTH = CA_HEADS * CA_HEAD_DIM
MLA_OUT_WIDTH = MLA_HEADS * MLA_V
EPS = 1e-6
NEG_INF = -1e30
LOG2_E = 1.4426950408889634

LANES = 128
HEAD_SLAB = LANES
VMEM_LIMIT_BYTES = 56 * 1024 * 1024

ZC_QLAT = 0
ZC_KVLAT = ZC_QLAT + MLA_Q_RANK
ZC_KPE = ZC_KVLAT + MLA_KV_RANK
ZC_CAQ = ZC_KPE + LANES
ZC_CAK = ZC_CAQ + CA_WIDTH
ZC_CAV = ZC_CAK + CA_WIDTH
ZC_GA = ZC_CAV + CA_WIDTH
ZC_GB = ZC_GA + D_MODEL
Z_COLS = ZC_GB + D_MODEL

FFN_CHUNKS = ((0, 1024), (1024, 1024), (2048, 768))

BF16 = jnp.bfloat16
F32 = jnp.float32


def _const_spec(shape):
    nd = len(shape)
    return pl.BlockSpec(shape, lambda *_: (0,) * nd, pipeline_mode=pl.Buffered(1))


def _rmsnorm(x, g):
    return x * lax.rsqrt(jnp.mean(x * x, axis=-1, keepdims=True) + EPS) * g


def _dot(a, b):
    return jnp.dot(a, b, preferred_element_type=F32)


def _dot_nt(a, b):
    return lax.dot_general(a, b, (((1,), (1,)), ((), ())), preferred_element_type=F32)


def _ada_kernel(c_ref, w_ref, b_ref, o_ref):
    c = c_ref[...]
    c_act = c * jax.nn.sigmoid(c)
    o_ref[...] = jnp.dot(c_act, w_ref[...], preferred_element_type=F32,
                         precision=lax.Precision.HIGHEST) + b_ref[...]


def _ada(c, w_ada, b_ada):
    bsz, d = c.shape
    n = w_ada.shape[1]
    bn = 1152
    return pl.pallas_call(
        _ada_kernel,
        grid=(n // bn,),
        in_specs=[pl.BlockSpec((bsz, d), lambda j: (0, 0)),
                  pl.BlockSpec((d, bn), lambda j: (0, j)),
                  pl.BlockSpec((1, bn), lambda j: (0, j))],
        out_specs=pl.BlockSpec((bsz, bn), lambda j: (0, j)),
        out_shape=jax.ShapeDtypeStruct((bsz, n), F32),
        name="ada",
    )(c, w_ada, b_ada.reshape(1, n))


def _ffn_body(x, ada_ref, ada_base, norm_ref, w_in_ref, w_out_ref):
    shift = ada_ref[0, ada_base:ada_base + 1, :]
    scale = ada_ref[0, ada_base + 1:ada_base + 2, :]
    gate = ada_ref[0, ada_base + 2:ada_base + 3, :]
    h = (_rmsnorm(x, norm_ref[...]) * (1.0 + scale) + shift).astype(BF16)
    acc = None
    for c0, cw in FFN_CHUNKS:
        g = _dot(h, w_in_ref[:, c0:c0 + cw])
        u = _dot(h, w_in_ref[:, D_FF + c0:D_FF + c0 + cw])
        a = (g * jax.nn.sigmoid(g) * u).astype(BF16)
        part = _dot(a, w_out_ref[c0:c0 + cw, :])
        acc = part if acc is None else acc + part
    return x + (FFN_RES_WEIGHT * gate) * acc


def _ffn1_kernel(x_ref, ada_ref, norm_ref, w_in_ref, w_out_ref, o_ref):
    o_ref[0] = _ffn_body(x_ref[0], ada_ref, 0, norm_ref, w_in_ref, w_out_ref)


def _ffn1(x, ada, norm, w_in, w_out, tm):
    bsz, s, d = x.shape
    row = pl.BlockSpec((1, tm, d), lambda b, i: (b, i, 0))
    return pl.pallas_call(
        _ffn1_kernel,
        grid=(bsz, s // tm),
        in_specs=[row,
                  pl.BlockSpec((1, 9, d), lambda b, i: (b, 0, 0)),
                  _const_spec((1, d)),
                  _const_spec(w_in.shape),
                  _const_spec(w_out.shape)],
        out_specs=row,
        out_shape=jax.ShapeDtypeStruct(x.shape, F32),
        compiler_params=pltpu.CompilerParams(
            dimension_semantics=("parallel", "parallel"),
            vmem_limit_bytes=VMEM_LIMIT_BYTES),
        name="ffn1",
    )(x, ada, norm, w_in, w_out)


PROJ_SUBTILES = 2


def _rope(v, cos_t, sin_t):
    return v * cos_t + pltpu.roll(v, LANES - MLA_ROPE, axis=1) * sin_t


def _proj_kernel(x_ref, ada_ref, norm_ref, pos_ref, invf_ref, w_in_ref, qn_ref, kvn_ref,
                 wq_ref, wuk_ref, wuvt_ref, vone_ref,
                 q_ref, k_ref, vt_ref, caq_ref, cak_ref, cavt_ref, ga_ref, gb_ref):
    shift = ada_ref[0, 3:4, :]
    scale = ada_ref[0, 4:5, :]
    qk_scale = (MLA_NOPE + MLA_ROPE) ** -0.5 * LOG2_E
    rows = x_ref.shape[1] // PROJ_SUBTILES
    one_rows = jnp.ones((MLA_NOPE, rows), F32)
    zero_head = jnp.zeros((MLA_NOPE, rows), F32)
    zero_tail = jnp.zeros((LANES - MLA_NOPE - MLA_ROPE, rows), F32)
    ones_pad = (lax.broadcasted_iota(jnp.int32, (HEAD_SLAB - CA_HEAD_DIM, rows), 0) == 0).astype(F32)
    for sub in range(PROJ_SUBTILES):
        rs = slice(sub * rows, (sub + 1) * rows)
        h = (_rmsnorm(x_ref[0, rs, :], norm_ref[...]) * (1.0 + scale) + shift).astype(BF16)
        z = _dot_nt(h, w_in_ref[...])

        ang = invf_ref[...] * pos_ref[0, 0, :, rs].astype(F32)
        cos_h, sin_h = jnp.cos(ang), jnp.sin(ang)
        cos_t = jnp.transpose(jnp.concatenate([one_rows, cos_h, cos_h, zero_tail], axis=0))
        sin_t = jnp.transpose(jnp.concatenate([zero_head, sin_h, sin_h, zero_tail], axis=0))

        cq = _rmsnorm(z[:, ZC_QLAT:ZC_QLAT + MLA_Q_RANK], qn_ref[...]).astype(BF16)
        qall = _dot(cq, wq_ref[...])
        ckv = _rmsnorm(z[:, ZC_KVLAT:ZC_KVLAT + MLA_KV_RANK], kvn_ref[...]).astype(BF16)
        kn = _dot(ckv, wuk_ref[...])
        kpe = _rope(z[:, ZC_KPE:ZC_KPE + LANES], cos_t, sin_t)
        for hd in range(MLA_HEADS):
            sl = slice(hd * HEAD_SLAB, (hd + 1) * HEAD_SLAB)
            q_ref[0, rs, sl] = (_rope(qall[:, sl], cos_t, sin_t) * qk_scale).astype(BF16)
            k_ref[0, rs, sl] = (kn[:, sl] + kpe).astype(BF16)
        vt_ref[0, :, rs] = (_dot_nt(wuvt_ref[...], ckv) + vone_ref[...]).astype(BF16)

        caq_ref[0, rs, :] = (z[:, ZC_CAQ:ZC_CAQ + CA_WIDTH]
                             * (CA_HEAD_DIM ** -0.5 * LOG2_E)).astype(BF16)
        cak_ref[0, rs, :] = z[:, ZC_CAK:ZC_CAK + CA_WIDTH].astype(BF16)
        cavt = jnp.transpose(z[:, ZC_CAV:ZC_CAV + CA_WIDTH])
        for hd in range(CA_HEADS):
            slab = jnp.concatenate([cavt[hd * CA_HEAD_DIM:(hd + 1) * CA_HEAD_DIM], ones_pad], axis=0)
            cavt_ref[0, hd * HEAD_SLAB:(hd + 1) * HEAD_SLAB, rs] = slab.astype(BF16)
        ga_ref[0, rs, :] = jax.nn.sigmoid(z[:, ZC_GA:ZC_GA + D_MODEL]).astype(BF16)
        gb_ref[0, rs, :] = jax.nn.sigmoid(z[:, ZC_GB:ZC_GB + D_MODEL]).astype(BF16)


def _proj(x, ada, norm, pos, invf, w_in_ext, qn, kvn, wq, wuk, wuvt, vone, tm):
    bsz, s, d = x.shape

    def row(width):
        return pl.BlockSpec((1, tm, width), lambda b, i: (b, i, 0))

    def out(width):
        return jax.ShapeDtypeStruct((bsz, s, width), BF16)

    slabs = MLA_HEADS * HEAD_SLAB
    widths = (slabs, slabs, None, CA_WIDTH, CA_WIDTH, None, D_MODEL, D_MODEL)
    vt_spec = pl.BlockSpec((1, slabs, tm), lambda b, i: (b, 0, i))
    vt_shape = jax.ShapeDtypeStruct((bsz, slabs, s), BF16)
    return pl.pallas_call(
        _proj_kernel,
        grid=(bsz, s // tm),
        in_specs=[row(d),
                  pl.BlockSpec((1, 9, d), lambda b, i: (b, 0, 0)),
                  _const_spec((1, d)),
                  pl.BlockSpec((1, 1, 1, tm), lambda b, i: (b, i, 0, 0)),
                  _const_spec(invf.shape),
                  _const_spec(w_in_ext.shape),
                  _const_spec(qn.shape), _const_spec(kvn.shape),
                  _const_spec(wq.shape),
                  _const_spec(wuk.shape), _const_spec(wuvt.shape), _const_spec(vone.shape)],
        out_specs=[vt_spec if w is None else row(w) for w in widths],
        out_shape=[vt_shape if w is None else out(w) for w in widths],
        compiler_params=pltpu.CompilerParams(
            dimension_semantics=("parallel", "parallel"),
            vmem_limit_bytes=VMEM_LIMIT_BYTES),
        name="proj",
    )(x, ada, norm, pos, invf, w_in_ext, qn, kvn, wq, wuk, wuvt, vone)


MLA_HEADS_PER_STEP = 8
MLA_QSPLIT = 2
MLA_LOOKAHEAD = 3


def _mla_kernel(q_ref, k_ref, vt_ref, o_ref, m_ref, acc_ref, *s_refs, blk, hps):
    qi = pl.program_id(2)
    m_ref[...] = jnp.full(m_ref.shape, NEG_INF, F32)
    acc_ref[...] = jnp.zeros(acc_ref.shape, F32)

    nq = blk // MLA_QSPLIT
    units = [(hd, qh) for hd in range(hps) for qh in range(MLA_QSPLIT)]

    def step(j, mask):
        k0 = pl.multiple_of(j * blk, blk)

        def n_keys(qh):
            return blk if mask is None else (qh + 1) * nq

        def scores(u):
            hd, qh = units[u]
            sl = slice(hd * HEAD_SLAB, (hd + 1) * HEAD_SLAB)
            qs = slice(qh * nq, (qh + 1) * nq)
            nk = n_keys(qh)
            s = _dot_nt(k_ref[0, pl.ds(k0, nk), sl], q_ref[0, qs, sl])
            if mask is not None:
                s = jnp.where(mask[:nk, qs], s, NEG_INF)
            s_refs[u % len(s_refs)][:nk, :] = s

        for u in range(min(MLA_LOOKAHEAD, len(units))):
            scores(u)
        for u, (hd, qh) in enumerate(units):
            if u + MLA_LOOKAHEAD < len(units):
                scores(u + MLA_LOOKAHEAD)
            qs = slice(qh * nq, (qh + 1) * nq)
            nk = n_keys(qh)
            s = s_refs[u % len(s_refs)][:nk, :]
            m_prev = m_ref[hd, :, qs]
            m_new = jnp.maximum(m_prev, jnp.max(s, axis=0, keepdims=True))
            m_ref[hd, :, qs] = m_new
            alpha = jnp.exp2(m_prev - m_new)
            p = jnp.exp2((s - m_new).astype(BF16))
            vt = vt_ref[0, hd * HEAD_SLAB:(hd + 1) * HEAD_SLAB, pl.ds(k0, nk)]
            acc_ref[hd, :, qs] = acc_ref[hd, :, qs] * alpha + _dot(vt, p)

    def body(j, carry):
        step(j, None)
        return carry

    lax.fori_loop(0, qi, body, 0)
    kc = lax.broadcasted_iota(jnp.int32, (blk, blk), 0) // CHUNK
    qc = lax.broadcasted_iota(jnp.int32, (blk, blk), 1) // CHUNK
    step(qi, kc <= qc)
    for pr in range(hps // 2):
        outs = []
        for hd in (2 * pr, 2 * pr + 1):
            a = acc_ref[hd]
            outs.append(a[:MLA_V] * (1.0 / a[MLA_V:MLA_V + 1]))
        o_ref[0, :, pr * LANES:(pr + 1) * LANES] = jnp.transpose(
            jnp.concatenate(outs, axis=0)).astype(BF16)


def _mla(q, k, vt, blk):
    bsz, s, _ = q.shape
    hps = MLA_HEADS_PER_STEP
    return pl.pallas_call(
        functools.partial(_mla_kernel, blk=blk, hps=hps),
        grid=(bsz, MLA_HEADS // hps, s // blk),
        in_specs=[pl.BlockSpec((1, blk, hps * HEAD_SLAB), lambda b, g, i: (b, i, g)),
                  pl.BlockSpec((1, s, hps * HEAD_SLAB), lambda b, g, i: (b, 0, g)),
                  pl.BlockSpec((1, hps * HEAD_SLAB, s), lambda b, g, i: (b, g, 0))],
        out_specs=pl.BlockSpec((1, blk, hps * MLA_V), lambda b, g, i: (b, i, g)),
        out_shape=jax.ShapeDtypeStruct((bsz, s, MLA_OUT_WIDTH), BF16),
        scratch_shapes=[pltpu.VMEM((hps, 1, blk), F32),
                        pltpu.VMEM((hps, HEAD_SLAB, blk), F32)]
                       + [pltpu.VMEM((blk, blk // MLA_QSPLIT), F32)] * (MLA_LOOKAHEAD + 1),
        compiler_params=pltpu.CompilerParams(
            dimension_semantics=("parallel", "parallel", "arbitrary"),
            vmem_limit_bytes=VMEM_LIMIT_BYTES),
        name="mla",
    )(q, k, vt)


CA_TQ = 256
CA_TK = CA_TQ + CA_LEFT_CHUNKS * CHUNK
CA_KBLKS = CA_TK // CA_TQ
CA_TAB = 1024
CA_LOOKAHEAD = 3


def _ca_kernel(tab_ref, q_ref, k_ref, vt_ref, o_ref, bias_ref):
    b, i = pl.program_id(0), pl.program_id(1)

    @pl.when((b == 0) & (i == 0))
    def _build_bias():
        row = lax.broadcasted_iota(jnp.int32, (CA_TQ, CA_TAB), 0)
        kc = lax.broadcasted_iota(jnp.int32, (CA_TK, CA_TQ), 0) // CHUNK
        qc = lax.broadcasted_iota(jnp.int32, (CA_TK, CA_TQ), 1) // CHUNK
        band = (kc >= qc) & (kc <= qc + CA_LEFT_CHUNKS)
        for hd in range(CA_HEADS):
            t = jnp.broadcast_to(tab_ref[hd:hd + 1, :], (CA_TQ, CA_TAB))
            for bit in range(CA_TQ.bit_length() - 1):
                t = jnp.where(((row >> bit) & 1) == 1, pltpu.roll(t, 1 << bit, axis=1), t)
            bias_ref[hd] = jnp.where(band, jnp.transpose(t[:, :CA_TK]) * LOG2_E, NEG_INF)

    lo = lax.broadcasted_iota(jnp.int32, (1, LANES), 1) < CA_HEAD_DIM

    def attend(nkb):
        nk = nkb * CA_TQ
        k0 = pl.multiple_of((i - (nkb - 1)) * CA_TQ, CA_TQ)

        def scores(hd):
            sl = slice((hd // 2) * LANES, (hd // 2 + 1) * LANES)
            q = q_ref[0, :, sl]
            zero = jnp.zeros_like(q)
            qh = jnp.where(lo, q, zero) if hd % 2 == 0 else jnp.where(lo, zero, q)
            return _dot_nt(k_ref[0, pl.ds(k0, nk), sl], qh) + bias_ref[hd, CA_TK - nk:, :]

        pending = {hd: scores(hd) for hd in range(CA_LOOKAHEAD)}
        outs = []
        for hd in range(CA_HEADS):
            s = pending.pop(hd)
            if hd + CA_LOOKAHEAD < CA_HEADS:
                pending[hd + CA_LOOKAHEAD] = scores(hd + CA_LOOKAHEAD)
            m = jnp.max(s, axis=0, keepdims=True)
            p = jnp.exp2((s - m).astype(BF16))
            vt = vt_ref[0, hd * HEAD_SLAB:(hd + 1) * HEAD_SLAB, pl.ds(k0, nk)]
            a = _dot(vt, p)
            outs.append(a[:CA_HEAD_DIM] * (1.0 / a[CA_HEAD_DIM:CA_HEAD_DIM + 1]))
            if hd % 2 == 1:
                pr = hd // 2
                o_ref[0, :, pr * LANES:(pr + 1) * LANES] = jnp.transpose(
                    jnp.concatenate(outs, axis=0)).astype(BF16)
                outs = []

    for nkb in range(1, CA_KBLKS):
        pl.when(i == nkb - 1)(functools.partial(attend, nkb))
    pl.when(i >= CA_KBLKS - 1)(functools.partial(attend, CA_KBLKS))


def _ca(tab, q, k, vt):
    bsz, s, _ = q.shape
    blk = pl.BlockSpec((1, CA_TQ, CA_WIDTH), lambda b, i: (b, i, 0))
    return pl.pallas_call(
        _ca_kernel,
        grid=(bsz, s // CA_TQ),
        in_specs=[_const_spec(tab.shape), blk,
                  pl.BlockSpec((1, s, CA_WIDTH), lambda b, i: (b, 0, 0)),
                  pl.BlockSpec((1, CA_HEADS * HEAD_SLAB, s), lambda b, i: (b, 0, 0))],
        out_specs=blk,
        out_shape=jax.ShapeDtypeStruct((bsz, s, CA_WIDTH), BF16),
        scratch_shapes=[pltpu.VMEM((CA_HEADS, CA_TK, CA_TQ), F32)],
        compiler_params=pltpu.CompilerParams(
            dimension_semantics=("arbitrary", "arbitrary"),
            vmem_limit_bytes=VMEM_LIMIT_BYTES),
        name="ca",
    )(tab, q, k, vt)


def _tail_kernel(x_ref, ada_ref, aa_ref, ab_ref, ga_ref, gb_ref, wa_ref, wb_ref, wo_ref,
                 norm_ref, w_in_ref, w_out_ref, fnorm_ref, o_ref):
    ya = _dot(aa_ref[0], wa_ref[...])
    yb = _dot(ab_ref[0], wb_ref[...])
    merged = (ga_ref[0].astype(F32) * ya + gb_ref[0].astype(F32) * yb).astype(BF16)
    x = x_ref[0] + ada_ref[0, 5:6, :] * _dot(merged, wo_ref[...])
    x = _ffn_body(x, ada_ref, 6, norm_ref, w_in_ref, w_out_ref)
    o_ref[0] = _rmsnorm(x, fnorm_ref[...])


def _tail(x, ada, aa, ab, ga, gb, wa, wb, wo, norm, w_in, w_out, fnorm, tm):
    bsz, s, d = x.shape

    def row(width):
        return pl.BlockSpec((1, tm, width), lambda b, i: (b, i, 0))

    return pl.pallas_call(
        _tail_kernel,
        grid=(bsz, s // tm),
        in_specs=[row(d),
                  pl.BlockSpec((1, 9, d), lambda b, i: (b, 0, 0)),
                  row(MLA_OUT_WIDTH), row(CA_WIDTH), row(d), row(d),
                  _const_spec(wa.shape), _const_spec(wb.shape), _const_spec(wo.shape),
                  _const_spec((1, d)), _const_spec(w_in.shape), _const_spec(w_out.shape),
                  _const_spec((1, d))],
        out_specs=row(d),
        out_shape=jax.ShapeDtypeStruct(x.shape, F32),
        compiler_params=pltpu.CompilerParams(
            dimension_semantics=("parallel", "parallel"),
            vmem_limit_bytes=VMEM_LIMIT_BYTES),
        name="tail",
    )(x, ada, aa, ab, ga, gb, wa, wb, wo, norm, w_in, w_out, fnorm)


def _rotate_half_cols(w):
    half = w.shape[-1] // 2
    return jnp.concatenate([-w[..., half:], w[..., :half]], axis=-1)


def _prep_w_in(w_in):
    d = w_in.shape[0]
    o = 0
    pieces = {}
    for name, width in (("qlat", MLA_Q_RANK), ("kvlat", MLA_KV_RANK), ("kpe", MLA_ROPE),
                        ("caq", CA_WIDTH), ("cak", CA_WIDTH), ("cav", CA_WIDTH),
                        ("ga", D_MODEL), ("gb", D_MODEL)):
        pieces[name] = w_in[:, o:o + width]
        o += width
    zl = jnp.zeros((d, MLA_NOPE), w_in.dtype)
    kpe = jnp.concatenate([zl, pieces["kpe"], _rotate_half_cols(pieces["kpe"])], axis=1)
    ext = jnp.concatenate([pieces["qlat"].T, pieces["kvlat"].T, kpe.T, pieces["caq"].T,
                           pieces["cak"].T, pieces["cav"].T, pieces["ga"].T, pieces["gb"].T], axis=0)
    assert ext.shape[0] == Z_COLS
    return ext.astype(BF16)


def _prep_w_uq(w_uq):
    r = w_uq.shape[0]
    w = w_uq.reshape(r, MLA_HEADS, MLA_NOPE + MLA_ROPE)
    nope, pe = w[..., :MLA_NOPE], w[..., MLA_NOPE:]
    slab = jnp.concatenate([nope, pe, _rotate_half_cols(pe)], axis=-1)
    return slab.reshape(r, MLA_HEADS * HEAD_SLAB).astype(BF16)


def _prep_w_ukv(w_ukv):
    r = w_ukv.shape[0]
    w = w_ukv.reshape(r, MLA_HEADS, MLA_NOPE + MLA_V)
    k_nope, v = w[..., :MLA_NOPE], w[..., MLA_NOPE:]
    zk = jnp.zeros((r, MLA_HEADS, HEAD_SLAB - MLA_NOPE), w.dtype)
    wuk = jnp.concatenate([k_nope, zk], axis=-1).reshape(r, MLA_HEADS * HEAD_SLAB)
    zv = jnp.zeros((r, MLA_HEADS, HEAD_SLAB - MLA_V), w.dtype)
    wuvt = jnp.concatenate([v, zv], axis=-1).reshape(r, MLA_HEADS * HEAD_SLAB).T
    vone = (jnp.arange(MLA_HEADS * HEAD_SLAB) % HEAD_SLAB == MLA_V).astype(F32).reshape(-1, 1)
    return wuk.astype(BF16), wuvt.astype(BF16), vone


def _prep_bias_table(rel_bias):
    j = jnp.arange(CA_TAB)
    t = (j + CA_TQ - 1) % CA_TAB
    rel = (CA_TK - 1) - t
    idx = jnp.clip(rel, -MAX_REL_DIST, MAX_REL_DIST) + MAX_REL_DIST
    return rel_bias[idx].T.astype(F32)


def kernel(x, c, positions, w_ada, b_ada, ffn1_norm, ffn1_w_in, ffn1_w_out, mix_norm, w_in,
           mla_q_norm, mla_w_uq, mla_kv_norm, mla_w_ukv, rel_bias, w_branch_a, w_branch_b,
           w_out, ffn2_norm, ffn2_w_in, ffn2_w_out, final_norm):
    bsz, s, d = x.shape
    tm = 512
    for l in range(w_ada.shape[0]):
        ada = _ada(c, w_ada[l], b_ada[l]).reshape(bsz, 9, d)
        x = _ffn1(x, ada, ffn1_norm[l].reshape(1, d), ffn1_w_in[l].astype(BF16),
                  ffn1_w_out[l].astype(BF16), tm)

        freq = jnp.arange(0, MLA_ROPE, 2, dtype=F32) / MLA_ROPE
        inv_freq = ROPE_THETA ** (-freq)
        invf = inv_freq.reshape(MLA_ROPE // 2, 1)
        wq = _prep_w_uq(mla_w_uq[l])
        wuk, wuvt, vone = _prep_w_ukv(mla_w_ukv[l])
        q, k, vt, caq, cak, cavt, ga, gb = _proj(
            x, ada, mix_norm[l].reshape(1, d), positions.reshape(bsz, s // tm, 1, tm), invf,
            _prep_w_in(w_in[l]), mla_q_norm[l].reshape(1, -1), mla_kv_norm[l].reshape(1, -1),
            wq, wuk, wuvt, vone, tm)
        attn_a = _mla(q, k, vt, 512)
        attn_b = _ca(_prep_bias_table(rel_bias[l]), caq, cak, cavt)
        last = l == w_ada.shape[0] - 1
        assert last, "the fused tail applies the final norm; DEPTH is 1"
        x = _tail(x, ada, attn_a, attn_b, ga, gb, w_branch_a[l].astype(BF16),
                  w_branch_b[l].astype(BF16), w_out[l].astype(BF16),
                  ffn2_norm[l].reshape(1, d), ffn2_w_in[l].astype(BF16),
                  ffn2_w_out[l].astype(BF16), final_norm.reshape(1, d), tm)
    return x
```

```python
import functools

import jax
import jax.numpy as jnp
from jax import lax
from jax.experimental import pallas as pl
from jax.experimental.pallas import tpu as pltpu

D_MODEL = 1024
CHUNK = 64
D_FF = 2816
FFN_RES_WEIGHT = 0.5
MLA_HEADS = 8
MLA_Q_RANK = 256
MLA_KV_RANK = 128
MLA_NOPE = 64
MLA_ROPE = 32
MLA_V = 64
ROPE_THETA = 10000.0
CA_HEADS = 8
CA_HEAD_DIM = 64
CA_LEFT_CHUNKS = 8
MAX_REL_DIST = 256
CA_WIDTH = CA_HEADS * CA_HEAD_DIM
MLA_OUT_WIDTH = MLA_HEADS * MLA_V
EPS = 1e-6
NEG_INF = -1e30
LOG2_E = 1.4426950408889634

LANES = 128
HEAD_SLAB = LANES
VMEM_LIMIT_BYTES = 56 * 1024 * 1024

ZC_QLAT = 0
ZC_KVLAT = ZC_QLAT + MLA_Q_RANK
ZC_KPE = ZC_KVLAT + MLA_KV_RANK
ZC_CAQ = ZC_KPE + LANES
ZC_CAK = ZC_CAQ + CA_WIDTH
ZC_CAV = ZC_CAK + CA_WIDTH
ZC_GA = ZC_CAV + CA_WIDTH
ZC_GB = ZC_GA + D_MODEL
Z_COLS = ZC_GB + D_MODEL

FFN_CHUNKS = ((0, 1024), (1024, 1024), (2048, 768))

BF16 = jnp.bfloat16
F32 = jnp.float32


def _const_spec(shape):
    nd = len(shape)
    return pl.BlockSpec(shape, lambda *_: (0,) * nd, pipeline_mode=pl.Buffered(1))


def _rmsnorm(x, g):
    return x * lax.rsqrt(jnp.mean(x * x, axis=-1, keepdims=True) + EPS) * g


def _mod_norm(x, g, scale, shift):
    y = (x * lax.rsqrt(jnp.mean(x * x, axis=-1, keepdims=True) + EPS)).astype(BF16)
    return y * (g * (1.0 + scale)).astype(BF16) + shift.astype(BF16)


def _dot(a, b):
    return jnp.dot(a, b, preferred_element_type=F32)


def _dot_nt(a, b):
    return lax.dot_general(a, b, (((1,), (1,)), ((), ())), preferred_element_type=F32)


def _ada_kernel(c_ref, w_ref, b_ref, o_ref):
    c = c_ref[...]
    c_act = c * jax.nn.sigmoid(c)
    o_ref[...] = jnp.dot(c_act, w_ref[...], preferred_element_type=F32,
                         precision=lax.Precision.HIGHEST) + b_ref[...]


def _ada(c, w_ada, b_ada):
    bsz, d = c.shape
    n = w_ada.shape[1]
    bn = 1152
    return pl.pallas_call(
        _ada_kernel,
        grid=(n // bn,),
        in_specs=[pl.BlockSpec((bsz, d), lambda j: (0, 0)),
                  pl.BlockSpec((d, bn), lambda j: (0, j)),
                  pl.BlockSpec((1, bn), lambda j: (0, j))],
        out_specs=pl.BlockSpec((bsz, bn), lambda j: (0, j)),
        out_shape=jax.ShapeDtypeStruct((bsz, n), F32),
        name="ada",
    )(c, w_ada, b_ada.reshape(1, n))


def _ffn_body(x, ada_ref, ada_base, norm_ref, w_in_ref, w_out_ref):
    shift = ada_ref[0, ada_base:ada_base + 1, :]
    scale = ada_ref[0, ada_base + 1:ada_base + 2, :]
    gate = ada_ref[0, ada_base + 2:ada_base + 3, :]
    h = _mod_norm(x, norm_ref[...], scale, shift)
    acc = None
    for c0, cw in FFN_CHUNKS:
        g = _dot(h, w_in_ref[:, c0:c0 + cw])
        u = _dot(h, w_in_ref[:, D_FF + c0:D_FF + c0 + cw])
        a = (g * jax.nn.sigmoid(g) * u).astype(BF16)
        part = _dot(a, w_out_ref[c0:c0 + cw, :])
        acc = part if acc is None else acc + part
    return x + (FFN_RES_WEIGHT * gate) * acc


def _ffn1_kernel(x_ref, ada_ref, norm_ref, w_in_ref, w_out_ref, o_ref):
    o_ref[0] = _ffn_body(x_ref[0], ada_ref, 0, norm_ref, w_in_ref, w_out_ref)


def _ffn1(x, ada, norm, w_in, w_out, tm):
    bsz, s, d = x.shape
    row = pl.BlockSpec((1, tm, d), lambda b, i: (b, i, 0))
    return pl.pallas_call(
        _ffn1_kernel,
        grid=(bsz, s // tm),
        in_specs=[row,
                  pl.BlockSpec((1, 9, d), lambda b, i: (b, 0, 0)),
                  _const_spec((1, d)),
                  _const_spec(w_in.shape),
                  _const_spec(w_out.shape)],
        out_specs=row,
        out_shape=jax.ShapeDtypeStruct(x.shape, F32),
        compiler_params=pltpu.CompilerParams(
            dimension_semantics=("parallel", "parallel"),
            vmem_limit_bytes=VMEM_LIMIT_BYTES),
        name="ffn1",
    )(x, ada, norm, w_in, w_out)


PROJ_SUBTILES = 2


def _rope(v, cos_t, sin_t):
    return v * cos_t + pltpu.roll(v, LANES - MLA_ROPE, axis=1) * sin_t


def _proj_kernel(x_ref, ada_ref, norm_ref, pos_ref, invf_ref, w_in_ref, qn_ref, kvn_ref,
                 wq_ref, wuk_ref, wuvt_ref, vone_ref,
                 q_ref, k_ref, vt_ref, caq_ref, cak_ref, cavt_ref, ga_ref, gb_ref):
    shift = ada_ref[0, 3:4, :]
    scale = ada_ref[0, 4:5, :]
    qk_scale = (MLA_NOPE + MLA_ROPE) ** -0.5 * LOG2_E
    rows = x_ref.shape[1] // PROJ_SUBTILES
    one_rows = jnp.ones((MLA_NOPE, rows), F32)
    zero_head = jnp.zeros((MLA_NOPE, rows), F32)
    zero_tail = jnp.zeros((LANES - MLA_NOPE - MLA_ROPE, rows), F32)
    ones_pad = (lax.broadcasted_iota(jnp.int32, (HEAD_SLAB - CA_HEAD_DIM, rows), 0) == 0).astype(F32)
    for sub in range(PROJ_SUBTILES):
        rs = slice(sub * rows, (sub + 1) * rows)
        h = _mod_norm(x_ref[0, rs, :], norm_ref[...], scale, shift)
        z = _dot_nt(h, w_in_ref[...])

        ang = invf_ref[...] * pos_ref[0, 0, :, rs].astype(F32)
        cos_h, sin_h = jnp.cos(ang), jnp.sin(ang)
        cos_t = jnp.transpose(jnp.concatenate([one_rows, cos_h, cos_h, zero_tail], axis=0))
        sin_t = jnp.transpose(jnp.concatenate([zero_head, sin_h, sin_h, zero_tail], axis=0))

        cq = _rmsnorm(z[:, ZC_QLAT:ZC_QLAT + MLA_Q_RANK], qn_ref[...]).astype(BF16)
        qall = _dot(cq, wq_ref[...])
        ckv = _rmsnorm(z[:, ZC_KVLAT:ZC_KVLAT + MLA_KV_RANK], kvn_ref[...]).astype(BF16)
        kn = _dot(ckv, wuk_ref[...])
        kpe = _rope(z[:, ZC_KPE:ZC_KPE + LANES], cos_t, sin_t)
        for hd in range(MLA_HEADS):
            sl = slice(hd * HEAD_SLAB, (hd + 1) * HEAD_SLAB)
            q_ref[0, rs, sl] = (_rope(qall[:, sl], cos_t, sin_t) * qk_scale).astype(BF16)
            k_ref[0, rs, sl] = (kn[:, sl] + kpe).astype(BF16)
        vt_ref[0, :, rs] = (_dot_nt(wuvt_ref[...], ckv) + vone_ref[...]).astype(BF16)

        caq_ref[0, rs, :] = (z[:, ZC_CAQ:ZC_CAQ + CA_WIDTH]
                             * (CA_HEAD_DIM ** -0.5 * LOG2_E)).astype(BF16)
        cak_ref[0, rs, :] = z[:, ZC_CAK:ZC_CAK + CA_WIDTH].astype(BF16)
        cavt = jnp.transpose(z[:, ZC_CAV:ZC_CAV + CA_WIDTH])
        for hd in range(CA_HEADS):
            slab = jnp.concatenate([cavt[hd * CA_HEAD_DIM:(hd + 1) * CA_HEAD_DIM], ones_pad], axis=0)
            cavt_ref[0, hd * HEAD_SLAB:(hd + 1) * HEAD_SLAB, rs] = slab.astype(BF16)
        ga_ref[0, rs, :] = jax.nn.sigmoid(z[:, ZC_GA:ZC_GA + D_MODEL]).astype(BF16)
        gb_ref[0, rs, :] = jax.nn.sigmoid(z[:, ZC_GB:ZC_GB + D_MODEL]).astype(BF16)


def _proj(x, ada, norm, pos, invf, w_in_ext, qn, kvn, wq, wuk, wuvt, vone, tm):
    bsz, s, d = x.shape

    def row(width):
        return pl.BlockSpec((1, tm, width), lambda b, i: (b, i, 0))

    def out(width):
        return jax.ShapeDtypeStruct((bsz, s, width), BF16)

    slabs = MLA_HEADS * HEAD_SLAB
    widths = (slabs, slabs, None, CA_WIDTH, CA_WIDTH, None, D_MODEL, D_MODEL)
    vt_spec = pl.BlockSpec((1, slabs, tm), lambda b, i: (b, 0, i))
    vt_shape = jax.ShapeDtypeStruct((bsz, slabs, s), BF16)
    return pl.pallas_call(
        _proj_kernel,
        grid=(bsz, s // tm),
        in_specs=[row(d),
                  pl.BlockSpec((1, 9, d), lambda b, i: (b, 0, 0)),
                  _const_spec((1, d)),
                  pl.BlockSpec((1, 1, 1, tm), lambda b, i: (b, i, 0, 0)),
                  _const_spec(invf.shape),
                  _const_spec(w_in_ext.shape),
                  _const_spec(qn.shape), _const_spec(kvn.shape),
                  _const_spec(wq.shape),
                  _const_spec(wuk.shape), _const_spec(wuvt.shape), _const_spec(vone.shape)],
        out_specs=[vt_spec if w is None else row(w) for w in widths],
        out_shape=[vt_shape if w is None else out(w) for w in widths],
        compiler_params=pltpu.CompilerParams(
            dimension_semantics=("parallel", "parallel"),
            vmem_limit_bytes=VMEM_LIMIT_BYTES),
        name="proj",
    )(x, ada, norm, pos, invf, w_in_ext, qn, kvn, wq, wuk, wuvt, vone)


MLA_HEADS_PER_STEP = 8
MLA_QSPLIT = 2
MLA_LOOKAHEAD = 3


def _mla_kernel(q_ref, k_ref, vt_ref, o_ref, m_ref, acc_ref, *s_refs, blk, hps):
    qi = pl.program_id(2)
    m_ref[...] = jnp.full(m_ref.shape, NEG_INF, F32)
    acc_ref[...] = jnp.zeros(acc_ref.shape, F32)

    nq = blk // MLA_QSPLIT
    units = [(hd, qh) for hd in range(hps) for qh in range(MLA_QSPLIT)]

    def step(j, mask):
        k0 = pl.multiple_of(j * blk, blk)

        def n_keys(qh):
            return blk if mask is None else (qh + 1) * nq

        def scores(u):
            hd, qh = units[u]
            sl = slice(hd * HEAD_SLAB, (hd + 1) * HEAD_SLAB)
            qs = slice(qh * nq, (qh + 1) * nq)
            nk = n_keys(qh)
            s = _dot_nt(k_ref[0, pl.ds(k0, nk), sl], q_ref[0, qs, sl])
            if mask is not None:
                s = jnp.where(mask[:nk, qs], s, NEG_INF)
            s_refs[u % len(s_refs)][:nk, :] = s

        for u in range(min(MLA_LOOKAHEAD, len(units))):
            scores(u)
        for u, (hd, qh) in enumerate(units):
            if u + MLA_LOOKAHEAD < len(units):
                scores(u + MLA_LOOKAHEAD)
            qs = slice(qh * nq, (qh + 1) * nq)
            nk = n_keys(qh)
            s = s_refs[u % len(s_refs)][:nk, :]
            m_prev = m_ref[hd, :, qs]
            m_new = jnp.maximum(m_prev, jnp.max(s, axis=0, keepdims=True))
            m_ref[hd, :, qs] = m_new
            alpha = jnp.exp2(m_prev - m_new)
            p = jnp.exp2((s - m_new).astype(BF16))
            vt = vt_ref[0, hd * HEAD_SLAB:(hd + 1) * HEAD_SLAB, pl.ds(k0, nk)]
            acc_ref[hd, :, qs] = acc_ref[hd, :, qs] * alpha + _dot(vt, p)

    def body(j, carry):
        step(j, None)
        return carry

    lax.fori_loop(0, qi, body, 0)
    kc = lax.broadcasted_iota(jnp.int32, (blk, blk), 0) // CHUNK
    qc = lax.broadcasted_iota(jnp.int32, (blk, blk), 1) // CHUNK
    step(qi, kc <= qc)
    for pr in range(hps // 2):
        outs = []
        for hd in (2 * pr, 2 * pr + 1):
            a = acc_ref[hd]
            outs.append(a[:MLA_V] * (1.0 / a[MLA_V:MLA_V + 1]))
        o_ref[0, :, pr * LANES:(pr + 1) * LANES] = jnp.transpose(
            jnp.concatenate(outs, axis=0)).astype(BF16)


def _mla(q, k, vt, blk):
    bsz, s, _ = q.shape
    hps = MLA_HEADS_PER_STEP
    return pl.pallas_call(
        functools.partial(_mla_kernel, blk=blk, hps=hps),
        grid=(bsz, MLA_HEADS // hps, s // blk),
        in_specs=[pl.BlockSpec((1, blk, hps * HEAD_SLAB), lambda b, g, i: (b, i, g)),
                  pl.BlockSpec((1, s, hps * HEAD_SLAB), lambda b, g, i: (b, 0, g)),
                  pl.BlockSpec((1, hps * HEAD_SLAB, s), lambda b, g, i: (b, g, 0))],
        out_specs=pl.BlockSpec((1, blk, hps * MLA_V), lambda b, g, i: (b, i, g)),
        out_shape=jax.ShapeDtypeStruct((bsz, s, MLA_OUT_WIDTH), BF16),
        scratch_shapes=[pltpu.VMEM((hps, 1, blk), F32),
                        pltpu.VMEM((hps, HEAD_SLAB, blk), F32)]
                       + [pltpu.VMEM((blk, blk // MLA_QSPLIT), F32)] * (MLA_LOOKAHEAD + 1),
        compiler_params=pltpu.CompilerParams(
            dimension_semantics=("parallel", "parallel", "arbitrary"),
            vmem_limit_bytes=VMEM_LIMIT_BYTES),
        name="mla",
    )(q, k, vt)


CA_TQ = 256
CA_TK = CA_TQ + CA_LEFT_CHUNKS * CHUNK
CA_KBLKS = CA_TK // CA_TQ
CA_TAB = 1024
CA_LOOKAHEAD = 3


def _ca_kernel(tab_ref, q_ref, k_ref, vt_ref, o_ref, bias_ref):
    b, i = pl.program_id(0), pl.program_id(1)

    @pl.when((b == 0) & (i == 0))
    def _build_bias():
        row = lax.broadcasted_iota(jnp.int32, (CA_TQ, CA_TAB), 0)
        kc = lax.broadcasted_iota(jnp.int32, (CA_TK, CA_TQ), 0) // CHUNK
        qc = lax.broadcasted_iota(jnp.int32, (CA_TK, CA_TQ), 1) // CHUNK
        band = (kc >= qc) & (kc <= qc + CA_LEFT_CHUNKS)
        for hd in range(CA_HEADS):
            t = jnp.broadcast_to(tab_ref[hd:hd + 1, :], (CA_TQ, CA_TAB))
            for bit in range(CA_TQ.bit_length() - 1):
                t = jnp.where(((row >> bit) & 1) == 1, pltpu.roll(t, 1 << bit, axis=1), t)
            bias_ref[hd] = jnp.where(band, jnp.transpose(t[:, :CA_TK]) * LOG2_E, NEG_INF)

    lo = lax.broadcasted_iota(jnp.int32, (1, LANES), 1) < CA_HEAD_DIM

    def attend(nkb):
        nk = nkb * CA_TQ
        k0 = pl.multiple_of((i - (nkb - 1)) * CA_TQ, CA_TQ)

        def scores(hd):
            sl = slice((hd // 2) * LANES, (hd // 2 + 1) * LANES)
            q = q_ref[0, :, sl]
            zero = jnp.zeros_like(q)
            qh = jnp.where(lo, q, zero) if hd % 2 == 0 else jnp.where(lo, zero, q)
            return _dot_nt(k_ref[0, pl.ds(k0, nk), sl], qh) + bias_ref[hd, CA_TK - nk:, :]

        pending = {hd: scores(hd) for hd in range(CA_LOOKAHEAD)}
        outs = []
        for hd in range(CA_HEADS):
            s = pending.pop(hd)
            if hd + CA_LOOKAHEAD < CA_HEADS:
                pending[hd + CA_LOOKAHEAD] = scores(hd + CA_LOOKAHEAD)
            m = jnp.max(s, axis=0, keepdims=True)
            p = jnp.exp2((s - m).astype(BF16))
            vt = vt_ref[0, hd * HEAD_SLAB:(hd + 1) * HEAD_SLAB, pl.ds(k0, nk)]
            a = _dot(vt, p)
            outs.append(a[:CA_HEAD_DIM] * (1.0 / a[CA_HEAD_DIM:CA_HEAD_DIM + 1]))
            if hd % 2 == 1:
                pr = hd // 2
                o_ref[0, :, pr * LANES:(pr + 1) * LANES] = jnp.transpose(
                    jnp.concatenate(outs, axis=0)).astype(BF16)
                outs = []

    for nkb in range(1, CA_KBLKS):
        pl.when(i == nkb - 1)(functools.partial(attend, nkb))
    pl.when(i >= CA_KBLKS - 1)(functools.partial(attend, CA_KBLKS))


def _ca(tab, q, k, vt):
    bsz, s, _ = q.shape
    blk = pl.BlockSpec((1, CA_TQ, CA_WIDTH), lambda b, i: (b, i, 0))
    return pl.pallas_call(
        _ca_kernel,
        grid=(bsz, s // CA_TQ),
        in_specs=[_const_spec(tab.shape), blk,
                  pl.BlockSpec((1, s, CA_WIDTH), lambda b, i: (b, 0, 0)),
                  pl.BlockSpec((1, CA_HEADS * HEAD_SLAB, s), lambda b, i: (b, 0, 0))],
        out_specs=blk,
        out_shape=jax.ShapeDtypeStruct((bsz, s, CA_WIDTH), BF16),
        scratch_shapes=[pltpu.VMEM((CA_HEADS, CA_TK, CA_TQ), F32)],
        compiler_params=pltpu.CompilerParams(
            dimension_semantics=("arbitrary", "arbitrary"),
            vmem_limit_bytes=VMEM_LIMIT_BYTES),
        name="ca",
    )(tab, q, k, vt)


def _tail_kernel(x_ref, ada_ref, aa_ref, ab_ref, ga_ref, gb_ref, wa_ref, wb_ref, wo_ref,
                 norm_ref, w_in_ref, w_out_ref, fnorm_ref, o_ref):
    ya = _dot(aa_ref[0], wa_ref[...])
    yb = _dot(ab_ref[0], wb_ref[...])
    merged = (ga_ref[0].astype(F32) * ya + gb_ref[0].astype(F32) * yb).astype(BF16)
    x = x_ref[0] + ada_ref[0, 5:6, :] * _dot(merged, wo_ref[...])
    x = _ffn_body(x, ada_ref, 6, norm_ref, w_in_ref, w_out_ref)
    o_ref[0] = _rmsnorm(x, fnorm_ref[...])


def _tail(x, ada, aa, ab, ga, gb, wa, wb, wo, norm, w_in, w_out, fnorm, tm):
    bsz, s, d = x.shape

    def row(width):
        return pl.BlockSpec((1, tm, width), lambda b, i: (b, i, 0))

    return pl.pallas_call(
        _tail_kernel,
        grid=(bsz, s // tm),
        in_specs=[row(d),
                  pl.BlockSpec((1, 9, d), lambda b, i: (b, 0, 0)),
                  row(MLA_OUT_WIDTH), row(CA_WIDTH), row(d), row(d),
                  _const_spec(wa.shape), _const_spec(wb.shape), _const_spec(wo.shape),
                  _const_spec((1, d)), _const_spec(w_in.shape), _const_spec(w_out.shape),
                  _const_spec((1, d))],
        out_specs=row(d),
        out_shape=jax.ShapeDtypeStruct(x.shape, F32),
        compiler_params=pltpu.CompilerParams(
            dimension_semantics=("parallel", "parallel"),
            vmem_limit_bytes=VMEM_LIMIT_BYTES),
        name="tail",
    )(x, ada, aa, ab, ga, gb, wa, wb, wo, norm, w_in, w_out, fnorm)


def _rotate_half_cols(w):
    half = w.shape[-1] // 2
    return jnp.concatenate([-w[..., half:], w[..., :half]], axis=-1)


def _prep_w_in(w_in):
    d = w_in.shape[0]
    o = 0
    pieces = {}
    for name, width in (("qlat", MLA_Q_RANK), ("kvlat", MLA_KV_RANK), ("kpe", MLA_ROPE),
                        ("caq", CA_WIDTH), ("cak", CA_WIDTH), ("cav", CA_WIDTH),
                        ("ga", D_MODEL), ("gb", D_MODEL)):
        pieces[name] = w_in[:, o:o + width]
        o += width
    zl = jnp.zeros((d, MLA_NOPE), w_in.dtype)
    kpe = jnp.concatenate([zl, pieces["kpe"], _rotate_half_cols(pieces["kpe"])], axis=1)
    ext = jnp.concatenate([pieces["qlat"].T, pieces["kvlat"].T, kpe.T, pieces["caq"].T,
                           pieces["cak"].T, pieces["cav"].T, pieces["ga"].T, pieces["gb"].T], axis=0)
    assert ext.shape[0] == Z_COLS
    return ext.astype(BF16)


def _prep_w_uq(w_uq):
    r = w_uq.shape[0]
    w = w_uq.reshape(r, MLA_HEADS, MLA_NOPE + MLA_ROPE)
    nope, pe = w[..., :MLA_NOPE], w[..., MLA_NOPE:]
    slab = jnp.concatenate([nope, pe, _rotate_half_cols(pe)], axis=-1)
    return slab.reshape(r, MLA_HEADS * HEAD_SLAB).astype(BF16)


def _prep_w_ukv(w_ukv):
    r = w_ukv.shape[0]
    w = w_ukv.reshape(r, MLA_HEADS, MLA_NOPE + MLA_V)
    k_nope, v = w[..., :MLA_NOPE], w[..., MLA_NOPE:]
    zk = jnp.zeros((r, MLA_HEADS, HEAD_SLAB - MLA_NOPE), w.dtype)
    wuk = jnp.concatenate([k_nope, zk], axis=-1).reshape(r, MLA_HEADS * HEAD_SLAB)
    zv = jnp.zeros((r, MLA_HEADS, HEAD_SLAB - MLA_V), w.dtype)
    wuvt = jnp.concatenate([v, zv], axis=-1).reshape(r, MLA_HEADS * HEAD_SLAB).T
    vone = (jnp.arange(MLA_HEADS * HEAD_SLAB) % HEAD_SLAB == MLA_V).astype(F32).reshape(-1, 1)
    return wuk.astype(BF16), wuvt.astype(BF16), vone


def _prep_bias_table(rel_bias):
    j = jnp.arange(CA_TAB)
    t = (j + CA_TQ - 1) % CA_TAB
    rel = (CA_TK - 1) - t
    idx = jnp.clip(rel, -MAX_REL_DIST, MAX_REL_DIST) + MAX_REL_DIST
    return rel_bias[idx].T.astype(F32)


def kernel(x, c, positions, w_ada, b_ada, ffn1_norm, ffn1_w_in, ffn1_w_out, mix_norm, w_in,
           mla_q_norm, mla_w_uq, mla_kv_norm, mla_w_ukv, rel_bias, w_branch_a, w_branch_b,
           w_out, ffn2_norm, ffn2_w_in, ffn2_w_out, final_norm):
    bsz, s, d = x.shape
    tm = 512
    for l in range(w_ada.shape[0]):
        ada = _ada(c, w_ada[l], b_ada[l]).reshape(bsz, 9, d)
        x = _ffn1(x, ada, ffn1_norm[l].reshape(1, d), ffn1_w_in[l].astype(BF16),
                  ffn1_w_out[l].astype(BF16), tm)

        freq = jnp.arange(0, MLA_ROPE, 2, dtype=F32) / MLA_ROPE
        inv_freq = ROPE_THETA ** (-freq)
        invf = inv_freq.reshape(MLA_ROPE // 2, 1)
        wq = _prep_w_uq(mla_w_uq[l])
        wuk, wuvt, vone = _prep_w_ukv(mla_w_ukv[l])
        q, k, vt, caq, cak, cavt, ga, gb = _proj(
            x, ada, mix_norm[l].reshape(1, d), positions.reshape(bsz, s // tm, 1, tm), invf,
            _prep_w_in(w_in[l]), mla_q_norm[l].reshape(1, -1), mla_kv_norm[l].reshape(1, -1),
            wq, wuk, wuvt, vone, tm)
        attn_a = _mla(q, k, vt, 512)
        attn_b = _ca(_prep_bias_table(rel_bias[l]), caq, cak, cavt)
        last = l == w_ada.shape[0] - 1
        assert last, "the fused tail applies the final norm; DEPTH is 1"
        x = _tail(x, ada, attn_a, attn_b, ga, gb, w_branch_a[l].astype(BF16),
                  w_branch_b[l].astype(BF16), w_out[l].astype(BF16),
                  ffn2_norm[l].reshape(1, d), ffn2_w_in[l].astype(BF16),
                  ffn2_w_out[l].astype(BF16), final_norm.reshape(1, d), tm)
    return x
```

```python
import functools

import jax
import jax.numpy as jnp
from jax import lax
from jax.experimental import pallas as pl
from jax.experimental.pallas import tpu as pltpu

D_MODEL = 1024
CHUNK = 64
D_FF = 2816
FFN_RES_WEIGHT = 0.5
MLA_HEADS = 8
MLA_Q_RANK = 256
MLA_KV_RANK = 128
MLA_NOPE = 64
MLA_ROPE = 32
MLA_V = 64
ROPE_THETA = 10000.0
CA_HEADS = 8
CA_HEAD_DIM = 64
CA_LEFT_CHUNKS = 8
MAX_REL_DIST = 256
CA_WIDTH = CA_HEADS * CA_HEAD_DIM
MLA_OUT_WIDTH = MLA_HEADS * MLA_V
EPS = 1e-6
NEG_INF = -1e30
LOG2_E = 1.4426950408889634

LANES = 128
HEAD_SLAB = LANES
VMEM_LIMIT_BYTES = 56 * 1024 * 1024

ZC_QLAT = 0
ZC_KVLAT = ZC_QLAT + MLA_Q_RANK
ZC_KPE = ZC_KVLAT + MLA_KV_RANK
ZC_CAQ = ZC_KPE + LANES
ZC_CAK = ZC_CAQ + CA_WIDTH
ZC_CAV = ZC_CAK + CA_WIDTH
ZC_GA = ZC_CAV + CA_WIDTH
ZC_GB = ZC_GA + D_MODEL
Z_COLS = ZC_GB + D_MODEL

FFN_CHUNKS = ((0, 1024), (1024, 1024), (2048, 768))

BF16 = jnp.bfloat16
F32 = jnp.float32


def _const_spec(shape):
    nd = len(shape)
    return pl.BlockSpec(shape, lambda *_: (0,) * nd, pipeline_mode=pl.Buffered(1))


def _rmsnorm(x, g):
    return x * lax.rsqrt(jnp.mean(x * x, axis=-1, keepdims=True) + EPS) * g


def _mod_norm(x, g, scale, shift):
    y = (x * lax.rsqrt(jnp.mean(x * x, axis=-1, keepdims=True) + EPS)).astype(BF16)
    return y * (g * (1.0 + scale)).astype(BF16) + shift.astype(BF16)


def _dot(a, b):
    return jnp.dot(a, b, preferred_element_type=F32)


def _dot_nt(a, b):
    return lax.dot_general(a, b, (((1,), (1,)), ((), ())), preferred_element_type=F32)


def _ada_kernel(c_ref, w_ref, b_ref, o_ref):
    c = c_ref[...]
    c_act = c * jax.nn.sigmoid(c)
    o_ref[...] = jnp.dot(c_act, w_ref[...], preferred_element_type=F32,
                         precision=lax.Precision.HIGHEST) + b_ref[...]


def _ada(c, w_ada, b_ada):
    bsz, d = c.shape
    n = w_ada.shape[1]
    bn = 1152
    return pl.pallas_call(
        _ada_kernel,
        grid=(n // bn,),
        in_specs=[pl.BlockSpec((bsz, d), lambda j: (0, 0)),
                  pl.BlockSpec((d, bn), lambda j: (0, j)),
                  pl.BlockSpec((1, bn), lambda j: (0, j))],
        out_specs=pl.BlockSpec((bsz, bn), lambda j: (0, j)),
        out_shape=jax.ShapeDtypeStruct((bsz, n), F32),
        name="ada",
    )(c, w_ada, b_ada.reshape(1, n))


def _ffn_body(x, ada_ref, ada_base, norm_ref, w_in_ref, w_out_ref):
    shift = ada_ref[0, ada_base:ada_base + 1, :]
    scale = ada_ref[0, ada_base + 1:ada_base + 2, :]
    gate = ada_ref[0, ada_base + 2:ada_base + 3, :]
    h = _mod_norm(x, norm_ref[...], scale, shift)
    acc = None
    for c0, cw in FFN_CHUNKS:
        g = _dot(h, w_in_ref[:, c0:c0 + cw])
        u = _dot(h, w_in_ref[:, D_FF + c0:D_FF + c0 + cw])
        a = (g * jax.nn.sigmoid(g) * u).astype(BF16)
        part = _dot(a, w_out_ref[c0:c0 + cw, :])
        acc = part if acc is None else acc + part
    return x + (FFN_RES_WEIGHT * gate) * acc


def _ffn1_kernel(x_ref, ada_ref, norm_ref, w_in_ref, w_out_ref, o_ref):
    o_ref[0] = _ffn_body(x_ref[0], ada_ref, 0, norm_ref, w_in_ref, w_out_ref)


def _ffn1(x, ada, norm, w_in, w_out, tm):
    bsz, s, d = x.shape
    row = pl.BlockSpec((1, tm, d), lambda b, i: (b, i, 0))
    return pl.pallas_call(
        _ffn1_kernel,
        grid=(bsz, s // tm),
        in_specs=[row,
                  pl.BlockSpec((1, 9, d), lambda b, i: (b, 0, 0)),
                  _const_spec((1, d)),
                  _const_spec(w_in.shape),
                  _const_spec(w_out.shape)],
        out_specs=row,
        out_shape=jax.ShapeDtypeStruct(x.shape, F32),
        compiler_params=pltpu.CompilerParams(
            dimension_semantics=("parallel", "parallel"),
            vmem_limit_bytes=VMEM_LIMIT_BYTES),
        name="ffn1",
    )(x, ada, norm, w_in, w_out)


PROJ_SUBTILES = 2


def _rope(v, cos_t, sin_t):
    return v * cos_t + pltpu.roll(v, LANES - MLA_ROPE, axis=1) * sin_t


def _proj_kernel(x_ref, ada_ref, norm_ref, pos_ref, invf_ref, w_in_ref, qn_ref, kvn_ref,
                 wq_ref, wuk_ref, wuvt_ref, vone_ref,
                 q_ref, k_ref, vt_ref, caq_ref, cak_ref, cavt_ref, ga_ref, gb_ref):
    shift = ada_ref[0, 3:4, :]
    scale = ada_ref[0, 4:5, :]
    qk_scale = (MLA_NOPE + MLA_ROPE) ** -0.5 * LOG2_E
    rows = x_ref.shape[1] // PROJ_SUBTILES
    one_rows = jnp.ones((MLA_NOPE, rows), F32)
    zero_head = jnp.zeros((MLA_NOPE, rows), F32)
    zero_tail = jnp.zeros((LANES - MLA_NOPE - MLA_ROPE, rows), F32)
    ones_pad = (lax.broadcasted_iota(jnp.int32, (HEAD_SLAB - CA_HEAD_DIM, rows), 0) == 0).astype(F32)
    for sub in range(PROJ_SUBTILES):
        rs = slice(sub * rows, (sub + 1) * rows)
        h = _mod_norm(x_ref[0, rs, :], norm_ref[...], scale, shift)
        z = _dot_nt(h, w_in_ref[...])

        ang = invf_ref[...] * pos_ref[0, 0, :, rs].astype(F32)
        cos_h, sin_h = jnp.cos(ang), jnp.sin(ang)
        cos_t = jnp.transpose(jnp.concatenate([one_rows, cos_h, cos_h, zero_tail], axis=0))
        sin_t = jnp.transpose(jnp.concatenate([zero_head, sin_h, sin_h, zero_tail], axis=0))

        cq = _rmsnorm(z[:, ZC_QLAT:ZC_QLAT + MLA_Q_RANK], qn_ref[...]).astype(BF16)
        qall = _dot(cq, wq_ref[...])
        ckv = _rmsnorm(z[:, ZC_KVLAT:ZC_KVLAT + MLA_KV_RANK], kvn_ref[...]).astype(BF16)
        kn = _dot(ckv, wuk_ref[...])
        kpe = _rope(z[:, ZC_KPE:ZC_KPE + LANES], cos_t, sin_t)
        for hd in range(MLA_HEADS):
            sl = slice(hd * HEAD_SLAB, (hd + 1) * HEAD_SLAB)
            q_ref[0, rs, sl] = (_rope(qall[:, sl], cos_t, sin_t) * qk_scale).astype(BF16)
            k_ref[0, rs, sl] = (kn[:, sl] + kpe).astype(BF16)
        vt_ref[0, :, rs] = (_dot_nt(wuvt_ref[...], ckv) + vone_ref[...]).astype(BF16)

        caq_ref[0, rs, :] = (z[:, ZC_CAQ:ZC_CAQ + CA_WIDTH]
                             * (CA_HEAD_DIM ** -0.5 * LOG2_E)).astype(BF16)
        cak_ref[0, rs, :] = z[:, ZC_CAK:ZC_CAK + CA_WIDTH].astype(BF16)
        cavt = jnp.transpose(z[:, ZC_CAV:ZC_CAV + CA_WIDTH])
        for hd in range(CA_HEADS):
            slab = jnp.concatenate([cavt[hd * CA_HEAD_DIM:(hd + 1) * CA_HEAD_DIM], ones_pad], axis=0)
            cavt_ref[0, hd * HEAD_SLAB:(hd + 1) * HEAD_SLAB, rs] = slab.astype(BF16)
        ga_ref[0, rs, :] = jax.nn.sigmoid(z[:, ZC_GA:ZC_GA + D_MODEL]).astype(BF16)
        gb_ref[0, rs, :] = jax.nn.sigmoid(z[:, ZC_GB:ZC_GB + D_MODEL]).astype(BF16)


def _proj(x, ada, norm, pos, invf, w_in_ext, qn, kvn, wq, wuk, wuvt, vone, tm):
    bsz, s, d = x.shape

    def row(width):
        return pl.BlockSpec((1, tm, width), lambda b, i: (b, i, 0))

    def out(width):
        return jax.ShapeDtypeStruct((bsz, s, width), BF16)

    slabs = MLA_HEADS * HEAD_SLAB
    widths = (slabs, slabs, None, CA_WIDTH, CA_WIDTH, None, D_MODEL, D_MODEL)
    vt_spec = pl.BlockSpec((1, slabs, tm), lambda b, i: (b, 0, i))
    vt_shape = jax.ShapeDtypeStruct((bsz, slabs, s), BF16)
    return pl.pallas_call(
        _proj_kernel,
        grid=(bsz, s // tm),
        in_specs=[row(d),
                  pl.BlockSpec((1, 9, d), lambda b, i: (b, 0, 0)),
                  _const_spec((1, d)),
                  pl.BlockSpec((1, 1, 1, tm), lambda b, i: (b, i, 0, 0)),
                  _const_spec(invf.shape),
                  _const_spec(w_in_ext.shape),
                  _const_spec(qn.shape), _const_spec(kvn.shape),
                  _const_spec(wq.shape),
                  _const_spec(wuk.shape), _const_spec(wuvt.shape), _const_spec(vone.shape)],
        out_specs=[vt_spec if w is None else row(w) for w in widths],
        out_shape=[vt_shape if w is None else out(w) for w in widths],
        compiler_params=pltpu.CompilerParams(
            dimension_semantics=("parallel", "parallel"),
            vmem_limit_bytes=VMEM_LIMIT_BYTES),
        name="proj",
    )(x, ada, norm, pos, invf, w_in_ext, qn, kvn, wq, wuk, wuvt, vone)


MLA_HEADS_PER_STEP = 8
MLA_QSPLIT = 2
MLA_LOOKAHEAD = 3


def _mla_kernel(q_ref, k_ref, vt_ref, o_ref, m_ref, acc_ref, *s_refs, blk, hps):
    qi = pl.program_id(2)
    m_ref[...] = jnp.full(m_ref.shape, NEG_INF, F32)
    acc_ref[...] = jnp.zeros(acc_ref.shape, F32)

    nq = blk // MLA_QSPLIT
    units = [(hd, qh) for hd in range(hps) for qh in range(MLA_QSPLIT)]

    def step(j, mask):
        k0 = pl.multiple_of(j * blk, blk)

        def n_keys(qh):
            return blk if mask is None else (qh + 1) * nq

        def scores(u):
            hd, qh = units[u]
            sl = slice(hd * HEAD_SLAB, (hd + 1) * HEAD_SLAB)
            qs = slice(qh * nq, (qh + 1) * nq)
            nk = n_keys(qh)
            s = _dot_nt(k_ref[0, pl.ds(k0, nk), sl], q_ref[0, qs, sl])
            if mask is not None:
                s = jnp.where(mask[:nk, qs], s, NEG_INF)
            s_refs[u % len(s_refs)][:nk, :] = s
            return jnp.max(s, axis=0, keepdims=True)

        block_max = {u: scores(u) for u in range(min(MLA_LOOKAHEAD, len(units)))}
        for u, (hd, qh) in enumerate(units):
            if u + MLA_LOOKAHEAD < len(units):
                block_max[u + MLA_LOOKAHEAD] = scores(u + MLA_LOOKAHEAD)
            qs = slice(qh * nq, (qh + 1) * nq)
            nk = n_keys(qh)
            s = s_refs[u % len(s_refs)][:nk, :]
            m_prev = m_ref[hd, :, qs]
            m_new = jnp.maximum(m_prev, block_max.pop(u))
            m_ref[hd, :, qs] = m_new
            alpha = jnp.exp2(m_prev - m_new)
            p = jnp.exp2((s - m_new).astype(BF16))
            vt = vt_ref[0, hd * HEAD_SLAB:(hd + 1) * HEAD_SLAB, pl.ds(k0, nk)]
            acc_ref[hd, :, qs] = acc_ref[hd, :, qs] * alpha + _dot(vt, p)

    def body(j, carry):
        step(j, None)
        return carry

    lax.fori_loop(0, qi, body, 0)
    kc = lax.broadcasted_iota(jnp.int32, (blk, blk), 0) // CHUNK
    qc = lax.broadcasted_iota(jnp.int32, (blk, blk), 1) // CHUNK
    step(qi, kc <= qc)
    for pr in range(hps // 2):
        outs = []
        for hd in (2 * pr, 2 * pr + 1):
            a = acc_ref[hd]
            outs.append(a[:MLA_V] * (1.0 / a[MLA_V:MLA_V + 1]))
        o_ref[0, :, pr * LANES:(pr + 1) * LANES] = jnp.transpose(
            jnp.concatenate(outs, axis=0)).astype(BF16)


def _mla(q, k, vt, blk):
    bsz, s, _ = q.shape
    hps = MLA_HEADS_PER_STEP
    return pl.pallas_call(
        functools.partial(_mla_kernel, blk=blk, hps=hps),
        grid=(bsz, MLA_HEADS // hps, s // blk),
        in_specs=[pl.BlockSpec((1, blk, hps * HEAD_SLAB), lambda b, g, i: (b, i, g)),
                  pl.BlockSpec((1, s, hps * HEAD_SLAB), lambda b, g, i: (b, 0, g)),
                  pl.BlockSpec((1, hps * HEAD_SLAB, s), lambda b, g, i: (b, g, 0))],
        out_specs=pl.BlockSpec((1, blk, hps * MLA_V), lambda b, g, i: (b, i, g)),
        out_shape=jax.ShapeDtypeStruct((bsz, s, MLA_OUT_WIDTH), BF16),
        scratch_shapes=[pltpu.VMEM((hps, 1, blk), F32),
                        pltpu.VMEM((hps, HEAD_SLAB, blk), F32)]
                       + [pltpu.VMEM((blk, blk // MLA_QSPLIT), F32)] * (MLA_LOOKAHEAD + 1),
        compiler_params=pltpu.CompilerParams(
            dimension_semantics=("parallel", "parallel", "arbitrary"),
            vmem_limit_bytes=VMEM_LIMIT_BYTES),
        name="mla",
    )(q, k, vt)


CA_TQ = 256
CA_TK = CA_TQ + CA_LEFT_CHUNKS * CHUNK
CA_KBLKS = CA_TK // CA_TQ
CA_TAB = 1024
CA_LOOKAHEAD = 3


def _ca_kernel(tab_ref, q_ref, k_ref, vt_ref, o_ref, bias_ref):
    b, i = pl.program_id(0), pl.program_id(1)

    @pl.when((b == 0) & (i == 0))
    def _build_bias():
        row = lax.broadcasted_iota(jnp.int32, (CA_TQ, CA_TAB), 0)
        kc = lax.broadcasted_iota(jnp.int32, (CA_TK, CA_TQ), 0) // CHUNK
        qc = lax.broadcasted_iota(jnp.int32, (CA_TK, CA_TQ), 1) // CHUNK
        band = (kc >= qc) & (kc <= qc + CA_LEFT_CHUNKS)
        for hd in range(CA_HEADS):
            t = jnp.broadcast_to(tab_ref[hd:hd + 1, :], (CA_TQ, CA_TAB))
            for bit in range(CA_TQ.bit_length() - 1):
                t = jnp.where(((row >> bit) & 1) == 1, pltpu.roll(t, 1 << bit, axis=1), t)
            bias_ref[hd] = jnp.where(band, jnp.transpose(t[:, :CA_TK]) * LOG2_E, NEG_INF)

    lo = lax.broadcasted_iota(jnp.int32, (1, LANES), 1) < CA_HEAD_DIM

    def attend(nkb):
        nk = nkb * CA_TQ
        k0 = pl.multiple_of((i - (nkb - 1)) * CA_TQ, CA_TQ)

        def scores(hd):
            sl = slice((hd // 2) * LANES, (hd // 2 + 1) * LANES)
            q = q_ref[0, :, sl]
            zero = jnp.zeros_like(q)
            qh = jnp.where(lo, q, zero) if hd % 2 == 0 else jnp.where(lo, zero, q)
            return _dot_nt(k_ref[0, pl.ds(k0, nk), sl], qh) + bias_ref[hd, CA_TK - nk:, :]

        pending = {hd: scores(hd) for hd in range(CA_LOOKAHEAD)}
        outs = []
        for hd in range(CA_HEADS):
            s = pending.pop(hd)
            if hd + CA_LOOKAHEAD < CA_HEADS:
                pending[hd + CA_LOOKAHEAD] = scores(hd + CA_LOOKAHEAD)
            m = jnp.max(s, axis=0, keepdims=True)
            p = jnp.exp2((s - m).astype(BF16))
            vt = vt_ref[0, hd * HEAD_SLAB:(hd + 1) * HEAD_SLAB, pl.ds(k0, nk)]
            a = _dot(vt, p)
            outs.append(a[:CA_HEAD_DIM] * (1.0 / a[CA_HEAD_DIM:CA_HEAD_DIM + 1]))
            if hd % 2 == 1:
                pr = hd // 2
                o_ref[0, :, pr * LANES:(pr + 1) * LANES] = jnp.transpose(
                    jnp.concatenate(outs, axis=0)).astype(BF16)
                outs = []

    for nkb in range(1, CA_KBLKS):
        pl.when(i == nkb - 1)(functools.partial(attend, nkb))
    pl.when(i >= CA_KBLKS - 1)(functools.partial(attend, CA_KBLKS))


def _ca(tab, q, k, vt):
    bsz, s, _ = q.shape
    blk = pl.BlockSpec((1, CA_TQ, CA_WIDTH), lambda b, i: (b, i, 0))
    return pl.pallas_call(
        _ca_kernel,
        grid=(bsz, s // CA_TQ),
        in_specs=[_const_spec(tab.shape), blk,
                  pl.BlockSpec((1, s, CA_WIDTH), lambda b, i: (b, 0, 0)),
                  pl.BlockSpec((1, CA_HEADS * HEAD_SLAB, s), lambda b, i: (b, 0, 0))],
        out_specs=blk,
        out_shape=jax.ShapeDtypeStruct((bsz, s, CA_WIDTH), BF16),
        scratch_shapes=[pltpu.VMEM((CA_HEADS, CA_TK, CA_TQ), F32)],
        compiler_params=pltpu.CompilerParams(
            dimension_semantics=("arbitrary", "arbitrary"),
            vmem_limit_bytes=VMEM_LIMIT_BYTES),
        name="ca",
    )(tab, q, k, vt)


def _tail_kernel(x_ref, ada_ref, aa_ref, ab_ref, ga_ref, gb_ref, wa_ref, wb_ref, wo_ref,
                 norm_ref, w_in_ref, w_out_ref, fnorm_ref, o_ref):
    ya = _dot(aa_ref[0], wa_ref[...])
    yb = _dot(ab_ref[0], wb_ref[...])
    merged = ga_ref[0] * ya.astype(BF16) + gb_ref[0] * yb.astype(BF16)
    x = x_ref[0] + ada_ref[0, 5:6, :] * _dot(merged, wo_ref[...])
    x = _ffn_body(x, ada_ref, 6, norm_ref, w_in_ref, w_out_ref)
    o_ref[0] = _rmsnorm(x, fnorm_ref[...])


def _tail(x, ada, aa, ab, ga, gb, wa, wb, wo, norm, w_in, w_out, fnorm, tm):
    bsz, s, d = x.shape

    def row(width):
        return pl.BlockSpec((1, tm, width), lambda b, i: (b, i, 0))

    return pl.pallas_call(
        _tail_kernel,
        grid=(bsz, s // tm),
        in_specs=[row(d),
                  pl.BlockSpec((1, 9, d), lambda b, i: (b, 0, 0)),
                  row(MLA_OUT_WIDTH), row(CA_WIDTH), row(d), row(d),
                  _const_spec(wa.shape), _const_spec(wb.shape), _const_spec(wo.shape),
                  _const_spec((1, d)), _const_spec(w_in.shape), _const_spec(w_out.shape),
                  _const_spec((1, d))],
        out_specs=row(d),
        out_shape=jax.ShapeDtypeStruct(x.shape, F32),
        compiler_params=pltpu.CompilerParams(
            dimension_semantics=("parallel", "parallel"),
            vmem_limit_bytes=VMEM_LIMIT_BYTES),
        name="tail",
    )(x, ada, aa, ab, ga, gb, wa, wb, wo, norm, w_in, w_out, fnorm)


def _rotate_half_cols(w):
    half = w.shape[-1] // 2
    return jnp.concatenate([-w[..., half:], w[..., :half]], axis=-1)


def _prep_w_in(w_in):
    d = w_in.shape[0]
    o = 0
    pieces = {}
    for name, width in (("qlat", MLA_Q_RANK), ("kvlat", MLA_KV_RANK), ("kpe", MLA_ROPE),
                        ("caq", CA_WIDTH), ("cak", CA_WIDTH), ("cav", CA_WIDTH),
                        ("ga", D_MODEL), ("gb", D_MODEL)):
        pieces[name] = w_in[:, o:o + width]
        o += width
    zl = jnp.zeros((d, MLA_NOPE), w_in.dtype)
    kpe = jnp.concatenate([zl, pieces["kpe"], _rotate_half_cols(pieces["kpe"])], axis=1)
    ext = jnp.concatenate([pieces["qlat"].T, pieces["kvlat"].T, kpe.T, pieces["caq"].T,
                           pieces["cak"].T, pieces["cav"].T, pieces["ga"].T, pieces["gb"].T], axis=0)
    assert ext.shape[0] == Z_COLS
    return ext.astype(BF16)


def _prep_w_uq(w_uq):
    r = w_uq.shape[0]
    w = w_uq.reshape(r, MLA_HEADS, MLA_NOPE + MLA_ROPE)
    nope, pe = w[..., :MLA_NOPE], w[..., MLA_NOPE:]
    slab = jnp.concatenate([nope, pe, _rotate_half_cols(pe)], axis=-1)
    return slab.reshape(r, MLA_HEADS * HEAD_SLAB).astype(BF16)


def _prep_w_ukv(w_ukv):
    r = w_ukv.shape[0]
    w = w_ukv.reshape(r, MLA_HEADS, MLA_NOPE + MLA_V)
    k_nope, v = w[..., :MLA_NOPE], w[..., MLA_NOPE:]
    zk = jnp.zeros((r, MLA_HEADS, HEAD_SLAB - MLA_NOPE), w.dtype)
    wuk = jnp.concatenate([k_nope, zk], axis=-1).reshape(r, MLA_HEADS * HEAD_SLAB)
    zv = jnp.zeros((r, MLA_HEADS, HEAD_SLAB - MLA_V), w.dtype)
    wuvt = jnp.concatenate([v, zv], axis=-1).reshape(r, MLA_HEADS * HEAD_SLAB).T
    vone = (jnp.arange(MLA_HEADS * HEAD_SLAB) % HEAD_SLAB == MLA_V).astype(F32).reshape(-1, 1)
    return wuk.astype(BF16), wuvt.astype(BF16), vone


def _prep_bias_table(rel_bias):
    j = jnp.arange(CA_TAB)
    t = (j + CA_TQ - 1) % CA_TAB
    rel = (CA_TK - 1) - t
    idx = jnp.clip(rel, -MAX_REL_DIST, MAX_REL_DIST) + MAX_REL_DIST
    return rel_bias[idx].T.astype(F32)


def kernel(x, c, positions, w_ada, b_ada, ffn1_norm, ffn1_w_in, ffn1_w_out, mix_norm, w_in,
           mla_q_norm, mla_w_uq, mla_kv_norm, mla_w_ukv, rel_bias, w_branch_a, w_branch_b,
           w_out, ffn2_norm, ffn2_w_in, ffn2_w_out, final_norm):
    bsz, s, d = x.shape
    tm = 512
    for l in range(w_ada.shape[0]):
        ada = _ada(c, w_ada[l], b_ada[l]).reshape(bsz, 9, d)
        x = _ffn1(x, ada, ffn1_norm[l].reshape(1, d), ffn1_w_in[l].astype(BF16),
                  ffn1_w_out[l].astype(BF16), tm)

        freq = jnp.arange(0, MLA_ROPE, 2, dtype=F32) / MLA_ROPE
        inv_freq = ROPE_THETA ** (-freq)
        invf = inv_freq.reshape(MLA_ROPE // 2, 1)
        wq = _prep_w_uq(mla_w_uq[l])
        wuk, wuvt, vone = _prep_w_ukv(mla_w_ukv[l])
        q, k, vt, caq, cak, cavt, ga, gb = _proj(
            x, ada, mix_norm[l].reshape(1, d), positions.reshape(bsz, s // tm, 1, tm), invf,
            _prep_w_in(w_in[l]), mla_q_norm[l].reshape(1, -1), mla_kv_norm[l].reshape(1, -1),
            wq, wuk, wuvt, vone, tm)
        attn_a = _mla(q, k, vt, 512)
        attn_b = _ca(_prep_bias_table(rel_bias[l]), caq, cak, cavt)
        last = l == w_ada.shape[0] - 1
        assert last, "the fused tail applies the final norm; DEPTH is 1"
        x = _tail(x, ada, attn_a, attn_b, ga, gb, w_branch_a[l].astype(BF16),
                  w_branch_b[l].astype(BF16), w_out[l].astype(BF16),
                  ffn2_norm[l].reshape(1, d), ffn2_w_in[l].astype(BF16),
                  ffn2_w_out[l].astype(BF16), final_norm.reshape(1, d), tm)
    return x
```

```python
import functools

import jax
import jax.numpy as jnp
from jax import lax
from jax.experimental import pallas as pl
from jax.experimental.pallas import tpu as pltpu

D_MODEL = 1024
CHUNK = 64
D_FF = 2816
FFN_RES_WEIGHT = 0.5
MLA_HEADS = 8
MLA_Q_RANK = 256
MLA_KV_RANK = 128
MLA_NOPE = 64
MLA_ROPE = 32
MLA_V = 64
ROPE_THETA = 10000.0
CA_HEADS = 8
CA_HEAD_DIM = 64
CA_LEFT_CHUNKS = 8
MAX_REL_DIST = 256
CA_WIDTH = CA_HEADS * CA_HEAD_DIM
MLA_OUT_WIDTH = MLA_HEADS * MLA_V
EPS = 1e-6
NEG_INF = -1e30
LOG2_E = 1.4426950408889634

LANES = 128
HEAD_SLAB = LANES
VMEM_LIMIT_BYTES = 56 * 1024 * 1024

ZC_QLAT = 0
ZC_KVLAT = ZC_QLAT + MLA_Q_RANK
ZC_KPE = ZC_KVLAT + MLA_KV_RANK
ZC_CAQ = ZC_KPE + LANES
ZC_CAK = ZC_CAQ + CA_WIDTH
ZC_CAV = ZC_CAK + CA_WIDTH
ZC_GA = ZC_CAV + CA_WIDTH
ZC_GB = ZC_GA + D_MODEL
Z_COLS = ZC_GB + D_MODEL

FFN_CHUNKS = ((0, 1024), (1024, 1024), (2048, 768))

BF16 = jnp.bfloat16
F32 = jnp.float32


def _const_spec(shape):
    nd = len(shape)
    return pl.BlockSpec(shape, lambda *_: (0,) * nd, pipeline_mode=pl.Buffered(1))


def _rmsnorm(x, g):
    return x * lax.rsqrt(jnp.mean(x * x, axis=-1, keepdims=True) + EPS) * g


def _mod_norm(x, g, scale, shift):
    y = (x * lax.rsqrt(jnp.mean(x * x, axis=-1, keepdims=True) + EPS)).astype(BF16)
    return y * (g * (1.0 + scale)).astype(BF16) + shift.astype(BF16)


def _dot(a, b):
    return jnp.dot(a, b, preferred_element_type=F32)


def _dot_nt(a, b):
    return lax.dot_general(a, b, (((1,), (1,)), ((), ())), preferred_element_type=F32)


def _ada_kernel(c_ref, w_ref, b_ref, o_ref):
    c = c_ref[...]
    c_act = c * jax.nn.sigmoid(c)
    o_ref[...] = jnp.dot(c_act, w_ref[...], preferred_element_type=F32,
                         precision=lax.Precision.HIGHEST) + b_ref[...]


def _ada(c, w_ada, b_ada):
    bsz, d = c.shape
    n = w_ada.shape[1]
    bn = 1152
    return pl.pallas_call(
        _ada_kernel,
        grid=(n // bn,),
        in_specs=[pl.BlockSpec((bsz, d), lambda j: (0, 0)),
                  pl.BlockSpec((d, bn), lambda j: (0, j)),
                  pl.BlockSpec((1, bn), lambda j: (0, j))],
        out_specs=pl.BlockSpec((bsz, bn), lambda j: (0, j)),
        out_shape=jax.ShapeDtypeStruct((bsz, n), F32),
        name="ada",
    )(c, w_ada, b_ada.reshape(1, n))


def _ffn_body(x, ada_ref, ada_base, norm_ref, w_in_ref, w_out_ref):
    shift = ada_ref[0, ada_base:ada_base + 1, :]
    scale = ada_ref[0, ada_base + 1:ada_base + 2, :]
    gate = ada_ref[0, ada_base + 2:ada_base + 3, :]
    h = _mod_norm(x, norm_ref[...], scale, shift)
    acc = None
    for c0, cw in FFN_CHUNKS:
        g = _dot(h, w_in_ref[:, c0:c0 + cw])
        u = _dot(h, w_in_ref[:, D_FF + c0:D_FF + c0 + cw])
        a = (g * jax.nn.sigmoid(g) * u).astype(BF16)
        part = _dot(a, w_out_ref[c0:c0 + cw, :])
        acc = part if acc is None else acc + part
    return x + (FFN_RES_WEIGHT * gate) * acc


def _ffn1_kernel(x_ref, ada_ref, norm_ref, w_in_ref, w_out_ref, o_ref):
    o_ref[0] = _ffn_body(x_ref[0], ada_ref, 0, norm_ref, w_in_ref, w_out_ref)


def _ffn1(x, ada, norm, w_in, w_out, tm):
    bsz, s, d = x.shape
    row = pl.BlockSpec((1, tm, d), lambda b, i: (b, i, 0))
    return pl.pallas_call(
        _ffn1_kernel,
        grid=(bsz, s // tm),
        in_specs=[row,
                  pl.BlockSpec((1, 9, d), lambda b, i: (b, 0, 0)),
                  _const_spec((1, d)),
                  _const_spec(w_in.shape),
                  _const_spec(w_out.shape)],
        out_specs=row,
        out_shape=jax.ShapeDtypeStruct(x.shape, F32),
        compiler_params=pltpu.CompilerParams(
            dimension_semantics=("parallel", "parallel"),
            vmem_limit_bytes=VMEM_LIMIT_BYTES),
        name="ffn1",
    )(x, ada, norm, w_in, w_out)


PROJ_SUBTILES = 2


def _rope(v, cos_t, sin_t):
    return v * cos_t + pltpu.roll(v, LANES - MLA_ROPE, axis=1) * sin_t


def _proj_kernel(x_ref, ada_ref, norm_ref, pos_ref, invf_ref, w_in_ref, qn_ref, kvn_ref,
                 wq_ref, wuk_ref, wuvt_ref, vone_ref,
                 q_ref, k_ref, vt_ref, caq_ref, cak_ref, cavt_ref, ga_ref, gb_ref):
    shift = ada_ref[0, 3:4, :]
    scale = ada_ref[0, 4:5, :]
    qk_scale = (MLA_NOPE + MLA_ROPE) ** -0.5 * LOG2_E
    rows = x_ref.shape[1] // PROJ_SUBTILES
    one_rows = jnp.ones((MLA_NOPE, rows), F32)
    zero_head = jnp.zeros((MLA_NOPE, rows), F32)
    zero_tail = jnp.zeros((LANES - MLA_NOPE - MLA_ROPE, rows), F32)
    ones_pad = (lax.broadcasted_iota(jnp.int32, (HEAD_SLAB - CA_HEAD_DIM, rows), 0) == 0).astype(F32)
    for sub in range(PROJ_SUBTILES):
        rs = slice(sub * rows, (sub + 1) * rows)
        h = _mod_norm(x_ref[0, rs, :], norm_ref[...], scale, shift)
        z = _dot_nt(h, w_in_ref[...])

        ang = invf_ref[...] * pos_ref[0, 0, :, rs].astype(F32)
        cos_h, sin_h = jnp.cos(ang), jnp.sin(ang)
        cos_t = jnp.transpose(jnp.concatenate([one_rows, cos_h, cos_h, zero_tail], axis=0))
        sin_t = jnp.transpose(jnp.concatenate([zero_head, sin_h, sin_h, zero_tail], axis=0))

        cq = _rmsnorm(z[:, ZC_QLAT:ZC_QLAT + MLA_Q_RANK], qn_ref[...]).astype(BF16)
        qall = _dot(cq, wq_ref[...])
        ckv = _rmsnorm(z[:, ZC_KVLAT:ZC_KVLAT + MLA_KV_RANK], kvn_ref[...]).astype(BF16)
        kn = _dot(ckv, wuk_ref[...])
        kpe = _rope(z[:, ZC_KPE:ZC_KPE + LANES], cos_t, sin_t)
        for hd in range(MLA_HEADS):
            sl = slice(hd * HEAD_SLAB, (hd + 1) * HEAD_SLAB)
            q_ref[0, rs, sl] = (_rope(qall[:, sl], cos_t, sin_t) * qk_scale).astype(BF16)
            k_ref[0, rs, sl] = (kn[:, sl] + kpe).astype(BF16)
        vt_ref[0, :, rs] = (_dot_nt(wuvt_ref[...], ckv) + vone_ref[...]).astype(BF16)

        caq_ref[0, rs, :] = (z[:, ZC_CAQ:ZC_CAQ + CA_WIDTH]
                             * (CA_HEAD_DIM ** -0.5 * LOG2_E)).astype(BF16)
        cak_ref[0, rs, :] = z[:, ZC_CAK:ZC_CAK + CA_WIDTH].astype(BF16)
        cavt = jnp.transpose(z[:, ZC_CAV:ZC_CAV + CA_WIDTH])
        for hd in range(CA_HEADS):
            slab = jnp.concatenate([cavt[hd * CA_HEAD_DIM:(hd + 1) * CA_HEAD_DIM], ones_pad], axis=0)
            cavt_ref[0, hd * HEAD_SLAB:(hd + 1) * HEAD_SLAB, rs] = slab.astype(BF16)
        ga_ref[0, rs, :] = jax.nn.sigmoid(z[:, ZC_GA:ZC_GA + D_MODEL]).astype(BF16)
        gb_ref[0, rs, :] = jax.nn.sigmoid(z[:, ZC_GB:ZC_GB + D_MODEL]).astype(BF16)


def _proj(x, ada, norm, pos, invf, w_in_ext, qn, kvn, wq, wuk, wuvt, vone, tm):
    bsz, s, d = x.shape

    def row(width):
        return pl.BlockSpec((1, tm, width), lambda b, i: (b, i, 0))

    def out(width):
        return jax.ShapeDtypeStruct((bsz, s, width), BF16)

    slabs = MLA_HEADS * HEAD_SLAB
    widths = (slabs, slabs, None, CA_WIDTH, CA_WIDTH, None, D_MODEL, D_MODEL)
    vt_spec = pl.BlockSpec((1, slabs, tm), lambda b, i: (b, 0, i))
    vt_shape = jax.ShapeDtypeStruct((bsz, slabs, s), BF16)
    return pl.pallas_call(
        _proj_kernel,
        grid=(bsz, s // tm),
        in_specs=[row(d),
                  pl.BlockSpec((1, 9, d), lambda b, i: (b, 0, 0)),
                  _const_spec((1, d)),
                  pl.BlockSpec((1, 1, 1, tm), lambda b, i: (b, i, 0, 0)),
                  _const_spec(invf.shape),
                  _const_spec(w_in_ext.shape),
                  _const_spec(qn.shape), _const_spec(kvn.shape),
                  _const_spec(wq.shape),
                  _const_spec(wuk.shape), _const_spec(wuvt.shape), _const_spec(vone.shape)],
        out_specs=[vt_spec if w is None else row(w) for w in widths],
        out_shape=[vt_shape if w is None else out(w) for w in widths],
        compiler_params=pltpu.CompilerParams(
            dimension_semantics=("parallel", "parallel"),
            vmem_limit_bytes=VMEM_LIMIT_BYTES),
        name="proj",
    )(x, ada, norm, pos, invf, w_in_ext, qn, kvn, wq, wuk, wuvt, vone)


MLA_HEADS_PER_STEP = 8
MLA_QSPLIT = 2
MLA_LOOKAHEAD = 3


def _mla_kernel(q_ref, k_ref, vt_ref, o_ref, m_ref, acc_ref, *s_refs, blk, hps):
    qi = pl.program_id(2)
    m_ref[...] = jnp.full(m_ref.shape, NEG_INF, F32)
    acc_ref[...] = jnp.zeros(acc_ref.shape, F32)

    nq = blk // MLA_QSPLIT
    units = [(hd, qh) for hd in range(hps) for qh in range(MLA_QSPLIT)]

    def step(j, mask):
        k0 = pl.multiple_of(j * blk, blk)

        def n_keys(qh):
            return blk if mask is None else (qh + 1) * nq

        def scores(u):
            hd, qh = units[u]
            sl = slice(hd * HEAD_SLAB, (hd + 1) * HEAD_SLAB)
            qs = slice(qh * nq, (qh + 1) * nq)
            nk = n_keys(qh)
            s = _dot_nt(k_ref[0, pl.ds(k0, nk), sl], q_ref[0, qs, sl])
            if mask is not None:
                s = jnp.where(mask[:nk, qs], s, NEG_INF)
            s_refs[u % len(s_refs)][:nk, :] = s
            return jnp.max(s, axis=0, keepdims=True)

        block_max = {u: scores(u) for u in range(min(MLA_LOOKAHEAD, len(units)))}
        for u, (hd, qh) in enumerate(units):
            if u + MLA_LOOKAHEAD < len(units):
                block_max[u + MLA_LOOKAHEAD] = scores(u + MLA_LOOKAHEAD)
            qs = slice(qh * nq, (qh + 1) * nq)
            nk = n_keys(qh)
            s = s_refs[u % len(s_refs)][:nk, :]
            m_prev = m_ref[hd, :, qs]
            m_new = jnp.maximum(m_prev, block_max.pop(u))
            m_ref[hd, :, qs] = m_new
            alpha = jnp.exp2(m_prev - m_new)
            p = jnp.exp2((s - m_new).astype(BF16))
            vt = vt_ref[0, hd * HEAD_SLAB:(hd + 1) * HEAD_SLAB, pl.ds(k0, nk)]
            acc_ref[hd, :, qs] = acc_ref[hd, :, qs] * alpha + _dot(vt, p)

    def body(j, carry):
        step(j, None)
        return carry

    lax.fori_loop(0, qi, body, 0)
    kc = lax.broadcasted_iota(jnp.int32, (blk, blk), 0) // CHUNK
    qc = lax.broadcasted_iota(jnp.int32, (blk, blk), 1) // CHUNK
    step(qi, kc <= qc)
    for pr in range(hps // 2):
        outs = []
        for hd in (2 * pr, 2 * pr + 1):
            a = acc_ref[hd]
            outs.append(a[:MLA_V] * (1.0 / a[MLA_V:MLA_V + 1]))
        o_ref[0, :, pr * LANES:(pr + 1) * LANES] = jnp.transpose(
            jnp.concatenate(outs, axis=0)).astype(BF16)


def _mla(q, k, vt, blk):
    bsz, s, _ = q.shape
    hps = MLA_HEADS_PER_STEP
    return pl.pallas_call(
        functools.partial(_mla_kernel, blk=blk, hps=hps),
        grid=(bsz, MLA_HEADS // hps, s // blk),
        in_specs=[pl.BlockSpec((1, blk, hps * HEAD_SLAB), lambda b, g, i: (b, i, g)),
                  pl.BlockSpec((1, s, hps * HEAD_SLAB), lambda b, g, i: (b, 0, g)),
                  pl.BlockSpec((1, hps * HEAD_SLAB, s), lambda b, g, i: (b, g, 0))],
        out_specs=pl.BlockSpec((1, blk, hps * MLA_V), lambda b, g, i: (b, i, g)),
        out_shape=jax.ShapeDtypeStruct((bsz, s, MLA_OUT_WIDTH), BF16),
        scratch_shapes=[pltpu.VMEM((hps, 1, blk), F32),
                        pltpu.VMEM((hps, HEAD_SLAB, blk), F32)]
                       + [pltpu.VMEM((blk, blk // MLA_QSPLIT), F32)] * (MLA_LOOKAHEAD + 1),
        compiler_params=pltpu.CompilerParams(
            dimension_semantics=("parallel", "parallel", "arbitrary"),
            vmem_limit_bytes=VMEM_LIMIT_BYTES),
        name="mla",
    )(q, k, vt)


CA_TQ = 256
CA_TK = CA_TQ + CA_LEFT_CHUNKS * CHUNK
CA_KBLKS = CA_TK // CA_TQ
CA_TAB = 1024
CA_LOOKAHEAD = 3


def _ca_kernel(tab_ref, q_ref, k_ref, vt_ref, o_ref, bias_ref, *s_refs):
    b, i = pl.program_id(0), pl.program_id(1)

    @pl.when((b == 0) & (i == 0))
    def _build_bias():
        row = lax.broadcasted_iota(jnp.int32, (CA_TQ, CA_TAB), 0)
        kc = lax.broadcasted_iota(jnp.int32, (CA_TK, CA_TQ), 0) // CHUNK
        qc = lax.broadcasted_iota(jnp.int32, (CA_TK, CA_TQ), 1) // CHUNK
        band = (kc >= qc) & (kc <= qc + CA_LEFT_CHUNKS)
        for hd in range(CA_HEADS):
            t = jnp.broadcast_to(tab_ref[hd:hd + 1, :], (CA_TQ, CA_TAB))
            for bit in range(CA_TQ.bit_length() - 1):
                t = jnp.where(((row >> bit) & 1) == 1, pltpu.roll(t, 1 << bit, axis=1), t)
            bias_ref[hd] = jnp.where(band, jnp.transpose(t[:, :CA_TK]) * LOG2_E, NEG_INF)

    lo = lax.broadcasted_iota(jnp.int32, (1, LANES), 1) < CA_HEAD_DIM

    def attend(nkb):
        nk = nkb * CA_TQ
        k0 = pl.multiple_of((i - (nkb - 1)) * CA_TQ, CA_TQ)

        def scores(hd):
            sl = slice((hd // 2) * LANES, (hd // 2 + 1) * LANES)
            q = q_ref[0, :, sl]
            zero = jnp.zeros_like(q)
            qh = jnp.where(lo, q, zero) if hd % 2 == 0 else jnp.where(lo, zero, q)
            s = _dot_nt(k_ref[0, pl.ds(k0, nk), sl], qh) + bias_ref[hd, CA_TK - nk:, :]
            s_refs[hd % len(s_refs)][:nk, :] = s
            return jnp.max(s, axis=0, keepdims=True)

        col_max = {hd: scores(hd) for hd in range(CA_LOOKAHEAD)}
        outs = []
        for hd in range(CA_HEADS):
            if hd + CA_LOOKAHEAD < CA_HEADS:
                col_max[hd + CA_LOOKAHEAD] = scores(hd + CA_LOOKAHEAD)
            s = s_refs[hd % len(s_refs)][:nk, :]
            p = jnp.exp2((s - col_max.pop(hd)).astype(BF16))
            vt = vt_ref[0, hd * HEAD_SLAB:(hd + 1) * HEAD_SLAB, pl.ds(k0, nk)]
            a = _dot(vt, p)
            outs.append(a[:CA_HEAD_DIM] * (1.0 / a[CA_HEAD_DIM:CA_HEAD_DIM + 1]))
            if hd % 2 == 1:
                pr = hd // 2
                o_ref[0, :, pr * LANES:(pr + 1) * LANES] = jnp.transpose(
                    jnp.concatenate(outs, axis=0)).astype(BF16)
                outs = []

    for nkb in range(1, CA_KBLKS):
        pl.when(i == nkb - 1)(functools.partial(attend, nkb))
    pl.when(i >= CA_KBLKS - 1)(functools.partial(attend, CA_KBLKS))


def _ca(tab, q, k, vt):
    bsz, s, _ = q.shape
    blk = pl.BlockSpec((1, CA_TQ, CA_WIDTH), lambda b, i: (b, i, 0))
    return pl.pallas_call(
        _ca_kernel,
        grid=(bsz, s // CA_TQ),
        in_specs=[_const_spec(tab.shape), blk,
                  pl.BlockSpec((1, s, CA_WIDTH), lambda b, i: (b, 0, 0)),
                  pl.BlockSpec((1, CA_HEADS * HEAD_SLAB, s), lambda b, i: (b, 0, 0))],
        out_specs=blk,
        out_shape=jax.ShapeDtypeStruct((bsz, s, CA_WIDTH), BF16),
        scratch_shapes=[pltpu.VMEM((CA_HEADS, CA_TK, CA_TQ), F32)]
                       + [pltpu.VMEM((CA_TK, CA_TQ), F32)] * (CA_LOOKAHEAD + 1),
        compiler_params=pltpu.CompilerParams(
            dimension_semantics=("arbitrary", "arbitrary"),
            vmem_limit_bytes=VMEM_LIMIT_BYTES),
        name="ca",
    )(tab, q, k, vt)


def _tail_kernel(x_ref, ada_ref, aa_ref, ab_ref, ga_ref, gb_ref, wa_ref, wb_ref, wo_ref,
                 norm_ref, w_in_ref, w_out_ref, fnorm_ref, o_ref):
    ya = _dot(aa_ref[0], wa_ref[...])
    yb = _dot(ab_ref[0], wb_ref[...])
    merged = ga_ref[0] * ya.astype(BF16) + gb_ref[0] * yb.astype(BF16)
    x = x_ref[0] + ada_ref[0, 5:6, :] * _dot(merged, wo_ref[...])
    x = _ffn_body(x, ada_ref, 6, norm_ref, w_in_ref, w_out_ref)
    o_ref[0] = _rmsnorm(x, fnorm_ref[...])


def _tail(x, ada, aa, ab, ga, gb, wa, wb, wo, norm, w_in, w_out, fnorm, tm):
    bsz, s, d = x.shape

    def row(width):
        return pl.BlockSpec((1, tm, width), lambda b, i: (b, i, 0))

    return pl.pallas_call(
        _tail_kernel,
        grid=(bsz, s // tm),
        in_specs=[row(d),
                  pl.BlockSpec((1, 9, d), lambda b, i: (b, 0, 0)),
                  row(MLA_OUT_WIDTH), row(CA_WIDTH), row(d), row(d),
                  _const_spec(wa.shape), _const_spec(wb.shape), _const_spec(wo.shape),
                  _const_spec((1, d)), _const_spec(w_in.shape), _const_spec(w_out.shape),
                  _const_spec((1, d))],
        out_specs=row(d),
        out_shape=jax.ShapeDtypeStruct(x.shape, F32),
        compiler_params=pltpu.CompilerParams(
            dimension_semantics=("parallel", "parallel"),
            vmem_limit_bytes=VMEM_LIMIT_BYTES),
        name="tail",
    )(x, ada, aa, ab, ga, gb, wa, wb, wo, norm, w_in, w_out, fnorm)


def _rotate_half_cols(w):
    half = w.shape[-1] // 2
    return jnp.concatenate([-w[..., half:], w[..., :half]], axis=-1)


def _prep_w_in(w_in):
    d = w_in.shape[0]
    o = 0
    pieces = {}
    for name, width in (("qlat", MLA_Q_RANK), ("kvlat", MLA_KV_RANK), ("kpe", MLA_ROPE),
                        ("caq", CA_WIDTH), ("cak", CA_WIDTH), ("cav", CA_WIDTH),
                        ("ga", D_MODEL), ("gb", D_MODEL)):
        pieces[name] = w_in[:, o:o + width]
        o += width
    zl = jnp.zeros((d, MLA_NOPE), w_in.dtype)
    kpe = jnp.concatenate([zl, pieces["kpe"], _rotate_half_cols(pieces["kpe"])], axis=1)
    ext = jnp.concatenate([pieces["qlat"].T, pieces["kvlat"].T, kpe.T, pieces["caq"].T,
                           pieces["cak"].T, pieces["cav"].T, pieces["ga"].T, pieces["gb"].T], axis=0)
    assert ext.shape[0] == Z_COLS
    return ext.astype(BF16)


def _prep_w_uq(w_uq):
    r = w_uq.shape[0]
    w = w_uq.reshape(r, MLA_HEADS, MLA_NOPE + MLA_ROPE)
    nope, pe = w[..., :MLA_NOPE], w[..., MLA_NOPE:]
    slab = jnp.concatenate([nope, pe, _rotate_half_cols(pe)], axis=-1)
    return slab.reshape(r, MLA_HEADS * HEAD_SLAB).astype(BF16)


def _prep_w_ukv(w_ukv):
    r = w_ukv.shape[0]
    w = w_ukv.reshape(r, MLA_HEADS, MLA_NOPE + MLA_V)
    k_nope, v = w[..., :MLA_NOPE], w[..., MLA_NOPE:]
    zk = jnp.zeros((r, MLA_HEADS, HEAD_SLAB - MLA_NOPE), w.dtype)
    wuk = jnp.concatenate([k_nope, zk], axis=-1).reshape(r, MLA_HEADS * HEAD_SLAB)
    zv = jnp.zeros((r, MLA_HEADS, HEAD_SLAB - MLA_V), w.dtype)
    wuvt = jnp.concatenate([v, zv], axis=-1).reshape(r, MLA_HEADS * HEAD_SLAB).T
    vone = (jnp.arange(MLA_HEADS * HEAD_SLAB) % HEAD_SLAB == MLA_V).astype(F32).reshape(-1, 1)
    return wuk.astype(BF16), wuvt.astype(BF16), vone


def _prep_bias_table(rel_bias):
    j = jnp.arange(CA_TAB)
    t = (j + CA_TQ - 1) % CA_TAB
    rel = (CA_TK - 1) - t
    idx = jnp.clip(rel, -MAX_REL_DIST, MAX_REL_DIST) + MAX_REL_DIST
    return rel_bias[idx].T.astype(F32)


def kernel(x, c, positions, w_ada, b_ada, ffn1_norm, ffn1_w_in, ffn1_w_out, mix_norm, w_in,
           mla_q_norm, mla_w_uq, mla_kv_norm, mla_w_ukv, rel_bias, w_branch_a, w_branch_b,
           w_out, ffn2_norm, ffn2_w_in, ffn2_w_out, final_norm):
    bsz, s, d = x.shape
    tm = 512
    for l in range(w_ada.shape[0]):
        ada = _ada(c, w_ada[l], b_ada[l]).reshape(bsz, 9, d)
        x = _ffn1(x, ada, ffn1_norm[l].reshape(1, d), ffn1_w_in[l].astype(BF16),
                  ffn1_w_out[l].astype(BF16), tm)

        freq = jnp.arange(0, MLA_ROPE, 2, dtype=F32) / MLA_ROPE
        inv_freq = ROPE_THETA ** (-freq)
        invf = inv_freq.reshape(MLA_ROPE // 2, 1)
        wq = _prep_w_uq(mla_w_uq[l])
        wuk, wuvt, vone = _prep_w_ukv(mla_w_ukv[l])
        q, k, vt, caq, cak, cavt, ga, gb = _proj(
            x, ada, mix_norm[l].reshape(1, d), positions.reshape(bsz, s // tm, 1, tm), invf,
            _prep_w_in(w_in[l]), mla_q_norm[l].reshape(1, -1), mla_kv_norm[l].reshape(1, -1),
            wq, wuk, wuvt, vone, tm)
        attn_a = _mla(q, k, vt, 512)
        attn_b = _ca(_prep_bias_table(rel_bias[l]), caq, cak, cavt)
        last = l == w_ada.shape[0] - 1
        assert last, "the fused tail applies the final norm; DEPTH is 1"
        x = _tail(x, ada, attn_a, attn_b, ga, gb, w_branch_a[l].astype(BF16),
                  w_branch_b[l].astype(BF16), w_out[l].astype(BF16),
                  ffn2_norm[l].reshape(1, d), ffn2_w_in[l].astype(BF16),
                  ffn2_w_out[l].astype(BF16), final_norm.reshape(1, d), tm)
    return x
```

```python
import functools

import jax
import jax.numpy as jnp
from jax import lax
from jax.experimental import pallas as pl
from jax.experimental.pallas import tpu as pltpu

D_MODEL = 1024
CHUNK = 64
D_FF = 2816
FFN_RES_WEIGHT = 0.5
MLA_HEADS = 8
MLA_Q_RANK = 256
MLA_KV_RANK = 128
MLA_NOPE = 64
MLA_ROPE = 32
MLA_V = 64
ROPE_THETA = 10000.0
CA_HEADS = 8
CA_HEAD_DIM = 64
CA_LEFT_CHUNKS = 8
MAX_REL_DIST = 256
CA_WIDTH = CA_HEADS * CA_HEAD_DIM
MLA_OUT_WIDTH = MLA_HEADS * MLA_V
EPS = 1e-6
NEG_INF = -1e30
LOG2_E = 1.4426950408889634

LANES = 128
HEAD_SLAB = LANES
VMEM_LIMIT_BYTES = 56 * 1024 * 1024

ZC_QLAT = 0
ZC_KVLAT = ZC_QLAT + MLA_Q_RANK
ZC_KPE = ZC_KVLAT + MLA_KV_RANK
ZC_CAQ = ZC_KPE + LANES
ZC_CAK = ZC_CAQ + CA_WIDTH
ZC_CAV = ZC_CAK + CA_WIDTH
ZC_GA = ZC_CAV + CA_WIDTH
ZC_GB = ZC_GA + D_MODEL
Z_COLS = ZC_GB + D_MODEL

FFN_CHUNKS = ((0, 1024), (1024, 1024), (2048, 768))

BF16 = jnp.bfloat16
F32 = jnp.float32


def _const_spec(shape):
    nd = len(shape)
    return pl.BlockSpec(shape, lambda *_: (0,) * nd, pipeline_mode=pl.Buffered(1))


def _rmsnorm(x, g):
    return x * lax.rsqrt(jnp.mean(x * x, axis=-1, keepdims=True) + EPS) * g


def _mod_norm(x, g, scale, shift):
    y = (x * lax.rsqrt(jnp.mean(x * x, axis=-1, keepdims=True) + EPS)).astype(BF16)
    return y * (g * (1.0 + scale)).astype(BF16) + shift.astype(BF16)


def _dot(a, b):
    return jnp.dot(a, b, preferred_element_type=F32)


def _dot_nt(a, b):
    return lax.dot_general(a, b, (((1,), (1,)), ((), ())), preferred_element_type=F32)


def _ada_kernel(c_ref, w_ref, b_ref, o_ref):
    c = c_ref[...]
    c_act = c * jax.nn.sigmoid(c)
    o_ref[...] = jnp.dot(c_act, w_ref[...], preferred_element_type=F32,
                         precision=lax.Precision.HIGHEST) + b_ref[...]


def _ada(c, w_ada, b_ada):
    bsz, d = c.shape
    n = w_ada.shape[1]
    bn = 1152
    return pl.pallas_call(
        _ada_kernel,
        grid=(n // bn,),
        in_specs=[pl.BlockSpec((bsz, d), lambda j: (0, 0)),
                  pl.BlockSpec((d, bn), lambda j: (0, j)),
                  pl.BlockSpec((1, bn), lambda j: (0, j))],
        out_specs=pl.BlockSpec((bsz, bn), lambda j: (0, j)),
        out_shape=jax.ShapeDtypeStruct((bsz, n), F32),
        name="ada",
    )(c, w_ada, b_ada.reshape(1, n))


def _ffn_body(x, ada_ref, ada_base, norm_ref, w_in_ref, w_out_ref):
    shift = ada_ref[0, ada_base:ada_base + 1, :]
    scale = ada_ref[0, ada_base + 1:ada_base + 2, :]
    gate = ada_ref[0, ada_base + 2:ada_base + 3, :]
    h = _mod_norm(x, norm_ref[...], scale, shift)
    acc = None
    for c0, cw in FFN_CHUNKS:
        g = _dot(h, w_in_ref[:, c0:c0 + cw])
        u = _dot(h, w_in_ref[:, D_FF + c0:D_FF + c0 + cw])
        a = (g * jax.nn.sigmoid(g) * u).astype(BF16)
        part = _dot(a, w_out_ref[c0:c0 + cw, :])
        acc = part if acc is None else acc + part
    return x + (FFN_RES_WEIGHT * gate) * acc


def _ffn1_kernel(x_ref, ada_ref, norm_ref, w_in_ref, w_out_ref, o_ref):
    o_ref[0] = _ffn_body(x_ref[0], ada_ref, 0, norm_ref, w_in_ref, w_out_ref)


def _ffn1(x, ada, norm, w_in, w_out, tm):
    bsz, s, d = x.shape
    row = pl.BlockSpec((1, tm, d), lambda b, i: (b, i, 0))
    return pl.pallas_call(
        _ffn1_kernel,
        grid=(bsz, s // tm),
        in_specs=[row,
                  pl.BlockSpec((1, 9, d), lambda b, i: (b, 0, 0)),
                  _const_spec((1, d)),
                  _const_spec(w_in.shape),
                  _const_spec(w_out.shape)],
        out_specs=row,
        out_shape=jax.ShapeDtypeStruct(x.shape, F32),
        compiler_params=pltpu.CompilerParams(
            dimension_semantics=("parallel", "parallel"),
            vmem_limit_bytes=VMEM_LIMIT_BYTES),
        name="ffn1",
    )(x, ada, norm, w_in, w_out)


PROJ_SUBTILES = 2


def _rope(v, cos_t, sin_t):
    return v * cos_t + pltpu.roll(v, LANES - MLA_ROPE, axis=1) * sin_t


def _proj_kernel(x_ref, ada_ref, norm_ref, pos_ref, invf_ref, w_in_ref, qn_ref, kvn_ref,
                 wq_ref, wuk_ref, wuvt_ref, vone_ref,
                 q_ref, k_ref, vt_ref, caq_ref, cak_ref, cavt_ref, ga_ref, gb_ref):
    shift = ada_ref[0, 3:4, :]
    scale = ada_ref[0, 4:5, :]
    qk_scale = (MLA_NOPE + MLA_ROPE) ** -0.5 * LOG2_E
    rows = x_ref.shape[1] // PROJ_SUBTILES
    one_rows = jnp.ones((MLA_NOPE, rows), F32)
    zero_head = jnp.zeros((MLA_NOPE, rows), F32)
    zero_tail = jnp.zeros((LANES - MLA_NOPE - MLA_ROPE, rows), F32)
    ones_pad = (lax.broadcasted_iota(jnp.int32, (HEAD_SLAB - CA_HEAD_DIM, rows), 0) == 0).astype(F32)
    for sub in range(PROJ_SUBTILES):
        rs = slice(sub * rows, (sub + 1) * rows)
        h = _mod_norm(x_ref[0, rs, :], norm_ref[...], scale, shift)
        z = _dot_nt(h, w_in_ref[...])

        ang = invf_ref[...] * pos_ref[0, 0, :, rs].astype(F32)
        cos_h, sin_h = jnp.cos(ang), jnp.sin(ang)
        cos_t = jnp.transpose(jnp.concatenate([one_rows, cos_h, cos_h, zero_tail], axis=0))
        sin_t = jnp.transpose(jnp.concatenate([zero_head, sin_h, sin_h, zero_tail], axis=0))

        cq = _rmsnorm(z[:, ZC_QLAT:ZC_QLAT + MLA_Q_RANK], qn_ref[...]).astype(BF16)
        qall = _dot(cq, wq_ref[...])
        ckv = _rmsnorm(z[:, ZC_KVLAT:ZC_KVLAT + MLA_KV_RANK], kvn_ref[...]).astype(BF16)
        kn = _dot(ckv, wuk_ref[...])
        kpe = _rope(z[:, ZC_KPE:ZC_KPE + LANES], cos_t, sin_t)
        for hd in range(MLA_HEADS):
            sl = slice(hd * HEAD_SLAB, (hd + 1) * HEAD_SLAB)
            q_ref[0, rs, sl] = (_rope(qall[:, sl], cos_t, sin_t) * qk_scale).astype(BF16)
            k_ref[0, rs, sl] = (kn[:, sl] + kpe).astype(BF16)
        vt_ref[0, :, rs] = (_dot_nt(wuvt_ref[...], ckv) + vone_ref[...]).astype(BF16)

        caq_ref[0, rs, :] = (z[:, ZC_CAQ:ZC_CAQ + CA_WIDTH]
                             * (CA_HEAD_DIM ** -0.5 * LOG2_E)).astype(BF16)
        cak_ref[0, rs, :] = z[:, ZC_CAK:ZC_CAK + CA_WIDTH].astype(BF16)
        cavt = jnp.transpose(z[:, ZC_CAV:ZC_CAV + CA_WIDTH])
        for hd in range(CA_HEADS):
            slab = jnp.concatenate([cavt[hd * CA_HEAD_DIM:(hd + 1) * CA_HEAD_DIM], ones_pad], axis=0)
            cavt_ref[0, hd * HEAD_SLAB:(hd + 1) * HEAD_SLAB, rs] = slab.astype(BF16)
        ga_ref[0, rs, :] = jax.nn.sigmoid(z[:, ZC_GA:ZC_GA + D_MODEL]).astype(BF16)
        gb_ref[0, rs, :] = jax.nn.sigmoid(z[:, ZC_GB:ZC_GB + D_MODEL]).astype(BF16)


def _proj(x, ada, norm, pos, invf, w_in_ext, qn, kvn, wq, wuk, wuvt, vone, tm):
    bsz, s, d = x.shape

    def row(width):
        return pl.BlockSpec((1, tm, width), lambda b, i: (b, i, 0))

    def out(width):
        return jax.ShapeDtypeStruct((bsz, s, width), BF16)

    slabs = MLA_HEADS * HEAD_SLAB
    widths = (slabs, slabs, None, CA_WIDTH, CA_WIDTH, None, D_MODEL, D_MODEL)
    vt_spec = pl.BlockSpec((1, slabs, tm), lambda b, i: (b, 0, i))
    vt_shape = jax.ShapeDtypeStruct((bsz, slabs, s), BF16)
    return pl.pallas_call(
        _proj_kernel,
        grid=(bsz, s // tm),
        in_specs=[row(d),
                  pl.BlockSpec((1, 9, d), lambda b, i: (b, 0, 0)),
                  _const_spec((1, d)),
                  pl.BlockSpec((1, 1, 1, tm), lambda b, i: (b, i, 0, 0)),
                  _const_spec(invf.shape),
                  _const_spec(w_in_ext.shape),
                  _const_spec(qn.shape), _const_spec(kvn.shape),
                  _const_spec(wq.shape),
                  _const_spec(wuk.shape), _const_spec(wuvt.shape), _const_spec(vone.shape)],
        out_specs=[vt_spec if w is None else row(w) for w in widths],
        out_shape=[vt_shape if w is None else out(w) for w in widths],
        compiler_params=pltpu.CompilerParams(
            dimension_semantics=("parallel", "parallel"),
            vmem_limit_bytes=VMEM_LIMIT_BYTES),
        name="proj",
    )(x, ada, norm, pos, invf, w_in_ext, qn, kvn, wq, wuk, wuvt, vone)


MLA_HEADS_PER_STEP = 8
MLA_QSPLIT = 2
MLA_LOOKAHEAD = 3


def _mla_kernel(q_ref, k_ref, vt_ref, o_ref, m_ref, acc_ref, *s_refs, blk, hps):
    qi = pl.program_id(2)
    m_ref[...] = jnp.full(m_ref.shape, NEG_INF, F32)
    acc_ref[...] = jnp.zeros(acc_ref.shape, F32)

    nq = blk // MLA_QSPLIT
    units = [(hd, qh) for hd in range(hps) for qh in range(MLA_QSPLIT)]

    def step(j, mask):
        k0 = pl.multiple_of(j * blk, blk)

        def n_keys(qh):
            return blk if mask is None else (qh + 1) * nq

        def scores(u):
            hd, qh = units[u]
            sl = slice(hd * HEAD_SLAB, (hd + 1) * HEAD_SLAB)
            qs = slice(qh * nq, (qh + 1) * nq)
            nk = n_keys(qh)
            s = _dot_nt(k_ref[0, pl.ds(k0, nk), sl], q_ref[0, qs, sl])
            if mask is not None:
                s = jnp.where(mask[:nk, qs], s, NEG_INF)
            s_refs[u % len(s_refs)][:nk, :] = s
            return jnp.max(s, axis=0, keepdims=True)

        block_max = {u: scores(u) for u in range(min(MLA_LOOKAHEAD, len(units)))}
        for u, (hd, qh) in enumerate(units):
            if u + MLA_LOOKAHEAD < len(units):
                block_max[u + MLA_LOOKAHEAD] = scores(u + MLA_LOOKAHEAD)
            qs = slice(qh * nq, (qh + 1) * nq)
            nk = n_keys(qh)
            s = s_refs[u % len(s_refs)][:nk, :]
            m_prev = m_ref[hd, :, qs]
            m_new = jnp.maximum(m_prev, block_max.pop(u))
            m_ref[hd, :, qs] = m_new
            alpha = jnp.exp2(m_prev - m_new)
            p = jnp.exp2((s - m_new).astype(BF16))
            vt = vt_ref[0, hd * HEAD_SLAB:(hd + 1) * HEAD_SLAB, pl.ds(k0, nk)]
            acc_ref[hd, :, qs] = acc_ref[hd, :, qs] * alpha + _dot(vt, p)

    def body(j, carry):
        step(j, None)
        return carry

    lax.fori_loop(0, qi, body, 0)
    kc = lax.broadcasted_iota(jnp.int32, (blk, blk), 0) // CHUNK
    qc = lax.broadcasted_iota(jnp.int32, (blk, blk), 1) // CHUNK
    step(qi, kc <= qc)
    for pr in range(hps // 2):
        outs = []
        for hd in (2 * pr, 2 * pr + 1):
            a = acc_ref[hd]
            outs.append(a[:MLA_V] * (1.0 / a[MLA_V:MLA_V + 1]))
        o_ref[0, :, pr * LANES:(pr + 1) * LANES] = jnp.transpose(
            jnp.concatenate(outs, axis=0)).astype(BF16)


def _mla(q, k, vt, blk):
    bsz, s, _ = q.shape
    hps = MLA_HEADS_PER_STEP
    return pl.pallas_call(
        functools.partial(_mla_kernel, blk=blk, hps=hps),
        grid=(bsz, MLA_HEADS // hps, s // blk),
        in_specs=[pl.BlockSpec((1, blk, hps * HEAD_SLAB), lambda b, g, i: (b, i, g)),
                  pl.BlockSpec((1, s, hps * HEAD_SLAB), lambda b, g, i: (b, 0, g)),
                  pl.BlockSpec((1, hps * HEAD_SLAB, s), lambda b, g, i: (b, g, 0))],
        out_specs=pl.BlockSpec((1, blk, hps * MLA_V), lambda b, g, i: (b, i, g)),
        out_shape=jax.ShapeDtypeStruct((bsz, s, MLA_OUT_WIDTH), BF16),
        scratch_shapes=[pltpu.VMEM((hps, 1, blk), F32),
                        pltpu.VMEM((hps, HEAD_SLAB, blk), F32)]
                       + [pltpu.VMEM((blk, blk // MLA_QSPLIT), F32)] * (MLA_LOOKAHEAD + 1),
        compiler_params=pltpu.CompilerParams(
            dimension_semantics=("parallel", "parallel", "arbitrary"),
            vmem_limit_bytes=VMEM_LIMIT_BYTES),
        name="mla",
    )(q, k, vt)


CA_TQ = 256
CA_TK = CA_TQ + CA_LEFT_CHUNKS * CHUNK
CA_KBLKS = CA_TK // CA_TQ
CA_TAB = 1024
CA_LOOKAHEAD = 4


def _ca_kernel(tab_ref, q_ref, k_ref, vt_ref, o_ref, bias_ref, *s_refs):
    b, i = pl.program_id(0), pl.program_id(1)

    @pl.when((b == 0) & (i == 0))
    def _build_bias():
        row = lax.broadcasted_iota(jnp.int32, (CA_TQ, CA_TAB), 0)
        kc = lax.broadcasted_iota(jnp.int32, (CA_TK, CA_TQ), 0) // CHUNK
        qc = lax.broadcasted_iota(jnp.int32, (CA_TK, CA_TQ), 1) // CHUNK
        band = (kc >= qc) & (kc <= qc + CA_LEFT_CHUNKS)
        for hd in range(CA_HEADS):
            t = jnp.broadcast_to(tab_ref[hd:hd + 1, :], (CA_TQ, CA_TAB))
            for bit in range(CA_TQ.bit_length() - 1):
                t = jnp.where(((row >> bit) & 1) == 1, pltpu.roll(t, 1 << bit, axis=1), t)
            bias_ref[hd] = jnp.where(band, jnp.transpose(t[:, :CA_TK]) * LOG2_E, NEG_INF)

    lo = lax.broadcasted_iota(jnp.int32, (1, LANES), 1) < CA_HEAD_DIM

    def attend(nkb):
        nk = nkb * CA_TQ
        k0 = pl.multiple_of((i - (nkb - 1)) * CA_TQ, CA_TQ)

        def scores(hd):
            sl = slice((hd // 2) * LANES, (hd // 2 + 1) * LANES)
            q = q_ref[0, :, sl]
            zero = jnp.zeros_like(q)
            qh = jnp.where(lo, q, zero) if hd % 2 == 0 else jnp.where(lo, zero, q)
            s = _dot_nt(k_ref[0, pl.ds(k0, nk), sl], qh) + bias_ref[hd, CA_TK - nk:, :]
            s_refs[hd % len(s_refs)][:nk, :] = s
            return jnp.max(s, axis=0, keepdims=True)

        col_max = {hd: scores(hd) for hd in range(CA_LOOKAHEAD)}
        outs = []
        for hd in range(CA_HEADS):
            if hd + CA_LOOKAHEAD < CA_HEADS:
                col_max[hd + CA_LOOKAHEAD] = scores(hd + CA_LOOKAHEAD)
            s = s_refs[hd % len(s_refs)][:nk, :]
            p = jnp.exp2((s - col_max.pop(hd)).astype(BF16))
            vt = vt_ref[0, hd * HEAD_SLAB:(hd + 1) * HEAD_SLAB, pl.ds(k0, nk)]
            a = _dot(vt, p)
            outs.append(a[:CA_HEAD_DIM] * (1.0 / a[CA_HEAD_DIM:CA_HEAD_DIM + 1]))
            if hd % 2 == 1:
                pr = hd // 2
                o_ref[0, :, pr * LANES:(pr + 1) * LANES] = jnp.transpose(
                    jnp.concatenate(outs, axis=0)).astype(BF16)
                outs = []

    for nkb in range(1, CA_KBLKS):
        pl.when(i == nkb - 1)(functools.partial(attend, nkb))
    pl.when(i >= CA_KBLKS - 1)(functools.partial(attend, CA_KBLKS))


def _ca(tab, q, k, vt):
    bsz, s, _ = q.shape
    blk = pl.BlockSpec((1, CA_TQ, CA_WIDTH), lambda b, i: (b, i, 0))
    return pl.pallas_call(
        _ca_kernel,
        grid=(bsz, s // CA_TQ),
        in_specs=[_const_spec(tab.shape), blk,
                  pl.BlockSpec((1, s, CA_WIDTH), lambda b, i: (b, 0, 0)),
                  pl.BlockSpec((1, CA_HEADS * HEAD_SLAB, s), lambda b, i: (b, 0, 0))],
        out_specs=blk,
        out_shape=jax.ShapeDtypeStruct((bsz, s, CA_WIDTH), BF16),
        scratch_shapes=[pltpu.VMEM((CA_HEADS, CA_TK, CA_TQ), F32)]
                       + [pltpu.VMEM((CA_TK, CA_TQ), F32)] * (CA_LOOKAHEAD + 1),
        compiler_params=pltpu.CompilerParams(
            dimension_semantics=("arbitrary", "arbitrary"),
            vmem_limit_bytes=VMEM_LIMIT_BYTES),
        name="ca",
    )(tab, q, k, vt)


def _tail_kernel(x_ref, ada_ref, aa_ref, ab_ref, ga_ref, gb_ref, wa_ref, wb_ref, wo_ref,
                 norm_ref, w_in_ref, w_out_ref, fnorm_ref, o_ref):
    ya = _dot(aa_ref[0], wa_ref[...])
    yb = _dot(ab_ref[0], wb_ref[...])
    merged = ga_ref[0] * ya.astype(BF16) + gb_ref[0] * yb.astype(BF16)
    x = x_ref[0] + ada_ref[0, 5:6, :] * _dot(merged, wo_ref[...])
    x = _ffn_body(x, ada_ref, 6, norm_ref, w_in_ref, w_out_ref)
    o_ref[0] = _rmsnorm(x, fnorm_ref[...])


def _tail(x, ada, aa, ab, ga, gb, wa, wb, wo, norm, w_in, w_out, fnorm, tm):
    bsz, s, d = x.shape

    def row(width):
        return pl.BlockSpec((1, tm, width), lambda b, i: (b, i, 0))

    return pl.pallas_call(
        _tail_kernel,
        grid=(bsz, s // tm),
        in_specs=[row(d),
                  pl.BlockSpec((1, 9, d), lambda b, i: (b, 0, 0)),
                  row(MLA_OUT_WIDTH), row(CA_WIDTH), row(d), row(d),
                  _const_spec(wa.shape), _const_spec(wb.shape), _const_spec(wo.shape),
                  _const_spec((1, d)), _const_spec(w_in.shape), _const_spec(w_out.shape),
                  _const_spec((1, d))],
        out_specs=row(d),
        out_shape=jax.ShapeDtypeStruct(x.shape, F32),
        compiler_params=pltpu.CompilerParams(
            dimension_semantics=("parallel", "parallel"),
            vmem_limit_bytes=VMEM_LIMIT_BYTES),
        name="tail",
    )(x, ada, aa, ab, ga, gb, wa, wb, wo, norm, w_in, w_out, fnorm)


def _rotate_half_cols(w):
    half = w.shape[-1] // 2
    return jnp.concatenate([-w[..., half:], w[..., :half]], axis=-1)


def _prep_w_in(w_in):
    d = w_in.shape[0]
    o = 0
    pieces = {}
    for name, width in (("qlat", MLA_Q_RANK), ("kvlat", MLA_KV_RANK), ("kpe", MLA_ROPE),
                        ("caq", CA_WIDTH), ("cak", CA_WIDTH), ("cav", CA_WIDTH),
                        ("ga", D_MODEL), ("gb", D_MODEL)):
        pieces[name] = w_in[:, o:o + width]
        o += width
    zl = jnp.zeros((d, MLA_NOPE), w_in.dtype)
    kpe = jnp.concatenate([zl, pieces["kpe"], _rotate_half_cols(pieces["kpe"])], axis=1)
    ext = jnp.concatenate([pieces["qlat"].T, pieces["kvlat"].T, kpe.T, pieces["caq"].T,
                           pieces["cak"].T, pieces["cav"].T, pieces["ga"].T, pieces["gb"].T], axis=0)
    assert ext.shape[0] == Z_COLS
    return ext.astype(BF16)


def _prep_w_uq(w_uq):
    r = w_uq.shape[0]
    w = w_uq.reshape(r, MLA_HEADS, MLA_NOPE + MLA_ROPE)
    nope, pe = w[..., :MLA_NOPE], w[..., MLA_NOPE:]
    slab = jnp.concatenate([nope, pe, _rotate_half_cols(pe)], axis=-1)
    return slab.reshape(r, MLA_HEADS * HEAD_SLAB).astype(BF16)


def _prep_w_ukv(w_ukv):
    r = w_ukv.shape[0]
    w = w_ukv.reshape(r, MLA_HEADS, MLA_NOPE + MLA_V)
    k_nope, v = w[..., :MLA_NOPE], w[..., MLA_NOPE:]
    zk = jnp.zeros((r, MLA_HEADS, HEAD_SLAB - MLA_NOPE), w.dtype)
    wuk = jnp.concatenate([k_nope, zk], axis=-1).reshape(r, MLA_HEADS * HEAD_SLAB)
    zv = jnp.zeros((r, MLA_HEADS, HEAD_SLAB - MLA_V), w.dtype)
    wuvt = jnp.concatenate([v, zv], axis=-1).reshape(r, MLA_HEADS * HEAD_SLAB).T
    vone = (jnp.arange(MLA_HEADS * HEAD_SLAB) % HEAD_SLAB == MLA_V).astype(F32).reshape(-1, 1)
    return wuk.astype(BF16), wuvt.astype(BF16), vone


def _prep_bias_table(rel_bias):
    j = jnp.arange(CA_TAB)
    t = (j + CA_TQ - 1) % CA_TAB
    rel = (CA_TK - 1) - t
    idx = jnp.clip(rel, -MAX_REL_DIST, MAX_REL_DIST) + MAX_REL_DIST
    return rel_bias[idx].T.astype(F32)


def kernel(x, c, positions, w_ada, b_ada, ffn1_norm, ffn1_w_in, ffn1_w_out, mix_norm, w_in,
           mla_q_norm, mla_w_uq, mla_kv_norm, mla_w_ukv, rel_bias, w_branch_a, w_branch_b,
           w_out, ffn2_norm, ffn2_w_in, ffn2_w_out, final_norm):
    bsz, s, d = x.shape
    tm = 512
    for l in range(w_ada.shape[0]):
        ada = _ada(c, w_ada[l], b_ada[l]).reshape(bsz, 9, d)
        x = _ffn1(x, ada, ffn1_norm[l].reshape(1, d), ffn1_w_in[l].astype(BF16),
                  ffn1_w_out[l].astype(BF16), tm)

        freq = jnp.arange(0, MLA_ROPE, 2, dtype=F32) / MLA_ROPE
        inv_freq = ROPE_THETA ** (-freq)
        invf = inv_freq.reshape(MLA_ROPE // 2, 1)
        wq = _prep_w_uq(mla_w_uq[l])
        wuk, wuvt, vone = _prep_w_ukv(mla_w_ukv[l])
        q, k, vt, caq, cak, cavt, ga, gb = _proj(
            x, ada, mix_norm[l].reshape(1, d), positions.reshape(bsz, s // tm, 1, tm), invf,
            _prep_w_in(w_in[l]), mla_q_norm[l].reshape(1, -1), mla_kv_norm[l].reshape(1, -1),
            wq, wuk, wuvt, vone, tm)
        attn_a = _mla(q, k, vt, 512)
        attn_b = _ca(_prep_bias_table(rel_bias[l]), caq, cak, cavt)
        last = l == w_ada.shape[0] - 1
        assert last, "the fused tail applies the final norm; DEPTH is 1"
        x = _tail(x, ada, attn_a, attn_b, ga, gb, w_branch_a[l].astype(BF16),
                  w_branch_b[l].astype(BF16), w_out[l].astype(BF16),
                  ffn2_norm[l].reshape(1, d), ffn2_w_in[l].astype(BF16),
                  ffn2_w_out[l].astype(BF16), final_norm.reshape(1, d), tm)
    return x
```

```python
import functools

import jax
import jax.numpy as jnp
from jax import lax
from jax.experimental import pallas as pl
from jax.experimental.pallas import tpu as pltpu

D_MODEL = 1024
CHUNK = 64
D_FF = 2816
FFN_RES_WEIGHT = 0.5
MLA_HEADS = 8
MLA_Q_RANK = 256
MLA_KV_RANK = 128
MLA_NOPE = 64
MLA_ROPE = 32
MLA_V = 64
ROPE_THETA = 10000.0
CA_HEADS = 8
CA_HEAD_DIM = 64
CA_LEFT_CHUNKS = 8
MAX_REL_DIST = 256
CA_WIDTH = CA_HEADS * CA_HEAD_DIM
MLA_OUT_WIDTH = MLA_HEADS * MLA_V
EPS = 1e-6
NEG_INF = -1e30
LOG2_E = 1.4426950408889634

LANES = 128
HEAD_SLAB = LANES
VMEM_LIMIT_BYTES = 56 * 1024 * 1024
ROW_TILE = 512
MLA_BLOCK = 512

ZC_QLAT = 0
ZC_KVLAT = ZC_QLAT + MLA_Q_RANK
ZC_KPE = ZC_KVLAT + MLA_KV_RANK
ZC_CAQ = ZC_KPE + LANES
ZC_CAK = ZC_CAQ + CA_WIDTH
ZC_CAV = ZC_CAK + CA_WIDTH
ZC_GA = ZC_CAV + CA_WIDTH
ZC_GB = ZC_GA + D_MODEL
Z_COLS = ZC_GB + D_MODEL

FFN_CHUNKS = tuple((c0, min(512, D_FF - c0)) for c0 in range(0, D_FF, 512))

BF16 = jnp.bfloat16
F32 = jnp.float32


def _const_spec(shape):
    nd = len(shape)
    return pl.BlockSpec(shape, lambda *_: (0,) * nd, pipeline_mode=pl.Buffered(1))


def _rmsnorm(x, g):
    return x * lax.rsqrt(jnp.mean(x * x, axis=-1, keepdims=True) + EPS) * g


def _mod_norm(x, g, scale, shift):
    y = (x * lax.rsqrt(jnp.mean(x * x, axis=-1, keepdims=True) + EPS)).astype(BF16)
    return y * (g * (1.0 + scale)).astype(BF16) + shift.astype(BF16)


def _dot(a, b):
    return jnp.dot(a, b, preferred_element_type=F32)


def _dot_nt(a, b):
    return lax.dot_general(a, b, (((1,), (1,)), ((), ())), preferred_element_type=F32)


def _ada_kernel(c_ref, w_ref, b_ref, o_ref):
    c = c_ref[...]
    c_act = c * jax.nn.sigmoid(c)
    o_ref[...] = jnp.dot(c_act, w_ref[...], preferred_element_type=F32,
                         precision=lax.Precision.HIGHEST) + b_ref[...]


def _ada(c, w_ada, b_ada):
    bsz, d = c.shape
    n = w_ada.shape[1]
    bn = 1152
    return pl.pallas_call(
        _ada_kernel,
        grid=(n // bn,),
        in_specs=[pl.BlockSpec((bsz, d), lambda j: (0, 0)),
                  pl.BlockSpec((d, bn), lambda j: (0, j)),
                  pl.BlockSpec((1, bn), lambda j: (0, j))],
        out_specs=pl.BlockSpec((bsz, bn), lambda j: (0, j)),
        out_shape=jax.ShapeDtypeStruct((bsz, n), F32),
        name="ada",
    )(c, w_ada, b_ada.reshape(1, n))


def _ffn_body(x, ada_ref, ada_base, norm_ref, w_in_ref, w_out_ref):
    shift = ada_ref[0, ada_base:ada_base + 1, :]
    scale = ada_ref[0, ada_base + 1:ada_base + 2, :]
    gate = ada_ref[0, ada_base + 2:ada_base + 3, :]
    h = _mod_norm(x, norm_ref[...], scale, shift)
    acc = None
    for c0, cw in FFN_CHUNKS:
        g = _dot(h, w_in_ref[:, c0:c0 + cw])
        u = _dot(h, w_in_ref[:, D_FF + c0:D_FF + c0 + cw])
        a = (g * jax.nn.sigmoid(g) * u).astype(BF16)
        part = _dot(a, w_out_ref[c0:c0 + cw, :])
        acc = part if acc is None else acc + part
    return x + (FFN_RES_WEIGHT * gate) * acc


def _ffn1_kernel(x_ref, ada_ref, norm_ref, w_in_ref, w_out_ref, o_ref):
    o_ref[0] = _ffn_body(x_ref[0], ada_ref, 0, norm_ref, w_in_ref, w_out_ref)


def _ffn1(x, ada, norm, w_in, w_out, tm):
    bsz, s, d = x.shape
    row = pl.BlockSpec((1, tm, d), lambda b, i: (b, i, 0))
    return pl.pallas_call(
        _ffn1_kernel,
        grid=(bsz, s // tm),
        in_specs=[row,
                  pl.BlockSpec((1, 9, d), lambda b, i: (b, 0, 0)),
                  _const_spec((1, d)),
                  _const_spec(w_in.shape),
                  _const_spec(w_out.shape)],
        out_specs=row,
        out_shape=jax.ShapeDtypeStruct(x.shape, F32),
        compiler_params=pltpu.CompilerParams(
            dimension_semantics=("parallel", "parallel"),
            vmem_limit_bytes=VMEM_LIMIT_BYTES),
        name="ffn1",
    )(x, ada, norm, w_in, w_out)


PROJ_SUBTILES = 2


def _rope(v, cos_t, sin_t):
    return v * cos_t + pltpu.roll(v, LANES - MLA_ROPE, axis=1) * sin_t


def _proj_kernel(x_ref, ada_ref, norm_ref, pos_ref, invf_ref, w_in_ref, qn_ref, kvn_ref,
                 wq_ref, wuk_ref, wuvt_ref, vone_ref,
                 q_ref, k_ref, vt_ref, caq_ref, cak_ref, cavt_ref, ga_ref, gb_ref):
    shift = ada_ref[0, 3:4, :]
    scale = ada_ref[0, 4:5, :]
    qk_scale = (MLA_NOPE + MLA_ROPE) ** -0.5 * LOG2_E
    rows = x_ref.shape[1] // PROJ_SUBTILES
    one_rows = jnp.ones((MLA_NOPE, rows), F32)
    zero_head = jnp.zeros((MLA_NOPE, rows), F32)
    zero_tail = jnp.zeros((LANES - MLA_NOPE - MLA_ROPE, rows), F32)
    ones_pad = (lax.broadcasted_iota(jnp.int32, (HEAD_SLAB - CA_HEAD_DIM, rows), 0) == 0).astype(F32)
    for sub in range(PROJ_SUBTILES):
        rs = slice(sub * rows, (sub + 1) * rows)
        h = _mod_norm(x_ref[0, rs, :], norm_ref[...], scale, shift)
        z = _dot_nt(h, w_in_ref[...])

        ang = invf_ref[...] * pos_ref[0, 0, :, rs].astype(F32)
        cos_h, sin_h = jnp.cos(ang), jnp.sin(ang)
        cos_t = jnp.transpose(jnp.concatenate([one_rows, cos_h, cos_h, zero_tail], axis=0))
        sin_t = jnp.transpose(jnp.concatenate([zero_head, sin_h, sin_h, zero_tail], axis=0))

        cq = _rmsnorm(z[:, ZC_QLAT:ZC_QLAT + MLA_Q_RANK], qn_ref[...]).astype(BF16)
        qall = _dot(cq, wq_ref[...])
        ckv = _rmsnorm(z[:, ZC_KVLAT:ZC_KVLAT + MLA_KV_RANK], kvn_ref[...]).astype(BF16)
        kn = _dot(ckv, wuk_ref[...])
        kpe = _rope(z[:, ZC_KPE:ZC_KPE + LANES], cos_t, sin_t)
        for hd in range(MLA_HEADS):
            sl = slice(hd * HEAD_SLAB, (hd + 1) * HEAD_SLAB)
            q_ref[0, rs, sl] = (_rope(qall[:, sl], cos_t, sin_t) * qk_scale).astype(BF16)
            k_ref[0, rs, sl] = (kn[:, sl] + kpe).astype(BF16)
        vt_ref[0, :, rs] = (_dot_nt(wuvt_ref[...], ckv) + vone_ref[...]).astype(BF16)

        caq_ref[0, rs, :] = (z[:, ZC_CAQ:ZC_CAQ + CA_WIDTH]
                             * (CA_HEAD_DIM ** -0.5 * LOG2_E)).astype(BF16)
        cak_ref[0, rs, :] = z[:, ZC_CAK:ZC_CAK + CA_WIDTH].astype(BF16)
        cavt = jnp.transpose(z[:, ZC_CAV:ZC_CAV + CA_WIDTH])
        for hd in range(CA_HEADS):
            slab = jnp.concatenate([cavt[hd * CA_HEAD_DIM:(hd + 1) * CA_HEAD_DIM], ones_pad], axis=0)
            cavt_ref[0, hd * HEAD_SLAB:(hd + 1) * HEAD_SLAB, rs] = slab.astype(BF16)
        ga_ref[0, rs, :] = jax.nn.sigmoid(z[:, ZC_GA:ZC_GA + D_MODEL]).astype(BF16)
        gb_ref[0, rs, :] = jax.nn.sigmoid(z[:, ZC_GB:ZC_GB + D_MODEL]).astype(BF16)


def _proj(x, ada, norm, pos, invf, w_in_ext, qn, kvn, wq, wuk, wuvt, vone, tm):
    bsz, s, d = x.shape

    def row(width):
        return pl.BlockSpec((1, tm, width), lambda b, i: (b, i, 0))

    def out(width):
        return jax.ShapeDtypeStruct((bsz, s, width), BF16)

    slabs = MLA_HEADS * HEAD_SLAB
    widths = (slabs, slabs, None, CA_WIDTH, CA_WIDTH, None, D_MODEL, D_MODEL)
    vt_spec = pl.BlockSpec((1, slabs, tm), lambda b, i: (b, 0, i))
    vt_shape = jax.ShapeDtypeStruct((bsz, slabs, s), BF16)
    return pl.pallas_call(
        _proj_kernel,
        grid=(bsz, s // tm),
        in_specs=[row(d),
                  pl.BlockSpec((1, 9, d), lambda b, i: (b, 0, 0)),
                  _const_spec((1, d)),
                  pl.BlockSpec((1, 1, 1, tm), lambda b, i: (b, i, 0, 0)),
                  _const_spec(invf.shape),
                  _const_spec(w_in_ext.shape),
                  _const_spec(qn.shape), _const_spec(kvn.shape),
                  _const_spec(wq.shape),
                  _const_spec(wuk.shape), _const_spec(wuvt.shape), _const_spec(vone.shape)],
        out_specs=[vt_spec if w is None else row(w) for w in widths],
        out_shape=[vt_shape if w is None else out(w) for w in widths],
        compiler_params=pltpu.CompilerParams(
            dimension_semantics=("parallel", "parallel"),
            vmem_limit_bytes=VMEM_LIMIT_BYTES),
        name="proj",
    )(x, ada, norm, pos, invf, w_in_ext, qn, kvn, wq, wuk, wuvt, vone)


MLA_HEADS_PER_STEP = 8
MLA_QSPLIT = 2
MLA_LOOKAHEAD = 3


def _mla_kernel(q_ref, k_ref, vt_ref, o_ref, m_ref, acc_ref, *s_refs, blk, hps):
    qi = pl.program_id(2)
    m_ref[...] = jnp.full(m_ref.shape, NEG_INF, F32)
    acc_ref[...] = jnp.zeros(acc_ref.shape, F32)

    nq = blk // MLA_QSPLIT
    units = [(hd, qh) for hd in range(hps) for qh in range(MLA_QSPLIT)]

    def step(j, mask):
        k0 = pl.multiple_of(j * blk, blk)

        def n_keys(qh):
            return blk if mask is None else (qh + 1) * nq

        def scores(u):
            hd, qh = units[u]
            sl = slice(hd * HEAD_SLAB, (hd + 1) * HEAD_SLAB)
            qs = slice(qh * nq, (qh + 1) * nq)
            nk = n_keys(qh)
            s = _dot_nt(k_ref[0, pl.ds(k0, nk), sl], q_ref[0, qs, sl])
            if mask is not None:
                s = jnp.where(mask[:nk, qs], s, NEG_INF)
            s_refs[u % len(s_refs)][:nk, :] = s
            return jnp.max(s, axis=0, keepdims=True)

        block_max = {u: scores(u) for u in range(min(MLA_LOOKAHEAD, len(units)))}
        for u, (hd, qh) in enumerate(units):
            if u + MLA_LOOKAHEAD < len(units):
                block_max[u + MLA_LOOKAHEAD] = scores(u + MLA_LOOKAHEAD)
            qs = slice(qh * nq, (qh + 1) * nq)
            nk = n_keys(qh)
            s = s_refs[u % len(s_refs)][:nk, :]
            m_prev = m_ref[hd, :, qs]
            m_new = jnp.maximum(m_prev, block_max.pop(u))
            m_ref[hd, :, qs] = m_new
            alpha = jnp.exp2(m_prev - m_new)
            p = jnp.exp2((s - m_new).astype(BF16))
            vt = vt_ref[0, hd * HEAD_SLAB:(hd + 1) * HEAD_SLAB, pl.ds(k0, nk)]
            acc_ref[hd, :, qs] = acc_ref[hd, :, qs] * alpha + _dot(vt, p)

    def body(j, carry):
        step(j, None)
        return carry

    lax.fori_loop(0, qi, body, 0)
    kc = lax.broadcasted_iota(jnp.int32, (blk, blk), 0) // CHUNK
    qc = lax.broadcasted_iota(jnp.int32, (blk, blk), 1) // CHUNK
    step(qi, kc <= qc)
    for pr in range(hps // 2):
        outs = []
        for hd in (2 * pr, 2 * pr + 1):
            a = acc_ref[hd]
            outs.append(a[:MLA_V] * (1.0 / a[MLA_V:MLA_V + 1]))
        o_ref[0, :, pr * LANES:(pr + 1) * LANES] = jnp.transpose(
            jnp.concatenate(outs, axis=0)).astype(BF16)


def _mla(q, k, vt, blk):
    bsz, s, _ = q.shape
    hps = MLA_HEADS_PER_STEP
    return pl.pallas_call(
        functools.partial(_mla_kernel, blk=blk, hps=hps),
        grid=(bsz, MLA_HEADS // hps, s // blk),
        in_specs=[pl.BlockSpec((1, blk, hps * HEAD_SLAB), lambda b, g, i: (b, i, g)),
                  pl.BlockSpec((1, s, hps * HEAD_SLAB), lambda b, g, i: (b, 0, g)),
                  pl.BlockSpec((1, hps * HEAD_SLAB, s), lambda b, g, i: (b, g, 0))],
        out_specs=pl.BlockSpec((1, blk, hps * MLA_V), lambda b, g, i: (b, i, g)),
        out_shape=jax.ShapeDtypeStruct((bsz, s, MLA_OUT_WIDTH), BF16),
        scratch_shapes=[pltpu.VMEM((hps, 1, blk), F32),
                        pltpu.VMEM((hps, HEAD_SLAB, blk), F32)]
                       + [pltpu.VMEM((blk, blk // MLA_QSPLIT), F32)] * (MLA_LOOKAHEAD + 1),
        compiler_params=pltpu.CompilerParams(
            dimension_semantics=("parallel", "parallel", "arbitrary"),
            vmem_limit_bytes=VMEM_LIMIT_BYTES),
        name="mla",
    )(q, k, vt)


CA_TQ = 256
CA_TK = CA_TQ + CA_LEFT_CHUNKS * CHUNK
CA_KBLKS = CA_TK // CA_TQ
CA_TAB = 1024
CA_LOOKAHEAD = 4


def _ca_kernel(tab_ref, q_ref, k_ref, vt_ref, o_ref, bias_ref, *s_refs):
    b, i = pl.program_id(0), pl.program_id(1)

    @pl.when((b == 0) & (i == 0))
    def _build_bias():
        row = lax.broadcasted_iota(jnp.int32, (CA_TQ, CA_TAB), 0)
        kc = lax.broadcasted_iota(jnp.int32, (CA_TK, CA_TQ), 0) // CHUNK
        qc = lax.broadcasted_iota(jnp.int32, (CA_TK, CA_TQ), 1) // CHUNK
        band = (kc >= qc) & (kc <= qc + CA_LEFT_CHUNKS)
        for hd in range(CA_HEADS):
            t = jnp.broadcast_to(tab_ref[hd:hd + 1, :], (CA_TQ, CA_TAB))
            for bit in range(CA_TQ.bit_length() - 1):
                t = jnp.where(((row >> bit) & 1) == 1, pltpu.roll(t, 1 << bit, axis=1), t)
            bias_ref[hd] = jnp.where(band, jnp.transpose(t[:, :CA_TK]) * LOG2_E, NEG_INF)

    lo = lax.broadcasted_iota(jnp.int32, (1, LANES), 1) < CA_HEAD_DIM

    def attend(nkb):
        nk = nkb * CA_TQ
        k0 = pl.multiple_of((i - (nkb - 1)) * CA_TQ, CA_TQ)

        def scores(hd):
            sl = slice((hd // 2) * LANES, (hd // 2 + 1) * LANES)
            q = q_ref[0, :, sl]
            zero = jnp.zeros_like(q)
            qh = jnp.where(lo, q, zero) if hd % 2 == 0 else jnp.where(lo, zero, q)
            s = _dot_nt(k_ref[0, pl.ds(k0, nk), sl], qh) + bias_ref[hd, CA_TK - nk:, :]
            s_refs[hd % len(s_refs)][:nk, :] = s
            return jnp.max(s, axis=0, keepdims=True)

        col_max = {hd: scores(hd) for hd in range(CA_LOOKAHEAD)}
        outs = []
        for hd in range(CA_HEADS):
            if hd + CA_LOOKAHEAD < CA_HEADS:
                col_max[hd + CA_LOOKAHEAD] = scores(hd + CA_LOOKAHEAD)
            s = s_refs[hd % len(s_refs)][:nk, :]
            p = jnp.exp2((s - col_max.pop(hd)).astype(BF16))
            vt = vt_ref[0, hd * HEAD_SLAB:(hd + 1) * HEAD_SLAB, pl.ds(k0, nk)]
            a = _dot(vt, p)
            outs.append(a[:CA_HEAD_DIM] * (1.0 / a[CA_HEAD_DIM:CA_HEAD_DIM + 1]))
            if hd % 2 == 1:
                pr = hd // 2
                o_ref[0, :, pr * LANES:(pr + 1) * LANES] = jnp.transpose(
                    jnp.concatenate(outs, axis=0)).astype(BF16)
                outs = []

    for nkb in range(1, CA_KBLKS):
        pl.when(i == nkb - 1)(functools.partial(attend, nkb))
    pl.when(i >= CA_KBLKS - 1)(functools.partial(attend, CA_KBLKS))


def _ca(tab, q, k, vt):
    bsz, s, _ = q.shape
    blk = pl.BlockSpec((1, CA_TQ, CA_WIDTH), lambda b, i: (b, i, 0))
    return pl.pallas_call(
        _ca_kernel,
        grid=(bsz, s // CA_TQ),
        in_specs=[_const_spec(tab.shape), blk,
                  pl.BlockSpec((1, s, CA_WIDTH), lambda b, i: (b, 0, 0)),
                  pl.BlockSpec((1, CA_HEADS * HEAD_SLAB, s), lambda b, i: (b, 0, 0))],
        out_specs=blk,
        out_shape=jax.ShapeDtypeStruct((bsz, s, CA_WIDTH), BF16),
        scratch_shapes=[pltpu.VMEM((CA_HEADS, CA_TK, CA_TQ), F32)]
                       + [pltpu.VMEM((CA_TK, CA_TQ), F32)] * (CA_LOOKAHEAD + 1),
        compiler_params=pltpu.CompilerParams(
            dimension_semantics=("arbitrary", "arbitrary"),
            vmem_limit_bytes=VMEM_LIMIT_BYTES),
        name="ca",
    )(tab, q, k, vt)


def _tail_kernel(x_ref, ada_ref, aa_ref, ab_ref, ga_ref, gb_ref, wa_ref, wb_ref, wo_ref,
                 norm_ref, w_in_ref, w_out_ref, fnorm_ref, o_ref):
    ya = _dot(aa_ref[0], wa_ref[...])
    yb = _dot(ab_ref[0], wb_ref[...])
    merged = ga_ref[0] * ya.astype(BF16) + gb_ref[0] * yb.astype(BF16)
    x = x_ref[0] + ada_ref[0, 5:6, :] * _dot(merged, wo_ref[...])
    x = _ffn_body(x, ada_ref, 6, norm_ref, w_in_ref, w_out_ref)
    o_ref[0] = _rmsnorm(x, fnorm_ref[...])


def _tail(x, ada, aa, ab, ga, gb, wa, wb, wo, norm, w_in, w_out, fnorm, tm):
    bsz, s, d = x.shape

    def row(width):
        return pl.BlockSpec((1, tm, width), lambda b, i: (b, i, 0))

    return pl.pallas_call(
        _tail_kernel,
        grid=(bsz, s // tm),
        in_specs=[row(d),
                  pl.BlockSpec((1, 9, d), lambda b, i: (b, 0, 0)),
                  row(MLA_OUT_WIDTH), row(CA_WIDTH), row(d), row(d),
                  _const_spec(wa.shape), _const_spec(wb.shape), _const_spec(wo.shape),
                  _const_spec((1, d)), _const_spec(w_in.shape), _const_spec(w_out.shape),
                  _const_spec((1, d))],
        out_specs=row(d),
        out_shape=jax.ShapeDtypeStruct(x.shape, F32),
        compiler_params=pltpu.CompilerParams(
            dimension_semantics=("parallel", "parallel"),
            vmem_limit_bytes=VMEM_LIMIT_BYTES),
        name="tail",
    )(x, ada, aa, ab, ga, gb, wa, wb, wo, norm, w_in, w_out, fnorm)


def _rotate_half_cols(w):
    half = w.shape[-1] // 2
    return jnp.concatenate([-w[..., half:], w[..., :half]], axis=-1)


def _prep_w_in(w_in):
    d = w_in.shape[0]
    o = 0
    pieces = {}
    for name, width in (("qlat", MLA_Q_RANK), ("kvlat", MLA_KV_RANK), ("kpe", MLA_ROPE),
                        ("caq", CA_WIDTH), ("cak", CA_WIDTH), ("cav", CA_WIDTH),
                        ("ga", D_MODEL), ("gb", D_MODEL)):
        pieces[name] = w_in[:, o:o + width]
        o += width
    zl = jnp.zeros((d, MLA_NOPE), w_in.dtype)
    kpe = jnp.concatenate([zl, pieces["kpe"], _rotate_half_cols(pieces["kpe"])], axis=1)
    ext = jnp.concatenate([pieces["qlat"].T, pieces["kvlat"].T, kpe.T, pieces["caq"].T,
                           pieces["cak"].T, pieces["cav"].T, pieces["ga"].T, pieces["gb"].T], axis=0)
    assert ext.shape[0] == Z_COLS
    return ext.astype(BF16)


def _prep_w_uq(w_uq):
    r = w_uq.shape[0]
    w = w_uq.reshape(r, MLA_HEADS, MLA_NOPE + MLA_ROPE)
    nope, pe = w[..., :MLA_NOPE], w[..., MLA_NOPE:]
    slab = jnp.concatenate([nope, pe, _rotate_half_cols(pe)], axis=-1)
    return slab.reshape(r, MLA_HEADS * HEAD_SLAB).astype(BF16)


def _prep_w_ukv(w_ukv):
    r = w_ukv.shape[0]
    w = w_ukv.reshape(r, MLA_HEADS, MLA_NOPE + MLA_V)
    k_nope, v = w[..., :MLA_NOPE], w[..., MLA_NOPE:]
    zk = jnp.zeros((r, MLA_HEADS, HEAD_SLAB - MLA_NOPE), w.dtype)
    wuk = jnp.concatenate([k_nope, zk], axis=-1).reshape(r, MLA_HEADS * HEAD_SLAB)
    zv = jnp.zeros((r, MLA_HEADS, HEAD_SLAB - MLA_V), w.dtype)
    wuvt = jnp.concatenate([v, zv], axis=-1).reshape(r, MLA_HEADS * HEAD_SLAB).T
    vone = (jnp.arange(MLA_HEADS * HEAD_SLAB) % HEAD_SLAB == MLA_V).astype(F32).reshape(-1, 1)
    return wuk.astype(BF16), wuvt.astype(BF16), vone


def _prep_bias_table(rel_bias):
    j = jnp.arange(CA_TAB)
    t = (j + CA_TQ - 1) % CA_TAB
    rel = (CA_TK - 1) - t
    idx = jnp.clip(rel, -MAX_REL_DIST, MAX_REL_DIST) + MAX_REL_DIST
    return rel_bias[idx].T.astype(F32)


def kernel(x, c, positions, w_ada, b_ada, ffn1_norm, ffn1_w_in, ffn1_w_out, mix_norm, w_in,
           mla_q_norm, mla_w_uq, mla_kv_norm, mla_w_ukv, rel_bias, w_branch_a, w_branch_b,
           w_out, ffn2_norm, ffn2_w_in, ffn2_w_out, final_norm):
    bsz, s, d = x.shape
    tm = ROW_TILE
    for l in range(w_ada.shape[0]):
        ada = _ada(c, w_ada[l], b_ada[l]).reshape(bsz, 9, d)
        x = _ffn1(x, ada, ffn1_norm[l].reshape(1, d), ffn1_w_in[l].astype(BF16),
                  ffn1_w_out[l].astype(BF16), tm)

        freq = jnp.arange(0, MLA_ROPE, 2, dtype=F32) / MLA_ROPE
        inv_freq = ROPE_THETA ** (-freq)
        invf = inv_freq.reshape(MLA_ROPE // 2, 1)
        wq = _prep_w_uq(mla_w_uq[l])
        wuk, wuvt, vone = _prep_w_ukv(mla_w_ukv[l])
        q, k, vt, caq, cak, cavt, ga, gb = _proj(
            x, ada, mix_norm[l].reshape(1, d), positions.reshape(bsz, s // tm, 1, tm), invf,
            _prep_w_in(w_in[l]), mla_q_norm[l].reshape(1, -1), mla_kv_norm[l].reshape(1, -1),
            wq, wuk, wuvt, vone, tm)
        attn_a = _mla(q, k, vt, MLA_BLOCK)
        attn_b = _ca(_prep_bias_table(rel_bias[l]), caq, cak, cavt)
        last = l == w_ada.shape[0] - 1
        assert last, "the fused tail applies the final norm; DEPTH is 1"
        x = _tail(x, ada, attn_a, attn_b, ga, gb, w_branch_a[l].astype(BF16),
                  w_branch_b[l].astype(BF16), w_out[l].astype(BF16),
                  ffn2_norm[l].reshape(1, d), ffn2_w_in[l].astype(BF16),
                  ffn2_w_out[l].astype(BF16), final_norm.reshape(1, d), tm)
    return x
```

```python
import functools

import jax
import jax.numpy as jnp
from jax import lax
from jax.experimental import pallas as pl
from jax.experimental.pallas import tpu as pltpu

D_MODEL = 1024
CHUNK = 64
D_FF = 2816
FFN_RES_WEIGHT = 0.5
MLA_HEADS = 8
MLA_Q_RANK = 256
MLA_KV_RANK = 128
MLA_NOPE = 64
MLA_ROPE = 32
MLA_V = 64
ROPE_THETA = 10000.0
CA_HEADS = 8
CA_HEAD_DIM = 64
CA_LEFT_CHUNKS = 8
MAX_REL_DIST = 256
CA_WIDTH = CA_HEADS * CA_HEAD_DIM
MLA_OUT_WIDTH = MLA_HEADS * MLA_V
EPS = 1e-6
NEG_INF = -1e30
LOG2_E = 1.4426950408889634

LANES = 128
HEAD_SLAB = LANES
V_SLAB = 80
VMEM_LIMIT_BYTES = 56 * 1024 * 1024
ROW_TILE = 512
MLA_BLOCK = 512

ZC_QLAT = 0
ZC_KVLAT = ZC_QLAT + MLA_Q_RANK
ZC_KPE = ZC_KVLAT + MLA_KV_RANK
ZC_CAQ = ZC_KPE + LANES
ZC_CAK = ZC_CAQ + CA_WIDTH
ZC_CAV = ZC_CAK + CA_WIDTH
ZC_GA = ZC_CAV + CA_WIDTH
ZC_GB = ZC_GA + D_MODEL
Z_COLS = ZC_GB + D_MODEL

FFN_CHUNKS = tuple((c0, min(512, D_FF - c0)) for c0 in range(0, D_FF, 512))

BF16 = jnp.bfloat16
F32 = jnp.float32


def _const_spec(shape):
    nd = len(shape)
    return pl.BlockSpec(shape, lambda *_: (0,) * nd, pipeline_mode=pl.Buffered(1))


def _rmsnorm(x, g):
    return x * lax.rsqrt(jnp.mean(x * x, axis=-1, keepdims=True) + EPS) * g


def _mod_norm(x, g, scale, shift):
    y = (x * lax.rsqrt(jnp.mean(x * x, axis=-1, keepdims=True) + EPS)).astype(BF16)
    return y * (g * (1.0 + scale)).astype(BF16) + shift.astype(BF16)


def _dot(a, b):
    return jnp.dot(a, b, preferred_element_type=F32)


def _dot_nt(a, b):
    return lax.dot_general(a, b, (((1,), (1,)), ((), ())), preferred_element_type=F32)


def _ada_kernel(c_ref, w_ref, b_ref, o_ref):
    c = c_ref[...]
    c_act = c * jax.nn.sigmoid(c)
    o_ref[...] = jnp.dot(c_act, w_ref[...], preferred_element_type=F32,
                         precision=lax.Precision.HIGHEST) + b_ref[...]


def _ada(c, w_ada, b_ada):
    bsz, d = c.shape
    n = w_ada.shape[1]
    bn = 1152
    return pl.pallas_call(
        _ada_kernel,
        grid=(n // bn,),
        in_specs=[pl.BlockSpec((bsz, d), lambda j: (0, 0)),
                  pl.BlockSpec((d, bn), lambda j: (0, j)),
                  pl.BlockSpec((1, bn), lambda j: (0, j))],
        out_specs=pl.BlockSpec((bsz, bn), lambda j: (0, j)),
        out_shape=jax.ShapeDtypeStruct((bsz, n), F32),
        name="ada",
    )(c, w_ada, b_ada.reshape(1, n))


def _ffn_body(x, ada_ref, ada_base, norm_ref, w_in_ref, w_out_ref):
    shift = ada_ref[0, ada_base:ada_base + 1, :]
    scale = ada_ref[0, ada_base + 1:ada_base + 2, :]
    gate = ada_ref[0, ada_base + 2:ada_base + 3, :]
    h = _mod_norm(x, norm_ref[...], scale, shift)
    acc = None
    for c0, cw in FFN_CHUNKS:
        g = _dot(h, w_in_ref[:, c0:c0 + cw])
        u = _dot(h, w_in_ref[:, D_FF + c0:D_FF + c0 + cw])
        a = (g * jax.nn.sigmoid(g) * u).astype(BF16)
        part = _dot(a, w_out_ref[c0:c0 + cw, :])
        acc = part if acc is None else acc + part
    return x + (FFN_RES_WEIGHT * gate) * acc


def _ffn1_kernel(x_ref, ada_ref, norm_ref, w_in_ref, w_out_ref, o_ref):
    o_ref[0] = _ffn_body(x_ref[0], ada_ref, 0, norm_ref, w_in_ref, w_out_ref)


def _ffn1(x, ada, norm, w_in, w_out, tm):
    bsz, s, d = x.shape
    row = pl.BlockSpec((1, tm, d), lambda b, i: (b, i, 0))
    return pl.pallas_call(
        _ffn1_kernel,
        grid=(bsz, s // tm),
        in_specs=[row,
                  pl.BlockSpec((1, 9, d), lambda b, i: (b, 0, 0)),
                  _const_spec((1, d)),
                  _const_spec(w_in.shape),
                  _const_spec(w_out.shape)],
        out_specs=row,
        out_shape=jax.ShapeDtypeStruct(x.shape, F32),
        compiler_params=pltpu.CompilerParams(
            dimension_semantics=("parallel", "parallel"),
            vmem_limit_bytes=VMEM_LIMIT_BYTES),
        name="ffn1",
    )(x, ada, norm, w_in, w_out)


PROJ_SUBTILES = 2


def _rope(v, cos_t, sin_t):
    return v * cos_t + pltpu.roll(v, LANES - MLA_ROPE, axis=1) * sin_t


def _proj_kernel(x_ref, ada_ref, norm_ref, pos_ref, invf_ref, w_in_ref, qn_ref, kvn_ref,
                 wq_ref, wuk_ref, wuvt_ref, vone_ref,
                 q_ref, k_ref, vt_ref, caq_ref, cak_ref, cavt_ref, ga_ref, gb_ref):
    shift = ada_ref[0, 3:4, :]
    scale = ada_ref[0, 4:5, :]
    qk_scale = (MLA_NOPE + MLA_ROPE) ** -0.5 * LOG2_E
    rows = x_ref.shape[1] // PROJ_SUBTILES
    one_rows = jnp.ones((MLA_NOPE, rows), F32)
    zero_head = jnp.zeros((MLA_NOPE, rows), F32)
    zero_tail = jnp.zeros((LANES - MLA_NOPE - MLA_ROPE, rows), F32)
    ones_pad = (lax.broadcasted_iota(jnp.int32, (V_SLAB - CA_HEAD_DIM, rows), 0) == 0).astype(F32)
    for sub in range(PROJ_SUBTILES):
        rs = slice(sub * rows, (sub + 1) * rows)
        h = _mod_norm(x_ref[0, rs, :], norm_ref[...], scale, shift)
        z = _dot_nt(h, w_in_ref[...])

        ang = invf_ref[...] * pos_ref[0, 0, :, rs].astype(F32)
        cos_h, sin_h = jnp.cos(ang), jnp.sin(ang)
        cos_t = jnp.transpose(jnp.concatenate([one_rows, cos_h, cos_h, zero_tail], axis=0))
        sin_t = jnp.transpose(jnp.concatenate([zero_head, sin_h, sin_h, zero_tail], axis=0))

        cq = _rmsnorm(z[:, ZC_QLAT:ZC_QLAT + MLA_Q_RANK], qn_ref[...]).astype(BF16)
        qall = _dot(cq, wq_ref[...])
        ckv = _rmsnorm(z[:, ZC_KVLAT:ZC_KVLAT + MLA_KV_RANK], kvn_ref[...]).astype(BF16)
        kn = _dot(ckv, wuk_ref[...])
        kpe = _rope(z[:, ZC_KPE:ZC_KPE + LANES], cos_t, sin_t)
        for hd in range(MLA_HEADS):
            sl = slice(hd * HEAD_SLAB, (hd + 1) * HEAD_SLAB)
            q_ref[0, rs, sl] = (_rope(qall[:, sl], cos_t, sin_t) * qk_scale).astype(BF16)
            k_ref[0, rs, sl] = (kn[:, sl] + kpe).astype(BF16)
        vt_ref[0, :, rs] = (_dot_nt(wuvt_ref[...], ckv) + vone_ref[...]).astype(BF16)

        caq_ref[0, rs, :] = (z[:, ZC_CAQ:ZC_CAQ + CA_WIDTH]
                             * (CA_HEAD_DIM ** -0.5 * LOG2_E)).astype(BF16)
        cak_ref[0, rs, :] = z[:, ZC_CAK:ZC_CAK + CA_WIDTH].astype(BF16)
        cavt = jnp.transpose(z[:, ZC_CAV:ZC_CAV + CA_WIDTH])
        for hd in range(CA_HEADS):
            slab = jnp.concatenate([cavt[hd * CA_HEAD_DIM:(hd + 1) * CA_HEAD_DIM], ones_pad], axis=0)
            cavt_ref[0, hd * V_SLAB:(hd + 1) * V_SLAB, rs] = slab.astype(BF16)
        ga_ref[0, rs, :] = jax.nn.sigmoid(z[:, ZC_GA:ZC_GA + D_MODEL]).astype(BF16)
        gb_ref[0, rs, :] = jax.nn.sigmoid(z[:, ZC_GB:ZC_GB + D_MODEL]).astype(BF16)


def _proj(x, ada, norm, pos, invf, w_in_ext, qn, kvn, wq, wuk, wuvt, vone, tm):
    bsz, s, d = x.shape

    def row(width):
        return pl.BlockSpec((1, tm, width), lambda b, i: (b, i, 0))

    def out(width):
        return jax.ShapeDtypeStruct((bsz, s, width), BF16)

    slabs = MLA_HEADS * HEAD_SLAB
    widths = (slabs, slabs, None, CA_WIDTH, CA_WIDTH, None, D_MODEL, D_MODEL)
    vt_rows = MLA_HEADS * V_SLAB
    vt_spec = pl.BlockSpec((1, vt_rows, tm), lambda b, i: (b, 0, i))
    vt_shape = jax.ShapeDtypeStruct((bsz, vt_rows, s), BF16)
    return pl.pallas_call(
        _proj_kernel,
        grid=(bsz, s // tm),
        in_specs=[row(d),
                  pl.BlockSpec((1, 9, d), lambda b, i: (b, 0, 0)),
                  _const_spec((1, d)),
                  pl.BlockSpec((1, 1, 1, tm), lambda b, i: (b, i, 0, 0)),
                  _const_spec(invf.shape),
                  _const_spec(w_in_ext.shape),
                  _const_spec(qn.shape), _const_spec(kvn.shape),
                  _const_spec(wq.shape),
                  _const_spec(wuk.shape), _const_spec(wuvt.shape), _const_spec(vone.shape)],
        out_specs=[vt_spec if w is None else row(w) for w in widths],
        out_shape=[vt_shape if w is None else out(w) for w in widths],
        compiler_params=pltpu.CompilerParams(
            dimension_semantics=("parallel", "parallel"),
            vmem_limit_bytes=VMEM_LIMIT_BYTES),
        name="proj",
    )(x, ada, norm, pos, invf, w_in_ext, qn, kvn, wq, wuk, wuvt, vone)


MLA_HEADS_PER_STEP = 8
MLA_QSPLIT = 2
MLA_LOOKAHEAD = 3


def _mla_kernel(q_ref, k_ref, vt_ref, o_ref, m_ref, acc_ref, *s_refs, blk, hps):
    qi = pl.program_id(2)
    m_ref[...] = jnp.full(m_ref.shape, NEG_INF, F32)
    acc_ref[...] = jnp.zeros(acc_ref.shape, F32)

    nq = blk // MLA_QSPLIT
    units = [(hd, qh) for hd in range(hps) for qh in range(MLA_QSPLIT)]

    def step(j, mask):
        k0 = pl.multiple_of(j * blk, blk)

        def n_keys(qh):
            return blk if mask is None else (qh + 1) * nq

        def scores(u):
            hd, qh = units[u]
            sl = slice(hd * HEAD_SLAB, (hd + 1) * HEAD_SLAB)
            qs = slice(qh * nq, (qh + 1) * nq)
            nk = n_keys(qh)
            s = _dot_nt(k_ref[0, pl.ds(k0, nk), sl], q_ref[0, qs, sl])
            if mask is not None:
                s = jnp.where(mask[:nk, qs], s, NEG_INF)
            s_refs[u % len(s_refs)][:nk, :] = s
            return jnp.max(s, axis=0, keepdims=True)

        block_max = {u: scores(u) for u in range(min(MLA_LOOKAHEAD, len(units)))}
        for u, (hd, qh) in enumerate(units):
            if u + MLA_LOOKAHEAD < len(units):
                block_max[u + MLA_LOOKAHEAD] = scores(u + MLA_LOOKAHEAD)
            qs = slice(qh * nq, (qh + 1) * nq)
            nk = n_keys(qh)
            s = s_refs[u % len(s_refs)][:nk, :]
            m_prev = m_ref[hd, :, qs]
            m_new = jnp.maximum(m_prev, block_max.pop(u))
            m_ref[hd, :, qs] = m_new
            alpha = jnp.exp2(m_prev - m_new)
            p = jnp.exp2((s - m_new).astype(BF16))
            vt = vt_ref[0, hd * V_SLAB:(hd + 1) * V_SLAB, pl.ds(k0, nk)]
            acc_ref[hd, :, qs] = acc_ref[hd, :, qs] * alpha + _dot(vt, p)

    def body(j, carry):
        step(j, None)
        return carry

    lax.fori_loop(0, qi, body, 0)
    kc = lax.broadcasted_iota(jnp.int32, (blk, blk), 0) // CHUNK
    qc = lax.broadcasted_iota(jnp.int32, (blk, blk), 1) // CHUNK
    step(qi, kc <= qc)
    for pr in range(hps // 2):
        outs = []
        for hd in (2 * pr, 2 * pr + 1):
            a = acc_ref[hd]
            outs.append(a[:MLA_V] * (1.0 / a[MLA_V:MLA_V + 1]))
        o_ref[0, :, pr * LANES:(pr + 1) * LANES] = jnp.transpose(
            jnp.concatenate(outs, axis=0)).astype(BF16)


def _mla(q, k, vt, blk):
    bsz, s, _ = q.shape
    hps = MLA_HEADS_PER_STEP
    return pl.pallas_call(
        functools.partial(_mla_kernel, blk=blk, hps=hps),
        grid=(bsz, MLA_HEADS // hps, s // blk),
        in_specs=[pl.BlockSpec((1, blk, hps * HEAD_SLAB), lambda b, g, i: (b, i, g)),
                  pl.BlockSpec((1, s, hps * HEAD_SLAB), lambda b, g, i: (b, 0, g)),
                  pl.BlockSpec((1, hps * V_SLAB, s), lambda b, g, i: (b, g, 0))],
        out_specs=pl.BlockSpec((1, blk, hps * MLA_V), lambda b, g, i: (b, i, g)),
        out_shape=jax.ShapeDtypeStruct((bsz, s, MLA_OUT_WIDTH), BF16),
        scratch_shapes=[pltpu.VMEM((hps, 1, blk), F32),
                        pltpu.VMEM((hps, V_SLAB, blk), F32)]
                       + [pltpu.VMEM((blk, blk // MLA_QSPLIT), F32)] * (MLA_LOOKAHEAD + 1),
        compiler_params=pltpu.CompilerParams(
            dimension_semantics=("parallel", "parallel", "arbitrary"),
            vmem_limit_bytes=VMEM_LIMIT_BYTES),
        name="mla",
    )(q, k, vt)


CA_TQ = 256
CA_TK = CA_TQ + CA_LEFT_CHUNKS * CHUNK
CA_KBLKS = CA_TK // CA_TQ
CA_TAB = 1024
CA_LOOKAHEAD = 4


def _ca_kernel(tab_ref, q_ref, k_ref, vt_ref, o_ref, bias_ref, *s_refs):
    b, i = pl.program_id(0), pl.program_id(1)

    @pl.when((b == 0) & (i == 0))
    def _build_bias():
        row = lax.broadcasted_iota(jnp.int32, (CA_TQ, CA_TAB), 0)
        kc = lax.broadcasted_iota(jnp.int32, (CA_TK, CA_TQ), 0) // CHUNK
        qc = lax.broadcasted_iota(jnp.int32, (CA_TK, CA_TQ), 1) // CHUNK
        band = (kc >= qc) & (kc <= qc + CA_LEFT_CHUNKS)
        for hd in range(CA_HEADS):
            t = jnp.broadcast_to(tab_ref[hd:hd + 1, :], (CA_TQ, CA_TAB))
            for bit in range(CA_TQ.bit_length() - 1):
                t = jnp.where(((row >> bit) & 1) == 1, pltpu.roll(t, 1 << bit, axis=1), t)
            bias_ref[hd] = jnp.where(band, jnp.transpose(t[:, :CA_TK]) * LOG2_E, NEG_INF)

    lo = lax.broadcasted_iota(jnp.int32, (1, LANES), 1) < CA_HEAD_DIM

    def attend(nkb):
        nk = nkb * CA_TQ
        k0 = pl.multiple_of((i - (nkb - 1)) * CA_TQ, CA_TQ)

        def scores(hd):
            sl = slice((hd // 2) * LANES, (hd // 2 + 1) * LANES)
            q = q_ref[0, :, sl]
            zero = jnp.zeros_like(q)
            qh = jnp.where(lo, q, zero) if hd % 2 == 0 else jnp.where(lo, zero, q)
            s = _dot_nt(k_ref[0, pl.ds(k0, nk), sl], qh) + bias_ref[hd, CA_TK - nk:, :]
            s_refs[hd % len(s_refs)][:nk, :] = s
            return jnp.max(s, axis=0, keepdims=True)

        col_max = {hd: scores(hd) for hd in range(CA_LOOKAHEAD)}
        outs = []
        for hd in range(CA_HEADS):
            if hd + CA_LOOKAHEAD < CA_HEADS:
                col_max[hd + CA_LOOKAHEAD] = scores(hd + CA_LOOKAHEAD)
            s = s_refs[hd % len(s_refs)][:nk, :]
            p = jnp.exp2((s - col_max.pop(hd)).astype(BF16))
            vt = vt_ref[0, hd * V_SLAB:(hd + 1) * V_SLAB, pl.ds(k0, nk)]
            a = _dot(vt, p)
            outs.append(a[:CA_HEAD_DIM] * (1.0 / a[CA_HEAD_DIM:CA_HEAD_DIM + 1]))
            if hd % 2 == 1:
                pr = hd // 2
                o_ref[0, :, pr * LANES:(pr + 1) * LANES] = jnp.transpose(
                    jnp.concatenate(outs, axis=0)).astype(BF16)
                outs = []

    for nkb in range(1, CA_KBLKS):
        pl.when(i == nkb - 1)(functools.partial(attend, nkb))
    pl.when(i >= CA_KBLKS - 1)(functools.partial(attend, CA_KBLKS))


def _ca(tab, q, k, vt):
    bsz, s, _ = q.shape
    blk = pl.BlockSpec((1, CA_TQ, CA_WIDTH), lambda b, i: (b, i, 0))
    return pl.pallas_call(
        _ca_kernel,
        grid=(bsz, s // CA_TQ),
        in_specs=[_const_spec(tab.shape), blk,
                  pl.BlockSpec((1, s, CA_WIDTH), lambda b, i: (b, 0, 0)),
                  pl.BlockSpec((1, CA_HEADS * V_SLAB, s), lambda b, i: (b, 0, 0))],
        out_specs=blk,
        out_shape=jax.ShapeDtypeStruct((bsz, s, CA_WIDTH), BF16),
        scratch_shapes=[pltpu.VMEM((CA_HEADS, CA_TK, CA_TQ), F32)]
                       + [pltpu.VMEM((CA_TK, CA_TQ), F32)] * (CA_LOOKAHEAD + 1),
        compiler_params=pltpu.CompilerParams(
            dimension_semantics=("arbitrary", "arbitrary"),
            vmem_limit_bytes=VMEM_LIMIT_BYTES),
        name="ca",
    )(tab, q, k, vt)


def _tail_kernel(x_ref, ada_ref, aa_ref, ab_ref, ga_ref, gb_ref, wa_ref, wb_ref, wo_ref,
                 norm_ref, w_in_ref, w_out_ref, fnorm_ref, o_ref):
    ya = _dot(aa_ref[0], wa_ref[...])
    yb = _dot(ab_ref[0], wb_ref[...])
    merged = ga_ref[0] * ya.astype(BF16) + gb_ref[0] * yb.astype(BF16)
    x = x_ref[0] + ada_ref[0, 5:6, :] * _dot(merged, wo_ref[...])
    x = _ffn_body(x, ada_ref, 6, norm_ref, w_in_ref, w_out_ref)
    o_ref[0] = _rmsnorm(x, fnorm_ref[...])


def _tail(x, ada, aa, ab, ga, gb, wa, wb, wo, norm, w_in, w_out, fnorm, tm):
    bsz, s, d = x.shape

    def row(width):
        return pl.BlockSpec((1, tm, width), lambda b, i: (b, i, 0))

    return pl.pallas_call(
        _tail_kernel,
        grid=(bsz, s // tm),
        in_specs=[row(d),
                  pl.BlockSpec((1, 9, d), lambda b, i: (b, 0, 0)),
                  row(MLA_OUT_WIDTH), row(CA_WIDTH), row(d), row(d),
                  _const_spec(wa.shape), _const_spec(wb.shape), _const_spec(wo.shape),
                  _const_spec((1, d)), _const_spec(w_in.shape), _const_spec(w_out.shape),
                  _const_spec((1, d))],
        out_specs=row(d),
        out_shape=jax.ShapeDtypeStruct(x.shape, F32),
        compiler_params=pltpu.CompilerParams(
            dimension_semantics=("parallel", "parallel"),
            vmem_limit_bytes=VMEM_LIMIT_BYTES),
        name="tail",
    )(x, ada, aa, ab, ga, gb, wa, wb, wo, norm, w_in, w_out, fnorm)


def _rotate_half_cols(w):
    half = w.shape[-1] // 2
    return jnp.concatenate([-w[..., half:], w[..., :half]], axis=-1)


def _prep_w_in(w_in):
    d = w_in.shape[0]
    o = 0
    pieces = {}
    for name, width in (("qlat", MLA_Q_RANK), ("kvlat", MLA_KV_RANK), ("kpe", MLA_ROPE),
                        ("caq", CA_WIDTH), ("cak", CA_WIDTH), ("cav", CA_WIDTH),
                        ("ga", D_MODEL), ("gb", D_MODEL)):
        pieces[name] = w_in[:, o:o + width]
        o += width
    zl = jnp.zeros((d, MLA_NOPE), w_in.dtype)
    kpe = jnp.concatenate([zl, pieces["kpe"], _rotate_half_cols(pieces["kpe"])], axis=1)
    ext = jnp.concatenate([pieces["qlat"].T, pieces["kvlat"].T, kpe.T, pieces["caq"].T,
                           pieces["cak"].T, pieces["cav"].T, pieces["ga"].T, pieces["gb"].T], axis=0)
    assert ext.shape[0] == Z_COLS
    return ext.astype(BF16)


def _prep_w_uq(w_uq):
    r = w_uq.shape[0]
    w = w_uq.reshape(r, MLA_HEADS, MLA_NOPE + MLA_ROPE)
    nope, pe = w[..., :MLA_NOPE], w[..., MLA_NOPE:]
    slab = jnp.concatenate([nope, pe, _rotate_half_cols(pe)], axis=-1)
    return slab.reshape(r, MLA_HEADS * HEAD_SLAB).astype(BF16)


def _prep_w_ukv(w_ukv):
    r = w_ukv.shape[0]
    w = w_ukv.reshape(r, MLA_HEADS, MLA_NOPE + MLA_V)
    k_nope, v = w[..., :MLA_NOPE], w[..., MLA_NOPE:]
    zk = jnp.zeros((r, MLA_HEADS, HEAD_SLAB - MLA_NOPE), w.dtype)
    wuk = jnp.concatenate([k_nope, zk], axis=-1).reshape(r, MLA_HEADS * HEAD_SLAB)
    zv = jnp.zeros((r, MLA_HEADS, V_SLAB - MLA_V), w.dtype)
    wuvt = jnp.concatenate([v, zv], axis=-1).reshape(r, MLA_HEADS * V_SLAB).T
    vone = (jnp.arange(MLA_HEADS * V_SLAB) % V_SLAB == MLA_V).astype(F32).reshape(-1, 1)
    return wuk.astype(BF16), wuvt.astype(BF16), vone


def _prep_bias_table(rel_bias):
    j = jnp.arange(CA_TAB)
    t = (j + CA_TQ - 1) % CA_TAB
    rel = (CA_TK - 1) - t
    idx = jnp.clip(rel, -MAX_REL_DIST, MAX_REL_DIST) + MAX_REL_DIST
    return rel_bias[idx].T.astype(F32)


def kernel(x, c, positions, w_ada, b_ada, ffn1_norm, ffn1_w_in, ffn1_w_out, mix_norm, w_in,
           mla_q_norm, mla_w_uq, mla_kv_norm, mla_w_ukv, rel_bias, w_branch_a, w_branch_b,
           w_out, ffn2_norm, ffn2_w_in, ffn2_w_out, final_norm):
    bsz, s, d = x.shape
    tm = ROW_TILE
    for l in range(w_ada.shape[0]):
        ada = _ada(c, w_ada[l], b_ada[l]).reshape(bsz, 9, d)
        x = _ffn1(x, ada, ffn1_norm[l].reshape(1, d), ffn1_w_in[l].astype(BF16),
                  ffn1_w_out[l].astype(BF16), tm)

        freq = jnp.arange(0, MLA_ROPE, 2, dtype=F32) / MLA_ROPE
        inv_freq = ROPE_THETA ** (-freq)
        invf = inv_freq.reshape(MLA_ROPE // 2, 1)
        wq = _prep_w_uq(mla_w_uq[l])
        wuk, wuvt, vone = _prep_w_ukv(mla_w_ukv[l])
        q, k, vt, caq, cak, cavt, ga, gb = _proj(
            x, ada, mix_norm[l].reshape(1, d), positions.reshape(bsz, s // tm, 1, tm), invf,
            _prep_w_in(w_in[l]), mla_q_norm[l].reshape(1, -1), mla_kv_norm[l].reshape(1, -1),
            wq, wuk, wuvt, vone, tm)
        attn_a = _mla(q, k, vt, MLA_BLOCK)
        attn_b = _ca(_prep_bias_table(rel_bias[l]), caq, cak, cavt)
        last = l == w_ada.shape[0] - 1
        assert last, "the fused tail applies the final norm; DEPTH is 1"
        x = _tail(x, ada, attn_a, attn_b, ga, gb, w_branch_a[l].astype(BF16),
                  w_branch_b[l].astype(BF16), w_out[l].astype(BF16),
                  ffn2_norm[l].reshape(1, d), ffn2_w_in[l].astype(BF16),
                  ffn2_w_out[l].astype(BF16), final_norm.reshape(1, d), tm)
    return x
```

```python
import functools

import jax
import jax.numpy as jnp
from jax import lax
from jax.experimental import pallas as pl
from jax.experimental.pallas import tpu as pltpu

D_MODEL = 1024
CHUNK = 64
D_FF = 2816
FFN_RES_WEIGHT = 0.5
MLA_HEADS = 8
MLA_Q_RANK = 256
MLA_KV_RANK = 128
MLA_NOPE = 64
MLA_ROPE = 32
MLA_V = 64
ROPE_THETA = 10000.0
CA_HEADS = 8
CA_HEAD_DIM = 64
CA_LEFT_CHUNKS = 8
MAX_REL_DIST = 256
CA_WIDTH = CA_HEADS * CA_HEAD_DIM
MLA_OUT_WIDTH = MLA_HEADS * MLA_V
EPS = 1e-6
NEG_INF = -1e30
LOG2_E = 1.4426950408889634

LANES = 128
HEAD_SLAB = LANES
V_SLAB = 80
MLA_V_SLAB = LANES
VMEM_LIMIT_BYTES = 56 * 1024 * 1024
ROW_TILE = 512
MLA_BLOCK = 512

ZC_QLAT = 0
ZC_KVLAT = ZC_QLAT + MLA_Q_RANK
ZC_KPE = ZC_KVLAT + MLA_KV_RANK
ZC_CAQ = ZC_KPE + LANES
ZC_CAK = ZC_CAQ + CA_WIDTH
ZC_CAV = ZC_CAK + CA_WIDTH
ZC_GA = ZC_CAV + CA_WIDTH
ZC_GB = ZC_GA + D_MODEL
Z_COLS = ZC_GB + D_MODEL

FFN_CHUNKS = tuple((c0, min(512, D_FF - c0)) for c0 in range(0, D_FF, 512))

BF16 = jnp.bfloat16
F32 = jnp.float32


def _const_spec(shape):
    nd = len(shape)
    return pl.BlockSpec(shape, lambda *_: (0,) * nd, pipeline_mode=pl.Buffered(1))


def _rmsnorm(x, g):
    return x * lax.rsqrt(jnp.mean(x * x, axis=-1, keepdims=True) + EPS) * g


def _mod_norm(x, g, scale, shift):
    y = (x * lax.rsqrt(jnp.mean(x * x, axis=-1, keepdims=True) + EPS)).astype(BF16)
    return y * (g * (1.0 + scale)).astype(BF16) + shift.astype(BF16)


def _dot(a, b):
    return jnp.dot(a, b, preferred_element_type=F32)


def _dot_nt(a, b):
    return lax.dot_general(a, b, (((1,), (1,)), ((), ())), preferred_element_type=F32)


def _ada_kernel(c_ref, w_ref, b_ref, o_ref):
    c = c_ref[...]
    c_act = c * jax.nn.sigmoid(c)
    o_ref[...] = jnp.dot(c_act, w_ref[...], preferred_element_type=F32,
                         precision=lax.Precision.HIGHEST) + b_ref[...]


def _ada(c, w_ada, b_ada):
    bsz, d = c.shape
    n = w_ada.shape[1]
    bn = 1152
    return pl.pallas_call(
        _ada_kernel,
        grid=(n // bn,),
        in_specs=[pl.BlockSpec((bsz, d), lambda j: (0, 0)),
                  pl.BlockSpec((d, bn), lambda j: (0, j)),
                  pl.BlockSpec((1, bn), lambda j: (0, j))],
        out_specs=pl.BlockSpec((bsz, bn), lambda j: (0, j)),
        out_shape=jax.ShapeDtypeStruct((bsz, n), F32),
        name="ada",
    )(c, w_ada, b_ada.reshape(1, n))


def _ffn_body(x, ada_ref, ada_base, norm_ref, w_in_ref, w_out_ref):
    shift = ada_ref[0, ada_base:ada_base + 1, :]
    scale = ada_ref[0, ada_base + 1:ada_base + 2, :]
    gate = ada_ref[0, ada_base + 2:ada_base + 3, :]
    h = _mod_norm(x, norm_ref[...], scale, shift)
    acc = None
    for c0, cw in FFN_CHUNKS:
        g = _dot(h, w_in_ref[:, c0:c0 + cw])
        u = _dot(h, w_in_ref[:, D_FF + c0:D_FF + c0 + cw])
        a = (g * jax.nn.sigmoid(g) * u).astype(BF16)
        part = _dot(a, w_out_ref[c0:c0 + cw, :])
        acc = part if acc is None else acc + part
    return x + (FFN_RES_WEIGHT * gate) * acc


def _ffn1_kernel(x_ref, ada_ref, norm_ref, w_in_ref, w_out_ref, o_ref):
    o_ref[0] = _ffn_body(x_ref[0], ada_ref, 0, norm_ref, w_in_ref, w_out_ref)


def _ffn1(x, ada, norm, w_in, w_out, tm):
    bsz, s, d = x.shape
    row = pl.BlockSpec((1, tm, d), lambda b, i: (b, i, 0))
    return pl.pallas_call(
        _ffn1_kernel,
        grid=(bsz, s // tm),
        in_specs=[row,
                  pl.BlockSpec((1, 9, d), lambda b, i: (b, 0, 0)),
                  _const_spec((1, d)),
                  _const_spec(w_in.shape),
                  _const_spec(w_out.shape)],
        out_specs=row,
        out_shape=jax.ShapeDtypeStruct(x.shape, F32),
        compiler_params=pltpu.CompilerParams(
            dimension_semantics=("parallel", "parallel"),
            vmem_limit_bytes=VMEM_LIMIT_BYTES),
        name="ffn1",
    )(x, ada, norm, w_in, w_out)


PROJ_SUBTILES = 2


def _rope(v, cos_t, sin_t):
    return v * cos_t + pltpu.roll(v, LANES - MLA_ROPE, axis=1) * sin_t


def _proj_kernel(x_ref, ada_ref, norm_ref, pos_ref, invf_ref, w_in_ref, qn_ref, kvn_ref,
                 wq_ref, wuk_ref, wuvt_ref, vone_ref,
                 q_ref, k_ref, vt_ref, caq_ref, cak_ref, cavt_ref, ga_ref, gb_ref):
    shift = ada_ref[0, 3:4, :]
    scale = ada_ref[0, 4:5, :]
    qk_scale = (MLA_NOPE + MLA_ROPE) ** -0.5 * LOG2_E
    rows = x_ref.shape[1] // PROJ_SUBTILES
    one_rows = jnp.ones((MLA_NOPE, rows), F32)
    zero_head = jnp.zeros((MLA_NOPE, rows), F32)
    zero_tail = jnp.zeros((LANES - MLA_NOPE - MLA_ROPE, rows), F32)
    ones_pad = (lax.broadcasted_iota(jnp.int32, (V_SLAB - CA_HEAD_DIM, rows), 0) == 0).astype(F32)
    for sub in range(PROJ_SUBTILES):
        rs = slice(sub * rows, (sub + 1) * rows)
        h = _mod_norm(x_ref[0, rs, :], norm_ref[...], scale, shift)
        z = _dot_nt(h, w_in_ref[...])

        ang = invf_ref[...] * pos_ref[0, 0, :, rs].astype(F32)
        cos_h, sin_h = jnp.cos(ang), jnp.sin(ang)
        cos_t = jnp.transpose(jnp.concatenate([one_rows, cos_h, cos_h, zero_tail], axis=0))
        sin_t = jnp.transpose(jnp.concatenate([zero_head, sin_h, sin_h, zero_tail], axis=0))

        cq = _rmsnorm(z[:, ZC_QLAT:ZC_QLAT + MLA_Q_RANK], qn_ref[...]).astype(BF16)
        qall = _dot(cq, wq_ref[...])
        ckv = _rmsnorm(z[:, ZC_KVLAT:ZC_KVLAT + MLA_KV_RANK], kvn_ref[...]).astype(BF16)
        kn = _dot(ckv, wuk_ref[...])
        kpe = _rope(z[:, ZC_KPE:ZC_KPE + LANES], cos_t, sin_t)
        for hd in range(MLA_HEADS):
            sl = slice(hd * HEAD_SLAB, (hd + 1) * HEAD_SLAB)
            q_ref[0, rs, sl] = (_rope(qall[:, sl], cos_t, sin_t) * qk_scale).astype(BF16)
            k_ref[0, rs, sl] = (kn[:, sl] + kpe).astype(BF16)
        vt_ref[0, :, rs] = (_dot_nt(wuvt_ref[...], ckv) + vone_ref[...]).astype(BF16)

        caq_ref[0, rs, :] = (z[:, ZC_CAQ:ZC_CAQ + CA_WIDTH]
                             * (CA_HEAD_DIM ** -0.5 * LOG2_E)).astype(BF16)
        cak_ref[0, rs, :] = z[:, ZC_CAK:ZC_CAK + CA_WIDTH].astype(BF16)
        cavt = jnp.transpose(z[:, ZC_CAV:ZC_CAV + CA_WIDTH])
        for hd in range(CA_HEADS):
            slab = jnp.concatenate([cavt[hd * CA_HEAD_DIM:(hd + 1) * CA_HEAD_DIM], ones_pad], axis=0)
            cavt_ref[0, hd * V_SLAB:(hd + 1) * V_SLAB, rs] = slab.astype(BF16)
        ga_ref[0, rs, :] = jax.nn.sigmoid(z[:, ZC_GA:ZC_GA + D_MODEL]).astype(BF16)
        gb_ref[0, rs, :] = jax.nn.sigmoid(z[:, ZC_GB:ZC_GB + D_MODEL]).astype(BF16)


def _proj(x, ada, norm, pos, invf, w_in_ext, qn, kvn, wq, wuk, wuvt, vone, tm):
    bsz, s, d = x.shape

    def row(width):
        return pl.BlockSpec((1, tm, width), lambda b, i: (b, i, 0))

    def out(width):
        return jax.ShapeDtypeStruct((bsz, s, width), BF16)

    def col(n_rows):
        return pl.BlockSpec((1, n_rows, tm), lambda b, i: (b, 0, i))

    def out_t(n_rows):
        return jax.ShapeDtypeStruct((bsz, n_rows, s), BF16)

    slabs = MLA_HEADS * HEAD_SLAB
    vt_rows, cavt_rows = MLA_HEADS * MLA_V_SLAB, CA_HEADS * V_SLAB
    out_specs = [row(slabs), row(slabs), col(vt_rows), row(CA_WIDTH), row(CA_WIDTH),
                 col(cavt_rows), row(D_MODEL), row(D_MODEL)]
    out_shape = [out(slabs), out(slabs), out_t(vt_rows), out(CA_WIDTH), out(CA_WIDTH),
                 out_t(cavt_rows), out(D_MODEL), out(D_MODEL)]
    return pl.pallas_call(
        _proj_kernel,
        grid=(bsz, s // tm),
        in_specs=[row(d),
                  pl.BlockSpec((1, 9, d), lambda b, i: (b, 0, 0)),
                  _const_spec((1, d)),
                  pl.BlockSpec((1, 1, 1, tm), lambda b, i: (b, i, 0, 0)),
                  _const_spec(invf.shape),
                  _const_spec(w_in_ext.shape),
                  _const_spec(qn.shape), _const_spec(kvn.shape),
                  _const_spec(wq.shape),
                  _const_spec(wuk.shape), _const_spec(wuvt.shape), _const_spec(vone.shape)],
        out_specs=out_specs,
        out_shape=out_shape,
        compiler_params=pltpu.CompilerParams(
            dimension_semantics=("parallel", "parallel"),
            vmem_limit_bytes=VMEM_LIMIT_BYTES),
        name="proj",
    )(x, ada, norm, pos, invf, w_in_ext, qn, kvn, wq, wuk, wuvt, vone)


MLA_HEADS_PER_STEP = 8
MLA_QSPLIT = 2
MLA_LOOKAHEAD = 3


def _mla_kernel(q_ref, k_ref, vt_ref, o_ref, m_ref, acc_ref, *s_refs, blk, hps):
    qi = pl.program_id(2)
    m_ref[...] = jnp.full(m_ref.shape, NEG_INF, F32)
    acc_ref[...] = jnp.zeros(acc_ref.shape, F32)

    nq = blk // MLA_QSPLIT
    units = [(hd, qh) for hd in range(hps) for qh in range(MLA_QSPLIT)]

    def step(j, mask):
        k0 = pl.multiple_of(j * blk, blk)

        def n_keys(qh):
            return blk if mask is None else (qh + 1) * nq

        def scores(u):
            hd, qh = units[u]
            sl = slice(hd * HEAD_SLAB, (hd + 1) * HEAD_SLAB)
            qs = slice(qh * nq, (qh + 1) * nq)
            nk = n_keys(qh)
            s = _dot_nt(k_ref[0, pl.ds(k0, nk), sl], q_ref[0, qs, sl])
            if mask is not None:
                s = jnp.where(mask[:nk, qs], s, NEG_INF)
            s_refs[u % len(s_refs)][:nk, :] = s
            return jnp.max(s, axis=0, keepdims=True)

        block_max = {u: scores(u) for u in range(min(MLA_LOOKAHEAD, len(units)))}
        for u, (hd, qh) in enumerate(units):
            if u + MLA_LOOKAHEAD < len(units):
                block_max[u + MLA_LOOKAHEAD] = scores(u + MLA_LOOKAHEAD)
            qs = slice(qh * nq, (qh + 1) * nq)
            nk = n_keys(qh)
            s = s_refs[u % len(s_refs)][:nk, :]
            m_prev = m_ref[hd, :, qs]
            m_new = jnp.maximum(m_prev, block_max.pop(u))
            m_ref[hd, :, qs] = m_new
            alpha = jnp.exp2(m_prev - m_new)
            p = jnp.exp2((s - m_new).astype(BF16))
            vt = vt_ref[0, hd * MLA_V_SLAB:(hd + 1) * MLA_V_SLAB, pl.ds(k0, nk)]
            acc_ref[hd, :, qs] = acc_ref[hd, :, qs] * alpha + _dot(vt, p)

    def body(j, carry):
        step(j, None)
        return carry

    lax.fori_loop(0, qi, body, 0)
    kc = lax.broadcasted_iota(jnp.int32, (blk, blk), 0) // CHUNK
    qc = lax.broadcasted_iota(jnp.int32, (blk, blk), 1) // CHUNK
    step(qi, kc <= qc)
    for pr in range(hps // 2):
        outs = []
        for hd in (2 * pr, 2 * pr + 1):
            a = acc_ref[hd]
            outs.append(a[:MLA_V] * (1.0 / a[MLA_V:MLA_V + 1]))
        o_ref[0, :, pr * LANES:(pr + 1) * LANES] = jnp.transpose(
            jnp.concatenate(outs, axis=0)).astype(BF16)


def _mla(q, k, vt, blk):
    bsz, s, _ = q.shape
    hps = MLA_HEADS_PER_STEP
    return pl.pallas_call(
        functools.partial(_mla_kernel, blk=blk, hps=hps),
        grid=(bsz, MLA_HEADS // hps, s // blk),
        in_specs=[pl.BlockSpec((1, blk, hps * HEAD_SLAB), lambda b, g, i: (b, i, g)),
                  pl.BlockSpec((1, s, hps * HEAD_SLAB), lambda b, g, i: (b, 0, g)),
                  pl.BlockSpec((1, hps * MLA_V_SLAB, s), lambda b, g, i: (b, g, 0))],
        out_specs=pl.BlockSpec((1, blk, hps * MLA_V), lambda b, g, i: (b, i, g)),
        out_shape=jax.ShapeDtypeStruct((bsz, s, MLA_OUT_WIDTH), BF16),
        scratch_shapes=[pltpu.VMEM((hps, 1, blk), F32),
                        pltpu.VMEM((hps, MLA_V_SLAB, blk), F32)]
                       + [pltpu.VMEM((blk, blk // MLA_QSPLIT), F32)] * (MLA_LOOKAHEAD + 1),
        compiler_params=pltpu.CompilerParams(
            dimension_semantics=("parallel", "parallel", "arbitrary"),
            vmem_limit_bytes=VMEM_LIMIT_BYTES),
        name="mla",
    )(q, k, vt)


CA_TQ = 256
CA_TK = CA_TQ + CA_LEFT_CHUNKS * CHUNK
CA_KBLKS = CA_TK // CA_TQ
CA_TAB = 1024
CA_LOOKAHEAD = 4


def _ca_kernel(tab_ref, q_ref, k_ref, vt_ref, o_ref, bias_ref, *s_refs):
    b, i = pl.program_id(0), pl.program_id(1)

    @pl.when((b == 0) & (i == 0))
    def _build_bias():
        row = lax.broadcasted_iota(jnp.int32, (CA_TQ, CA_TAB), 0)
        kc = lax.broadcasted_iota(jnp.int32, (CA_TK, CA_TQ), 0) // CHUNK
        qc = lax.broadcasted_iota(jnp.int32, (CA_TK, CA_TQ), 1) // CHUNK
        band = (kc >= qc) & (kc <= qc + CA_LEFT_CHUNKS)
        for hd in range(CA_HEADS):
            t = jnp.broadcast_to(tab_ref[hd:hd + 1, :], (CA_TQ, CA_TAB))
            for bit in range(CA_TQ.bit_length() - 1):
                t = jnp.where(((row >> bit) & 1) == 1, pltpu.roll(t, 1 << bit, axis=1), t)
            bias_ref[hd] = jnp.where(band, jnp.transpose(t[:, :CA_TK]) * LOG2_E, NEG_INF)

    lo = lax.broadcasted_iota(jnp.int32, (1, LANES), 1) < CA_HEAD_DIM

    def attend(nkb):
        nk = nkb * CA_TQ
        k0 = pl.multiple_of((i - (nkb - 1)) * CA_TQ, CA_TQ)

        def scores(hd):
            sl = slice((hd // 2) * LANES, (hd // 2 + 1) * LANES)
            q = q_ref[0, :, sl]
            zero = jnp.zeros_like(q)
            qh = jnp.where(lo, q, zero) if hd % 2 == 0 else jnp.where(lo, zero, q)
            s = _dot_nt(k_ref[0, pl.ds(k0, nk), sl], qh) + bias_ref[hd, CA_TK - nk:, :]
            s_refs[hd % len(s_refs)][:nk, :] = s
            return jnp.max(s, axis=0, keepdims=True)

        col_max = {hd: scores(hd) for hd in range(CA_LOOKAHEAD)}
        outs = []
        for hd in range(CA_HEADS):
            if hd + CA_LOOKAHEAD < CA_HEADS:
                col_max[hd + CA_LOOKAHEAD] = scores(hd + CA_LOOKAHEAD)
            s = s_refs[hd % len(s_refs)][:nk, :]
            p = jnp.exp2((s - col_max.pop(hd)).astype(BF16))
            vt = vt_ref[0, hd * V_SLAB:(hd + 1) * V_SLAB, pl.ds(k0, nk)]
            a = _dot(vt, p)
            outs.append(a[:CA_HEAD_DIM] * (1.0 / a[CA_HEAD_DIM:CA_HEAD_DIM + 1]))
            if hd % 2 == 1:
                pr = hd // 2
                o_ref[0, :, pr * LANES:(pr + 1) * LANES] = jnp.transpose(
                    jnp.concatenate(outs, axis=0)).astype(BF16)
                outs = []

    for nkb in range(1, CA_KBLKS):
        pl.when(i == nkb - 1)(functools.partial(attend, nkb))
    pl.when(i >= CA_KBLKS - 1)(functools.partial(attend, CA_KBLKS))


def _ca(tab, q, k, vt):
    bsz, s, _ = q.shape
    blk = pl.BlockSpec((1, CA_TQ, CA_WIDTH), lambda b, i: (b, i, 0))
    return pl.pallas_call(
        _ca_kernel,
        grid=(bsz, s // CA_TQ),
        in_specs=[_const_spec(tab.shape), blk,
                  pl.BlockSpec((1, s, CA_WIDTH), lambda b, i: (b, 0, 0)),
                  pl.BlockSpec((1, CA_HEADS * V_SLAB, s), lambda b, i: (b, 0, 0))],
        out_specs=blk,
        out_shape=jax.ShapeDtypeStruct((bsz, s, CA_WIDTH), BF16),
        scratch_shapes=[pltpu.VMEM((CA_HEADS, CA_TK, CA_TQ), F32)]
                       + [pltpu.VMEM((CA_TK, CA_TQ), F32)] * (CA_LOOKAHEAD + 1),
        compiler_params=pltpu.CompilerParams(
            dimension_semantics=("arbitrary", "arbitrary"),
            vmem_limit_bytes=VMEM_LIMIT_BYTES),
        name="ca",
    )(tab, q, k, vt)


def _tail_kernel(x_ref, ada_ref, aa_ref, ab_ref, ga_ref, gb_ref, wa_ref, wb_ref, wo_ref,
                 norm_ref, w_in_ref, w_out_ref, fnorm_ref, o_ref):
    ya = _dot(aa_ref[0], wa_ref[...])
    yb = _dot(ab_ref[0], wb_ref[...])
    merged = ga_ref[0] * ya.astype(BF16) + gb_ref[0] * yb.astype(BF16)
    x = x_ref[0] + ada_ref[0, 5:6, :] * _dot(merged, wo_ref[...])
    x = _ffn_body(x, ada_ref, 6, norm_ref, w_in_ref, w_out_ref)
    o_ref[0] = _rmsnorm(x, fnorm_ref[...])


def _tail(x, ada, aa, ab, ga, gb, wa, wb, wo, norm, w_in, w_out, fnorm, tm):
    bsz, s, d = x.shape

    def row(width):
        return pl.BlockSpec((1, tm, width), lambda b, i: (b, i, 0))

    return pl.pallas_call(
        _tail_kernel,
        grid=(bsz, s // tm),
        in_specs=[row(d),
                  pl.BlockSpec((1, 9, d), lambda b, i: (b, 0, 0)),
                  row(MLA_OUT_WIDTH), row(CA_WIDTH), row(d), row(d),
                  _const_spec(wa.shape), _const_spec(wb.shape), _const_spec(wo.shape),
                  _const_spec((1, d)), _const_spec(w_in.shape), _const_spec(w_out.shape),
                  _const_spec((1, d))],
        out_specs=row(d),
        out_shape=jax.ShapeDtypeStruct(x.shape, F32),
        compiler_params=pltpu.CompilerParams(
            dimension_semantics=("parallel", "parallel"),
            vmem_limit_bytes=VMEM_LIMIT_BYTES),
        name="tail",
    )(x, ada, aa, ab, ga, gb, wa, wb, wo, norm, w_in, w_out, fnorm)


def _rotate_half_cols(w):
    half = w.shape[-1] // 2
    return jnp.concatenate([-w[..., half:], w[..., :half]], axis=-1)


def _prep_w_in(w_in):
    d = w_in.shape[0]
    o = 0
    pieces = {}
    for name, width in (("qlat", MLA_Q_RANK), ("kvlat", MLA_KV_RANK), ("kpe", MLA_ROPE),
                        ("caq", CA_WIDTH), ("cak", CA_WIDTH), ("cav", CA_WIDTH),
                        ("ga", D_MODEL), ("gb", D_MODEL)):
        pieces[name] = w_in[:, o:o + width]
        o += width
    zl = jnp.zeros((d, MLA_NOPE), w_in.dtype)
    kpe = jnp.concatenate([zl, pieces["kpe"], _rotate_half_cols(pieces["kpe"])], axis=1)
    ext = jnp.concatenate([pieces["qlat"].T, pieces["kvlat"].T, kpe.T, pieces["caq"].T,
                           pieces["cak"].T, pieces["cav"].T, pieces["ga"].T, pieces["gb"].T], axis=0)
    assert ext.shape[0] == Z_COLS
    return ext.astype(BF16)


def _prep_w_uq(w_uq):
    r = w_uq.shape[0]
    w = w_uq.reshape(r, MLA_HEADS, MLA_NOPE + MLA_ROPE)
    nope, pe = w[..., :MLA_NOPE], w[..., MLA_NOPE:]
    slab = jnp.concatenate([nope, pe, _rotate_half_cols(pe)], axis=-1)
    return slab.reshape(r, MLA_HEADS * HEAD_SLAB).astype(BF16)


def _prep_w_ukv(w_ukv):
    r = w_ukv.shape[0]
    w = w_ukv.reshape(r, MLA_HEADS, MLA_NOPE + MLA_V)
    k_nope, v = w[..., :MLA_NOPE], w[..., MLA_NOPE:]
    zk = jnp.zeros((r, MLA_HEADS, HEAD_SLAB - MLA_NOPE), w.dtype)
    wuk = jnp.concatenate([k_nope, zk], axis=-1).reshape(r, MLA_HEADS * HEAD_SLAB)
    zv = jnp.zeros((r, MLA_HEADS, MLA_V_SLAB - MLA_V), w.dtype)
    wuvt = jnp.concatenate([v, zv], axis=-1).reshape(r, MLA_HEADS * MLA_V_SLAB).T
    vone = (jnp.arange(MLA_HEADS * MLA_V_SLAB) % MLA_V_SLAB == MLA_V).astype(F32).reshape(-1, 1)
    return wuk.astype(BF16), wuvt.astype(BF16), vone


def _prep_bias_table(rel_bias):
    j = jnp.arange(CA_TAB)
    t = (j + CA_TQ - 1) % CA_TAB
    rel = (CA_TK - 1) - t
    idx = jnp.clip(rel, -MAX_REL_DIST, MAX_REL_DIST) + MAX_REL_DIST
    return rel_bias[idx].T.astype(F32)


def kernel(x, c, positions, w_ada, b_ada, ffn1_norm, ffn1_w_in, ffn1_w_out, mix_norm, w_in,
           mla_q_norm, mla_w_uq, mla_kv_norm, mla_w_ukv, rel_bias, w_branch_a, w_branch_b,
           w_out, ffn2_norm, ffn2_w_in, ffn2_w_out, final_norm):
    bsz, s, d = x.shape
    tm = ROW_TILE
    for l in range(w_ada.shape[0]):
        ada = _ada(c, w_ada[l], b_ada[l]).reshape(bsz, 9, d)
        x = _ffn1(x, ada, ffn1_norm[l].reshape(1, d), ffn1_w_in[l].astype(BF16),
                  ffn1_w_out[l].astype(BF16), tm)

        freq = jnp.arange(0, MLA_ROPE, 2, dtype=F32) / MLA_ROPE
        inv_freq = ROPE_THETA ** (-freq)
        invf = inv_freq.reshape(MLA_ROPE // 2, 1)
        wq = _prep_w_uq(mla_w_uq[l])
        wuk, wuvt, vone = _prep_w_ukv(mla_w_ukv[l])
        q, k, vt, caq, cak, cavt, ga, gb = _proj(
            x, ada, mix_norm[l].reshape(1, d), positions.reshape(bsz, s // tm, 1, tm), invf,
            _prep_w_in(w_in[l]), mla_q_norm[l].reshape(1, -1), mla_kv_norm[l].reshape(1, -1),
            wq, wuk, wuvt, vone, tm)
        attn_a = _mla(q, k, vt, MLA_BLOCK)
        attn_b = _ca(_prep_bias_table(rel_bias[l]), caq, cak, cavt)
        last = l == w_ada.shape[0] - 1
        assert last, "the fused tail applies the final norm; DEPTH is 1"
        x = _tail(x, ada, attn_a, attn_b, ga, gb, w_branch_a[l].astype(BF16),
                  w_branch_b[l].astype(BF16), w_out[l].astype(BF16),
                  ffn2_norm[l].reshape(1, d), ffn2_w_in[l].astype(BF16),
                  ffn2_w_out[l].astype(BF16), final_norm.reshape(1, d), tm)
    return x
```
